```python
import math
import jax, jax.numpy as jnp
from jax import lax
import numpy as np

D_MODEL = 1024
BATCH = 4
SEQ = 8192
DEPTH = 4

CTX_LEN = 256
GRID_W = 64
HEAD_DIM = 64
N_HEADS = D_MODEL // HEAD_DIM
N_KV_HEADS = N_HEADS // 4
N_DIFF_HEADS = N_HEADS // 2
D_FF = ((8 * D_MODEL // 3 + 127) // 128) * 128
WINDOW = 128
Q_BLOCK = 128
ROPE_BASE = 10000.0
EPS = 1e-6
NEG_INF = -1e30
ATTN_SCALE = HEAD_DIM ** -0.5
FFN_RESIDUAL = 0.5
N_MIXERS = 3
N_MOD = 9
N_LAYERS_A = (DEPTH + 2) // 3
N_LAYERS_B = (DEPTH + 1) // 3
N_LAYERS_C = DEPTH // 3
MIX_WIDTH = N_HEADS * HEAD_DIM
QKV_GQA = (N_HEADS + 2 * N_KV_HEADS) * HEAD_DIM
QKV_DIFF = 3 * N_DIFF_HEADS * 2 * HEAD_DIM

kernel_name = "hybrid_interleaved_dit_trunk"


def rms_norm(x, g):
    xf = x.astype(jnp.float32)
    y = xf * lax.rsqrt(jnp.mean(xf * xf, axis=-1, keepdims=True) + EPS)
    return (y * g.astype(jnp.float32)).astype(x.dtype)


def modulate(h, shift, scale):
    return h * (1 + scale) + shift


def axial_rope_tables(n_tokens):
    rows = n_tokens // GRID_W
    row = jnp.repeat(jnp.arange(rows), GRID_W)
    col = jnp.tile(jnp.arange(GRID_W), rows)
    n_freq = HEAD_DIM // 4
    inv_freq = ROPE_BASE ** (-jnp.arange(n_freq, dtype=jnp.float32) / n_freq)
    ang = jnp.stack([row, col], axis=-1).astype(jnp.float32)[:, :, None] * inv_freq
    return jnp.cos(ang), jnp.sin(ang)


def apply_axial_rope(x, cos, sin):
    shape = x.shape
    xr = x.astype(jnp.float32).reshape(shape[0], shape[1], -1, 2, 2, HEAD_DIM // 4)
    x1, x2 = xr[..., 0, :], xr[..., 1, :]
    cs, sn = cos[:, None], sin[:, None]
    out = jnp.stack([x1 * cs - x2 * sn, x2 * cs + x1 * sn], axis=-2)
    return out.reshape(shape).astype(x.dtype)


def softmax_with_sink(s, sink):
    if sink is None:
        return jax.nn.softmax(s, axis=-1)
    m = jnp.maximum(jnp.max(s, axis=-1, keepdims=True), sink)
    e = jnp.exp(s - m)
    return e / (jnp.sum(e, axis=-1, keepdims=True) + jnp.exp(sink - m))


def to_query_blocks(q):
    b, t = q.shape[:2]
    return jnp.moveaxis(q.reshape(b, t // Q_BLOCK, Q_BLOCK, *q.shape[2:]), 1, 0)


def from_query_blocks(o):
    o = jnp.moveaxis(o, 0, 1)
    return o.reshape(o.shape[0], -1, *o.shape[3:])


def ffn_half_step(x, mod, k, g, w_in, w_out):
    h = modulate(rms_norm(x, g), mod[:, 3 * k], mod[:, 3 * k + 1])
    a, u = jnp.split(h @ w_in, 2, axis=-1)
    return x + FFN_RESIDUAL * mod[:, 3 * k + 2] * ((jax.nn.silu(a) * u) @ w_out)


def gqa_project(h, w_qkv, qk_g):
    b, t, _ = h.shape
    q, k, v = jnp.split(h @ w_qkv, [N_HEADS * HEAD_DIM, (N_HEADS + N_KV_HEADS) * HEAD_DIM], axis=-1)
    q = rms_norm(q.reshape(b, t, N_HEADS, HEAD_DIM), qk_g[0])
    k = rms_norm(k.reshape(b, t, N_KV_HEADS, HEAD_DIM), qk_g[1])
    return q, k, v.reshape(b, t, N_KV_HEADS, HEAD_DIM)


def full_gqa_attention(q, k, v, sink=None):
    b, t, h, d = q.shape
    kvh = k.shape[2]
    g = h // kvh
    qb = to_query_blocks(q.reshape(b, t, kvh, g, d))
    sink_f = None if sink is None else sink.astype(jnp.float32).reshape(kvh, g, 1, 1)

    def one_block(qi):
        s = jnp.einsum('bqkgd,bskd->bkgqs', qi, k, preferred_element_type=jnp.float32) * ATTN_SCALE
        p = softmax_with_sink(s, sink_f)
        return jnp.einsum('bkgqs,bskd->bqkgd', p.astype(v.dtype), v)

    return from_query_blocks(lax.map(one_block, qb)).reshape(b, t, h * d)


def window_gqa_attention(q, k, v, k_ctx, v_ctx, sink):
    b, n, h, d = q.shape
    kvh = k.shape[2]
    g = h // kvh
    span = Q_BLOCK + 2 * WINDOW
    pad = ((0, 0), (WINDOW, WINDOW), (0, 0), (0, 0))
    k_pad, v_pad = jnp.pad(k, pad), jnp.pad(v, pad)
    qb = to_query_blocks(q.reshape(b, n, kvh, g, d))
    starts = jnp.arange(n // Q_BLOCK) * Q_BLOCK
    rel = jnp.arange(span)[None, :] - WINDOW - jnp.arange(Q_BLOCK)[:, None]
    in_band = jnp.abs(rel) <= WINDOW
    ctx_ok = jnp.ones((Q_BLOCK, k_ctx.shape[1]), dtype=bool)
    sink_f = sink.astype(jnp.float32).reshape(kvh, g, 1, 1)

    def one_block(args):
        qi, start = args
        kpos = start - WINDOW + jnp.arange(span)
        valid = jnp.concatenate([ctx_ok, in_band & ((kpos >= 0) & (kpos < n))[None, :]], axis=-1)
        kk = jnp.concatenate([k_ctx, lax.dynamic_slice_in_dim(k_pad, start, span, axis=1)], axis=1)
        vv = jnp.concatenate([v_ctx, lax.dynamic_slice_in_dim(v_pad, start, span, axis=1)], axis=1)
        s = jnp.einsum('bqkgd,bskd->bkgqs', qi, kk, preferred_element_type=jnp.float32) * ATTN_SCALE
        p = softmax_with_sink(jnp.where(valid, s, NEG_INF), sink_f)
        return jnp.einsum('bkgqs,bskd->bqkgd', p.astype(vv.dtype), vv)

    return from_query_blocks(lax.map(one_block, (qb, starts))).reshape(b, n, h * d)


def dense_gqa_mixer(h_lat, h_ctx, cos, sin, w_qkv, qk_g, with_ctx_out):
    q, k, v = gqa_project(h_lat, w_qkv, qk_g)
    q, k = apply_axial_rope(q, cos, sin), apply_axial_rope(k, cos, sin)
    qc, kc, vc = gqa_project(h_ctx, w_qkv, qk_g)
    o_lat = full_gqa_attention(q, jnp.concatenate([kc, k], axis=1), jnp.concatenate([vc, v], axis=1))
    o_ctx = full_gqa_attention(qc, kc, vc) if with_ctx_out else None
    return o_lat, o_ctx


def window_gqa_mixer(h_lat, h_ctx, cos, sin, w_qkv, qk_g, sink, with_ctx_out):
    q, k, v = gqa_project(h_lat, w_qkv, qk_g)
    q, k = apply_axial_rope(q, cos, sin), apply_axial_rope(k, cos, sin)
    qc, kc, vc = gqa_project(h_ctx, w_qkv, qk_g)
    o_lat = window_gqa_attention(q, k, v, kc, vc, sink)
    o_ctx = full_gqa_attention(qc, kc, vc, sink) if with_ctx_out else None
    return o_lat, o_ctx


def diff_project(h, w_qkv, qk_g):
    b, t, _ = h.shape
    q, k, v = jnp.split(h @ w_qkv, 3, axis=-1)
    q = rms_norm(q.reshape(b, t, N_DIFF_HEADS, 2, HEAD_DIM), qk_g[0])
    k = rms_norm(k.reshape(b, t, N_DIFF_HEADS, 2, HEAD_DIM), qk_g[1])
    return q, k, v.reshape(b, t, N_DIFF_HEADS, 2 * HEAD_DIM)


def diff_attention(q, k, v, lam):
    qb = to_query_blocks(q)

    def one_block(qi):
        s = jnp.einsum('bqhmd,bshmd->bhmqs', qi, k, preferred_element_type=jnp.float32) * ATTN_SCALE
        p = jax.nn.softmax(s, axis=-1)
        w = p[:, :, 0] - lam * p[:, :, 1]
        return jnp.einsum('bhqs,bshe->bqhe', w.astype(v.dtype), v)

    return from_query_blocks(lax.map(one_block, qb))


def diff_mixer(h_lat, h_ctx, cos, sin, w_qkv, qk_g, lam_params, subln_g, layer_idx, with_ctx_out):
    lam_init = 0.8 - 0.6 * math.exp(-0.3 * layer_idx)
    lp = lam_params.astype(jnp.float32)
    lam = jnp.exp(jnp.sum(lp[0] * lp[1])) - jnp.exp(jnp.sum(lp[2] * lp[3])) + lam_init
    q, k, v = diff_project(h_lat, w_qkv, qk_g)
    q, k = apply_axial_rope(q, cos, sin), apply_axial_rope(k, cos, sin)
    qc, kc, vc = diff_project(h_ctx, w_qkv, qk_g)

    def finish(o):
        o = rms_norm(o, subln_g) * (1 - lam_init)
        return o.reshape(o.shape[0], o.shape[1], -1)

    o_lat = finish(diff_attention(q, jnp.concatenate([kc, k], axis=1), jnp.concatenate([vc, v], axis=1), lam))
    o_ctx = finish(diff_attention(qc, kc, vc, lam)) if with_ctx_out else None
    return o_lat, o_ctx


def setup_inputs(seed: int = 0) -> dict:
    key = jax.random.key(seed)
    ks = jax.random.split(key, 20)

    def nrm(k, shape, scale):
        return jax.random.normal(k, shape, jnp.float32) * scale

    return {
        "x": nrm(ks[0], (BATCH, SEQ, D_MODEL), 1.0),
        "c": nrm(ks[1], (BATCH, D_MODEL), 1.0),
        "ctx": nrm(ks[2], (BATCH, CTX_LEN, D_MODEL), 1.0),
        "c_ctx": nrm(ks[3], (D_MODEL,), 1.0),
        "norm_g": 1.0 + nrm(ks[4], (DEPTH, 3, D_MODEL), 0.02),
        "w_ada": nrm(ks[5], (DEPTH, D_MODEL, N_MOD * D_MODEL), 0.5 * D_MODEL ** -0.5),
        "b_ada": nrm(ks[6], (DEPTH, N_MOD * D_MODEL), 0.01),
        "w_ffn_in": nrm(ks[7], (DEPTH, 2, D_MODEL, 2 * D_FF), D_MODEL ** -0.5),
        "w_ffn_out": nrm(ks[8], (DEPTH, 2, D_FF, D_MODEL), D_FF ** -0.5),
        "w_o": nrm(ks[9], (DEPTH, MIX_WIDTH, D_MODEL), MIX_WIDTH ** -0.5),
        "w_qkv_a": nrm(ks[10], (N_LAYERS_A, D_MODEL, QKV_GQA), D_MODEL ** -0.5),
        "qk_norm_a": 1.0 + nrm(ks[11], (N_LAYERS_A, 2, HEAD_DIM), 0.02),
        "w_qkv_b": nrm(ks[12], (N_LAYERS_B, D_MODEL, QKV_GQA), D_MODEL ** -0.5),
        "qk_norm_b": 1.0 + nrm(ks[13], (N_LAYERS_B, 2, HEAD_DIM), 0.02),
        "sink_b": nrm(ks[14], (N_LAYERS_B, N_HEADS), 1.0),
        "w_qkv_c": nrm(ks[15], (N_LAYERS_C, D_MODEL, QKV_DIFF), D_MODEL ** -0.5),
        "qk_norm_c": 1.0 + nrm(ks[16], (N_LAYERS_C, 2, HEAD_DIM), 0.02),
        "diff_lambda": nrm(ks[17], (N_LAYERS_C, 4, HEAD_DIM), 0.1),
        "diff_subln": 1.0 + nrm(ks[18], (N_LAYERS_C, 2 * HEAD_DIM), 0.02),
    }


def reference(x, c, ctx, c_ctx, norm_g, w_ada, b_ada, w_ffn_in, w_ffn_out, w_o,
              w_qkv_a, qk_norm_a, w_qkv_b, qk_norm_b, sink_b,
              w_qkv_c, qk_norm_c, diff_lambda, diff_subln):
    n_lat = x.shape[1]
    cos, sin = axial_rope_tables(n_lat)
    s_lat = jax.nn.silu(c)
    s_ctx = jax.nn.silu(c_ctx)[None]
    for i in range(DEPTH):
        last = i == DEPTH - 1
        m_lat = (s_lat @ w_ada[i] + b_ada[i]).reshape(-1, N_MOD, 1, D_MODEL)
        m_ctx = (s_ctx @ w_ada[i] + b_ada[i]).reshape(-1, N_MOD, 1, D_MODEL)
        x = ffn_half_step(x, m_lat, 0, norm_g[i, 0], w_ffn_in[i, 0], w_ffn_out[i, 0])
        ctx = ffn_half_step(ctx, m_ctx, 0, norm_g[i, 0], w_ffn_in[i, 0], w_ffn_out[i, 0])
        h_lat = modulate(rms_norm(x, norm_g[i, 1]), m_lat[:, 3], m_lat[:, 4])
        h_ctx = modulate(rms_norm(ctx, norm_g[i, 1]), m_ctx[:, 3], m_ctx[:, 4])
        kind, j = i % N_MIXERS, i // N_MIXERS
        if kind == 0:
            o_lat, o_ctx = dense_gqa_mixer(h_lat, h_ctx, cos, sin, w_qkv_a[j], qk_norm_a[j], not last)
        elif kind == 1:
            o_lat, o_ctx = window_gqa_mixer(h_lat, h_ctx, cos, sin, w_qkv_b[j], qk_norm_b[j], sink_b[j], not last)
        else:
            o_lat, o_ctx = diff_mixer(h_lat, h_ctx, cos, sin, w_qkv_c[j], qk_norm_c[j],
                                      diff_lambda[j], diff_subln[j], i, not last)
        x = x + m_lat[:, 5] * (o_lat @ w_o[i])
        x = ffn_half_step(x, m_lat, 2, norm_g[i, 2], w_ffn_in[i, 1], w_ffn_out[i, 1])
        if not last:
            ctx = ctx + m_ctx[:, 5] * (o_ctx @ w_o[i])
            ctx = ffn_half_step(ctx, m_ctx, 2, norm_g[i, 2], w_ffn_in[i, 1], w_ffn_out[i, 1])
    return x
```

```python
import functools
import math

import jax
import jax.numpy as jnp
from jax import lax
from jax.experimental import pallas as pl
from jax.experimental.pallas import tpu as pltpu

F32 = jnp.float32
BF16 = jnp.bfloat16

HEAD_DIM = 64
N_KV_HEADS = 4
GQA_GROUP = 4
N_DIFF_HEADS = 8
GRID_W = 64
WINDOW = 128
ROPE_BASE = 10000.0
EPS = 1e-6
NEG_INF = -1e30
LOG2E = math.log2(math.e)
Q_SCALE = HEAD_DIM ** -0.5 * LOG2E
FFN_RESIDUAL = 0.5
N_MOD = 9
LANES = 128
MXU_DIM = 256
VMEM_LIMIT = 56 * 1024 * 1024


def _cparams(*sem):
    return pltpu.CompilerParams(dimension_semantics=sem, vmem_limit_bytes=VMEM_LIMIT)


def _silu(a):
    return a * jax.nn.sigmoid(a)


def _norm_mod(x, g, shift, scale):
    y = x * lax.rsqrt(jnp.mean(x * x, axis=-1, keepdims=True) + EPS)
    return (y * g) * (1.0 + scale) + shift


def _ada_kernel(c_ref, w_ref, b_ref, o_ref):
    s = _silu(c_ref[...])
    o_ref[...] = jnp.dot(s, w_ref[...], preferred_element_type=F32) + b_ref[...]


def _ada_all(c8, w_ada, b_ada):
    depth, d, nd = w_ada.shape
    tn = nd // 8
    return pl.pallas_call(
        _ada_kernel,
        grid=(depth, nd // tn),
        in_specs=[
            pl.BlockSpec((8, d), lambda i, j: (0, 0)),
            pl.BlockSpec((None, d, tn), lambda i, j: (i, 0, j)),
            pl.BlockSpec((None, 1, tn), lambda i, j: (i, 0, j)),
        ],
        out_specs=pl.BlockSpec((None, 8, tn), lambda i, j: (i, 0, j)),
        out_shape=jax.ShapeDtypeStruct((depth, 8, nd), F32),
        compiler_params=_cparams("parallel", "parallel"),
        name="adaln",
    )(c8, w_ada, b_ada.reshape(depth, 1, nd))


def _ffn_kernel(*refs, pre, n_f):
    if pre:
        (x_ref, ot_ref, wo_ref, modp_ref, mod_ref, g_ref, wa_ref, wu_ref, wout_ref,
         o_ref, x1_scr, h_scr, acc_scr) = refs
    else:
        x_ref, mod_ref, g_ref, wa_ref, wu_ref, wout_ref, o_ref, h_scr, acc_scr = refs
    f = pl.program_id(1)

    @pl.when(f == 0)
    def _():
        x = x_ref[...]
        if pre:
            y = pl.dot(ot_ref[...], wo_ref[...], trans_a=True)
            x = x + modp_ref[2:3, :] * y
            x1_scr[...] = x
        h = _norm_mod(x, g_ref[...], mod_ref[0:1, :], mod_ref[1:2, :])
        h_scr[...] = h.astype(BF16)
        acc_scr[...] = jnp.zeros_like(acc_scr)

    h = h_scr[...]
    a = jnp.dot(h, wa_ref[...], preferred_element_type=F32)
    u = jnp.dot(h, wu_ref[...], preferred_element_type=F32)
    act = (_silu(a) * u).astype(BF16)
    acc_scr[...] += jnp.dot(act, wout_ref[...], preferred_element_type=F32)

    @pl.when(f == n_f - 1)
    def _():
        x = x1_scr[...] if pre else x_ref[...]
        o_ref[...] = x + (FFN_RESIDUAL * mod_ref[2:3, :]) * acc_scr[...]


def _ffn(x2, mod5, norm_g4, w_in, w_out, layer, half, *, tm, rows_per_batch, ctx_row, tf,
         pre=None):
    r, d = x2.shape
    dff = w_out.shape[2]
    n_f = dff // tf
    tpb = rows_per_batch // tm
    k = 2 * half

    def bidx(i):
        return ctx_row if ctx_row is not None else i // tpb

    def mod_spec(kk):
        return pl.BlockSpec((None, None, None, 3, d),
                            lambda i, f: (layer, bidx(i), kk, 0, 0))

    in_specs = [pl.BlockSpec((tm, d), lambda i, f: (i, 0))]
    args = [x2]
    scratch = []
    if pre is not None:
        ot, wo = pre
        in_specs += [
            pl.BlockSpec((None, d, tm), lambda i, f: (i // tpb, 0, i % tpb)),
            pl.BlockSpec((None, d, d), lambda i, f: (layer, 0, 0)),
            mod_spec(1),
        ]
        args += [ot, wo, mod5]
        scratch.append(pltpu.VMEM((tm, d), F32))
    in_specs += [
        mod_spec(k),
        pl.BlockSpec((None, None, 1, d), lambda i, f: (layer, k, 0, 0)),
        pl.BlockSpec((None, None, d, tf), lambda i, f: (layer, half, 0, f)),
        pl.BlockSpec((None, None, d, tf), lambda i, f: (layer, half, 0, n_f + f)),
        pl.BlockSpec((None, None, tf, d), lambda i, f: (layer, half, f, 0)),
    ]
    args += [mod5, norm_g4, w_in, w_in, w_out]
    scratch += [pltpu.VMEM((tm, d), BF16), pltpu.VMEM((tm, d), F32)]
    return pl.pallas_call(
        functools.partial(_ffn_kernel, pre=pre is not None, n_f=n_f),
        grid=(r // tm, n_f),
        in_specs=in_specs,
        out_specs=pl.BlockSpec((tm, d), lambda i, f: (i, 0)),
        out_shape=jax.ShapeDtypeStruct((r, d), F32),
        scratch_shapes=scratch,
        compiler_params=_cparams("parallel", "arbitrary"),
        name="ffn_pre" if pre is not None else "ffn",
    )(*args)


def _headnorm_rope(z, gain, cos, sin, bd, rope):
    tm, w = z.shape
    lane = lax.broadcasted_iota(jnp.int32, (1, LANES), 1)
    first_half = (lane & 31) < 16
    outs = []
    for j in range(w // MXU_DIM):
        zj = z[:, j * MXU_DIM:(j + 1) * MXU_DIM]
        sq = zj * zj
        hi = sq.astype(BF16)
        lo = (sq - hi.astype(F32)).astype(BF16)
        ss = (jnp.dot(hi, bd, preferred_element_type=F32)
              + jnp.dot(lo, bd, preferred_element_type=F32))
        zn = zj * lax.rsqrt(ss * (1.0 / HEAD_DIM) + EPS)
        for half in range(MXU_DIM // LANES):
            t = zn[:, half * LANES:(half + 1) * LANES] * gain
            if rope:
                partner = jnp.where(first_half, pltpu.roll(t, LANES - 16, 1),
                                    pltpu.roll(t, 16, 1))
                t = t * cos + partner * sin
            outs.append(t)
    return jnp.concatenate(outs, axis=1)


def _proj_kernel(*refs, rope):
    if rope:
        (x_ref, mod_ref, g_ref, wq_ref, wk_ref, wvt_ref, gq_ref, gk_ref, bd_ref, cos_ref,
         sin_ref, q_ref, k_ref, vt_ref) = refs
        cos, sin = cos_ref[...], sin_ref[...]
    else:
        (x_ref, mod_ref, g_ref, wq_ref, wk_ref, wvt_ref, gq_ref, gk_ref, bd_ref,
         q_ref, k_ref, vt_ref) = refs
        cos = sin = None
    h = _norm_mod(x_ref[...], g_ref[...], mod_ref[0:1, :], mod_ref[1:2, :]).astype(BF16)
    bd = bd_ref[...]
    q = jnp.dot(h, wq_ref[...], preferred_element_type=F32)
    q_ref[...] = _headnorm_rope(q, gq_ref[...] * Q_SCALE, cos, sin, bd, rope).astype(BF16)
    k = jnp.dot(h, wk_ref[...], preferred_element_type=F32)
    k_ref[...] = _headnorm_rope(k, gk_ref[...], cos, sin, bd, rope).astype(BF16)
    vt_ref[...] = pl.dot(wvt_ref[...], h, trans_b=True).astype(BF16)


def _proj(x2, mod5, norm_g4, layer, wq, wk, wvt, gq, gk, bd, rope_tabs, *, tm, rows_per_batch,
          ctx_row):
    r, d = x2.shape
    wq_n, wk_n, wv_n = wq.shape[1], wk.shape[1], wvt.shape[0]
    tpb = rows_per_batch // tm
    nb = r // rows_per_batch
    rope = rope_tabs is not None

    def bidx(i):
        return ctx_row if ctx_row is not None else i // tpb

    const = lambda i: (0, 0)
    in_specs = [
        pl.BlockSpec((tm, d), lambda i: (i, 0)),
        pl.BlockSpec((None, None, None, 3, d), lambda i: (layer, bidx(i), 1, 0, 0)),
        pl.BlockSpec((None, None, 1, d), lambda i: (layer, 1, 0, 0)),
        pl.BlockSpec((d, wq_n), const),
        pl.BlockSpec((d, wk_n), const),
        pl.BlockSpec((wv_n, d), const),
        pl.BlockSpec((1, LANES), const),
        pl.BlockSpec((1, LANES), const),
        pl.BlockSpec((MXU_DIM, MXU_DIM), const),
    ]
    args = [x2, mod5, norm_g4, wq, wk, wvt, gq, gk, bd]
    if rope:
        in_specs += [pl.BlockSpec((tm, LANES), lambda i: (i % tpb, 0))] * 2
        args += list(rope_tabs)
    return pl.pallas_call(
        functools.partial(_proj_kernel, rope=rope),
        grid=(r // tm,),
        in_specs=in_specs,
        out_specs=[
            pl.BlockSpec((tm, wq_n), lambda i: (i, 0)),
            pl.BlockSpec((tm, wk_n), lambda i: (i, 0)),
            pl.BlockSpec((None, wv_n, tm), lambda i: (i // tpb, 0, i % tpb)),
        ],
        out_shape=[
            jax.ShapeDtypeStruct((r, wq_n), BF16),
            jax.ShapeDtypeStruct((r, wk_n), BF16),
            jax.ShapeDtypeStruct((nb, wv_n, rows_per_batch), BF16),
        ],
        compiler_params=_cparams("parallel"),
        name="qkv_proj",
    )(*args)


def _slot_mask(slot):
    lane = lax.broadcasted_iota(jnp.int32, (1, LANES), 1)
    return (lane >= slot * HEAD_DIM) & (lane < (slot + 1) * HEAD_DIM)


def _online_update(s, vt, m_ref, l_ref, acc_ref, idx):
    m_prev = m_ref[idx]
    m_new = jnp.maximum(m_prev, jnp.max(s, axis=0, keepdims=True))
    p = jnp.exp2(s - m_new)
    alpha = jnp.exp2(m_prev - m_new)
    l_ref[idx] = alpha * l_ref[idx] + jnp.sum(p, axis=0, keepdims=True)
    acc_ref[idx] = alpha * acc_ref[idx] + jnp.dot(vt, p.astype(BF16),
                                                  preferred_element_type=F32)
    m_ref[idx] = m_new


def _gqa_q_tile(q_ref, kvh, g):
    j = g + GQA_GROUP * (kvh // 2)
    tile = q_ref[:, j * LANES:(j + 1) * LANES]
    return jnp.where(_slot_mask(kvh % 2), tile, jnp.zeros_like(tile))


def _dense_kernel(*refs, tq, has_lat, has_sink):
    refs = list(refs)
    q_ref, kc_ref, vtc_ref = refs[:3]
    pos = 3
    if has_lat:
        kl_ref, vtl_ref = refs[pos:pos + 2]
        pos += 2
    if has_sink:
        sink_ref = refs[pos]
        pos += 1
    o_ref, qs_scr, m_scr, l_scr, acc_scr = refs[pos:]
    ki = pl.program_id(2)
    n_kv = pl.num_programs(2)

    @pl.when(ki == 0)
    def _():
        for kvh in range(N_KV_HEADS):
            for g in range(GQA_GROUP):
                qs_scr[kvh, g * tq:(g + 1) * tq, :] = _gqa_q_tile(q_ref, kvh, g)
        m_scr[...] = jnp.full_like(m_scr, NEG_INF)
        l_scr[...] = jnp.zeros_like(l_scr)
        acc_scr[...] = jnp.zeros_like(acc_scr)

    def process(k_ref, vt_ref):
        for kvh in range(N_KV_HEADS):
            half = kvh // 2
            kh = k_ref[:, half * LANES:(half + 1) * LANES]
            s = pl.dot(kh, qs_scr[kvh], trans_b=True)
            vt = vt_ref[kvh * HEAD_DIM:(kvh + 1) * HEAD_DIM, :]
            _online_update(s, vt, m_scr, l_scr, acc_scr, kvh)

    if has_lat:
        @pl.when(ki == 0)
        def _():
            process(kc_ref, vtc_ref)

        @pl.when(ki > 0)
        def _():
            process(kl_ref, vtl_ref)
    else:
        process(kc_ref, vtc_ref)

    @pl.when(ki == n_kv - 1)
    def _():
        for kvh in range(N_KV_HEADS):
            m, l, acc = m_scr[kvh], l_scr[kvh], acc_scr[kvh]
            if has_sink:
                sk = sink_ref[kvh] * LOG2E
                m_f = jnp.maximum(m, sk)
                w = jnp.exp2(m - m_f)
                l = l * w + jnp.exp2(sk - m_f)
                acc = acc * w
            o = acc / l
            for g in range(GQA_GROUP):
                row = (kvh * GQA_GROUP + g) * HEAD_DIM
                o_ref[row:row + HEAD_DIM, :] = o[:, g * tq:(g + 1) * tq].astype(BF16)


def _dense_attn(q, kc, vtc, kl, vtl, sink_rows, *, n_batch, tq, tk):
    r, dq = q.shape
    t_q = r // n_batch
    c = kc.shape[0] // n_batch
    wk = kc.shape[1]
    has_lat = kl is not None
    has_sink = sink_rows is not None
    nq = t_q // tq
    n_lat = (kl.shape[0] // n_batch) // tk if has_lat else 0
    in_specs = [
        pl.BlockSpec((tq, dq), lambda b, i, k: (b * nq + i, 0)),
        pl.BlockSpec((c, wk), lambda b, i, k: (b, 0)),
        pl.BlockSpec((None, wk, c), lambda b, i, k: (b, 0, 0)),
    ]
    args = [q, kc, vtc]
    if has_lat:
        in_specs += [
            pl.BlockSpec((tk, wk), lambda b, i, k: (b * n_lat + jnp.maximum(k - 1, 0), 0)),
            pl.BlockSpec((None, wk, tk), lambda b, i, k: (b, 0, jnp.maximum(k - 1, 0))),
        ]
        args += [kl, vtl]
    if has_sink:
        in_specs.append(pl.BlockSpec((N_KV_HEADS, 1, GQA_GROUP * tq), lambda b, i, k: (0, 0, 0)))
        args.append(sink_rows)
    cols = GQA_GROUP * tq
    return pl.pallas_call(
        functools.partial(_dense_kernel, tq=tq, has_lat=has_lat, has_sink=has_sink),
        grid=(n_batch, nq, 1 + n_lat),
        in_specs=in_specs,
        out_specs=pl.BlockSpec((None, dq, tq), lambda b, i, k: (b, 0, i)),
        out_shape=jax.ShapeDtypeStruct((n_batch, dq, t_q), BF16),
        scratch_shapes=[
            pltpu.VMEM((N_KV_HEADS, cols, LANES), BF16),
            pltpu.VMEM((N_KV_HEADS, 1, cols), F32),
            pltpu.VMEM((N_KV_HEADS, 1, cols), F32),
            pltpu.VMEM((N_KV_HEADS, HEAD_DIM, cols), F32),
        ],
        compiler_params=_cparams("parallel", "parallel", "arbitrary"),
        name="dense_attn",
    )(*args)


def _window_kernel(q_ref, kc_ref, vtc_ref, kp_ref, kcur_ref, kn_ref, vtp_ref, vtcur_ref,
                   vtn_ref, sink_ref, o_ref, *, tq):
    qi = pl.program_id(1)
    nq = pl.num_programs(1)
    c = kc_ref.shape[0]
    n_lat = tq + 2 * WINDOW
    cols = GQA_GROUP * tq
    rr = lax.broadcasted_iota(jnp.int32, (n_lat, 1), 0)
    cc = lax.broadcasted_iota(jnp.int32, (1, cols), 1) & (tq - 1)
    rel = rr - WINDOW - cc
    valid = (jnp.abs(rel) <= WINDOW)
    valid &= (rr >= WINDOW) | (qi > 0)
    valid &= (rr < tq + WINDOW) | (qi < nq - 1)
    for kvh in range(N_KV_HEADS):
        half = kvh // 2
        sl = slice(half * LANES, (half + 1) * LANES)
        qs = jnp.concatenate([_gqa_q_tile(q_ref, kvh, g) for g in range(GQA_GROUP)], axis=0)
        s_ctx = pl.dot(kc_ref[:, sl], qs, trans_b=True)
        k_lat = jnp.concatenate([kp_ref[:, sl], kcur_ref[:, sl], kn_ref[:, sl]], axis=0)
        s_lat = jnp.where(valid, pl.dot(k_lat, qs, trans_b=True), NEG_INF)
        sk = sink_ref[kvh] * LOG2E
        m = jnp.maximum(jnp.maximum(jnp.max(s_ctx, axis=0, keepdims=True),
                                    jnp.max(s_lat, axis=0, keepdims=True)), sk)
        p_ctx = jnp.exp2(s_ctx - m)
        p_lat = jnp.exp2(s_lat - m)
        l = (jnp.sum(p_ctx, axis=0, keepdims=True) + jnp.sum(p_lat, axis=0, keepdims=True)
             + jnp.exp2(sk - m))
        hs = slice(kvh * HEAD_DIM, (kvh + 1) * HEAD_DIM)
        vt_lat = jnp.concatenate([vtp_ref[hs, :], vtcur_ref[hs, :], vtn_ref[hs, :]], axis=1)
        acc = (jnp.dot(vtc_ref[hs, :], p_ctx.astype(BF16), preferred_element_type=F32)
               + jnp.dot(vt_lat, p_lat.astype(BF16), preferred_element_type=F32))
        o = acc / l
        for g in range(GQA_GROUP):
            row = (kvh * GQA_GROUP + g) * HEAD_DIM
            o_ref[row:row + HEAD_DIM, :] = o[:, g * tq:(g + 1) * tq].astype(BF16)


def _window_attn(q, kc, vtc, kl, vtl, sink_rows, *, n_batch, tq):
    r, dq = q.shape
    t = r // n_batch
    c = kc.shape[0] // n_batch
    wk = kc.shape[1]
    nq = t // tq
    rb = tq // WINDOW
    nwb = t // WINDOW
    prev = lambda b, i: jnp.maximum(i * rb - 1, 0)
    nxt = lambda b, i: jnp.minimum((i + 1) * rb, nwb - 1)
    in_specs = [
        pl.BlockSpec((tq, dq), lambda b, i: (b * nq + i, 0)),
        pl.BlockSpec((c, wk), lambda b, i: (b, 0)),
        pl.BlockSpec((None, wk, c), lambda b, i: (b, 0, 0)),
        pl.BlockSpec((WINDOW, wk), lambda b, i: (b * nwb + prev(b, i), 0)),
        pl.BlockSpec((tq, wk), lambda b, i: (b * nq + i, 0)),
        pl.BlockSpec((WINDOW, wk), lambda b, i: (b * nwb + nxt(b, i), 0)),
        pl.BlockSpec((None, wk, WINDOW), lambda b, i: (b, 0, prev(b, i))),
        pl.BlockSpec((None, wk, tq), lambda b, i: (b, 0, i)),
        pl.BlockSpec((None, wk, WINDOW), lambda b, i: (b, 0, nxt(b, i))),
        pl.BlockSpec((N_KV_HEADS, 1, GQA_GROUP * tq), lambda b, i: (0, 0, 0)),
    ]
    return pl.pallas_call(
        functools.partial(_window_kernel, tq=tq),
        grid=(n_batch, nq),
        in_specs=in_specs,
        out_specs=pl.BlockSpec((None, dq, tq), lambda b, i: (b, 0, i)),
        out_shape=jax.ShapeDtypeStruct((n_batch, dq, t), BF16),
        compiler_params=_cparams("parallel", "parallel"),
        name="window_attn",
    )(q, kc, vtc, kl, kl, kl, vtl, vtl, vtl, sink_rows)


def _diff_kernel(*refs, tq, has_lat, lam_init):
    refs = list(refs)
    q_ref, kc_ref, vtc_ref = refs[:3]
    pos = 3
    if has_lat:
        kl_ref, vtl_ref = refs[pos:pos + 2]
        pos += 2
    lam_ref, subln_ref, o_ref, qs_scr, m_scr, l_scr, acc_scr = refs[pos:]
    ki = pl.program_id(2)
    n_kv = pl.num_programs(2)
    dv = 2 * HEAD_DIM

    @pl.when(ki == 0)
    def _():
        for h in range(N_DIFF_HEADS):
            tile = q_ref[:, h * LANES:(h + 1) * LANES]
            for mm in range(2):
                qs_scr[h, mm * tq:(mm + 1) * tq, :] = jnp.where(
                    _slot_mask(mm), tile, jnp.zeros_like(tile))
        m_scr[...] = jnp.full_like(m_scr, NEG_INF)
        l_scr[...] = jnp.zeros_like(l_scr)
        acc_scr[...] = jnp.zeros_like(acc_scr)

    def process(k_ref, vt_ref):
        for h in range(N_DIFF_HEADS):
            kh = k_ref[:, h * LANES:(h + 1) * LANES]
            s = pl.dot(kh, qs_scr[h], trans_b=True)
            _online_update(s, vt_ref[h * dv:(h + 1) * dv, :], m_scr, l_scr, acc_scr, h)

    if has_lat:
        @pl.when(ki == 0)
        def _():
            process(kc_ref, vtc_ref)

        @pl.when(ki > 0)
        def _():
            process(kl_ref, vtl_ref)
    else:
        process(kc_ref, vtc_ref)

    @pl.when(ki == n_kv - 1)
    def _():
        lp = lam_ref[...]
        lam = (jnp.exp(jnp.sum(lp[0:1] * lp[1:2], axis=1, keepdims=True))
               - jnp.exp(jnp.sum(lp[2:3] * lp[3:4], axis=1, keepdims=True)) + lam_init)
        gain = subln_ref[...]
        for h in range(N_DIFF_HEADS):
            o12 = acc_scr[h] / l_scr[h]
            o = o12[:, :tq] - lam * o12[:, tq:]
            o = o * lax.rsqrt(jnp.mean(o * o, axis=0, keepdims=True) + EPS)
            o_ref[h * dv:(h + 1) * dv, :] = ((o * gain) * (1.0 - lam_init)).astype(BF16)


def _diff_attn(q, kc, vtc, kl, vtl, lam_params, subln, *, n_batch, tq, tk, lam_init):
    r, dq = q.shape
    t_q = r // n_batch
    c = kc.shape[0] // n_batch
    wk = kc.shape[1]
    has_lat = kl is not None
    nq = t_q // tq
    n_lat = (kl.shape[0] // n_batch) // tk if has_lat else 0
    dv = 2 * HEAD_DIM
    in_specs = [
        pl.BlockSpec((tq, dq), lambda b, i, k: (b * nq + i, 0)),
        pl.BlockSpec((c, wk), lambda b, i, k: (b, 0)),
        pl.BlockSpec((None, wk, c), lambda b, i, k: (b, 0, 0)),
    ]
    args = [q, kc, vtc]
    if has_lat:
        in_specs += [
            pl.BlockSpec((tk, wk), lambda b, i, k: (b * n_lat + jnp.maximum(k - 1, 0), 0)),
            pl.BlockSpec((None, wk, tk), lambda b, i, k: (b, 0, jnp.maximum(k - 1, 0))),
        ]
        args += [kl, vtl]
    in_specs += [
        pl.BlockSpec((4, HEAD_DIM), lambda b, i, k: (0, 0)),
        pl.BlockSpec((dv, 1), lambda b, i, k: (0, 0)),
    ]
    args += [lam_params, subln.reshape(dv, 1)]
    return pl.pallas_call(
        functools.partial(_diff_kernel, tq=tq, has_lat=has_lat, lam_init=lam_init),
        grid=(n_batch, nq, 1 + n_lat),
        in_specs=in_specs,
        out_specs=pl.BlockSpec((None, dq, tq), lambda b, i, k: (b, 0, i)),
        out_shape=jax.ShapeDtypeStruct((n_batch, dq, t_q), BF16),
        scratch_shapes=[
            pltpu.VMEM((N_DIFF_HEADS, 2 * tq, LANES), BF16),
            pltpu.VMEM((N_DIFF_HEADS, 1, 2 * tq), F32),
            pltpu.VMEM((N_DIFF_HEADS, 1, 2 * tq), F32),
            pltpu.VMEM((N_DIFF_HEADS, dv, 2 * tq), F32),
        ],
        compiler_params=_cparams("parallel", "parallel", "arbitrary"),
        name="diff_attn",
    )(*args)


def _rope_tables(n_tokens):
    rows = n_tokens // GRID_W
    row = jnp.repeat(jnp.arange(rows), GRID_W)
    col = jnp.tile(jnp.arange(GRID_W), rows)
    n_freq = HEAD_DIM // 4
    inv_freq = ROPE_BASE ** (-jnp.arange(n_freq, dtype=F32) / n_freq)
    ang = jnp.stack([row, col], axis=-1).astype(F32)[:, :, None] * inv_freq
    cos, sin = jnp.cos(ang), jnp.sin(ang)
    cos_h = jnp.concatenate([cos, cos], axis=-1).reshape(n_tokens, HEAD_DIM)
    sin_h = jnp.concatenate([-sin, sin], axis=-1).reshape(n_tokens, HEAD_DIM)
    return jnp.tile(cos_h, (1, 2)), jnp.tile(sin_h, (1, 2))


def _gqa_weights(w_qkv):
    d = w_qkv.shape[0]
    n_q = N_KV_HEADS * GQA_GROUP * HEAD_DIM
    n_kv = N_KV_HEADS * HEAD_DIM
    wq = w_qkv[:, :n_q].reshape(d, N_KV_HEADS // 2, 2, GQA_GROUP, HEAD_DIM)
    wq = wq.transpose(0, 1, 3, 2, 4).reshape(d, n_q)
    wk = w_qkv[:, n_q:n_q + n_kv]
    wvt = w_qkv[:, n_q + n_kv:].T
    return wq.astype(BF16), wk.astype(BF16), wvt.astype(BF16)


def _diff_weights(w_qkv):
    n = w_qkv.shape[1] // 3
    return (w_qkv[:, :n].astype(BF16), w_qkv[:, n:2 * n].astype(BF16),
            w_qkv[:, 2 * n:].T.astype(BF16))


def _lane_gain(g):
    return jnp.tile(g.reshape(1, HEAD_DIM), (1, LANES // HEAD_DIM))


def _sink_rows(sink, tq):
    return jnp.repeat(sink.reshape(N_KV_HEADS, 1, GQA_GROUP, 1), tq, axis=3).reshape(
        N_KV_HEADS, 1, GQA_GROUP * tq)


def kernel(x, c, ctx, c_ctx, norm_g, w_ada, b_ada, w_ffn_in, w_ffn_out, w_o, w_qkv_a, qk_norm_a,
           w_qkv_b, qk_norm_b, sink_b, w_qkv_c, qk_norm_c, diff_lambda, diff_subln):
    n_b, t, d = x.shape
    n_c = ctx.shape[1]
    depth = w_ada.shape[0]
    d_ff = w_ffn_out.shape[2]
    assert n_b + 1 <= 8 and t % 512 == 0 and n_c % LANES == 0 and n_c <= 512

    tm = 512
    tf = d_ff // 2
    tq, tk = 256, 512
    tq_c = n_c

    c8 = jnp.zeros((8, d), F32).at[:n_b].set(c).at[n_b].set(c_ctx)
    mod5 = _ada_all(c8, w_ada, b_ada).reshape(depth, 8, 3, 3, d)

    rope_tabs = _rope_tables(t)
    eye = jnp.arange(MXU_DIM) // HEAD_DIM
    bd = (eye[:, None] == eye[None, :]).astype(BF16)
    w_in = w_ffn_in.astype(BF16)
    w_out = w_ffn_out.astype(BF16)
    wo = w_o.astype(BF16)
    norm_g4 = norm_g.reshape(depth, 3, 1, d)

    xs = x.reshape(n_b * t, d)
    cs = ctx.reshape(n_b * n_c, d)
    lat = dict(tm=tm, rows_per_batch=t, ctx_row=None)
    cx = dict(tm=n_c, rows_per_batch=n_c, ctx_row=n_b)

    for i in range(depth):
        last = i == depth - 1
        kind, j = i % 3, i // 3
        xs = _ffn(xs, mod5, norm_g4, w_in, w_out, i, 0, tf=tf, **lat)
        cs = _ffn(cs, mod5, norm_g4, w_in, w_out, i, 0, tf=tf, **cx)
        if kind == 2:
            wq, wk, wvt = _diff_weights(w_qkv_c[j])
            qk_g = qk_norm_c[j]
        else:
            wq, wk, wvt = _gqa_weights((w_qkv_a, w_qkv_b)[kind][j])
            qk_g = (qk_norm_a, qk_norm_b)[kind][j]
        gq, gk = _lane_gain(qk_g[0]), _lane_gain(qk_g[1])
        q, kl, vtl = _proj(xs, mod5, norm_g4, i, wq, wk, wvt, gq, gk, bd, rope_tabs, **lat)
        qc, kc, vtc = _proj(cs, mod5, norm_g4, i, wq, wk, wvt, gq, gk, bd, None, **cx)
        if kind == 0:
            ot = _dense_attn(q, kc, vtc, kl, vtl, None, n_batch=n_b, tq=tq, tk=tk)
            if not last:
                otc = _dense_attn(qc, kc, vtc, None, None, None, n_batch=n_b, tq=tq_c, tk=tk)
        elif kind == 1:
            ot = _window_attn(q, kc, vtc, kl, vtl, _sink_rows(sink_b[j], tq), n_batch=n_b, tq=tq)
            if not last:
                otc = _dense_attn(qc, kc, vtc, None, None, _sink_rows(sink_b[j], tq_c),
                                  n_batch=n_b, tq=tq_c, tk=tk)
        else:
            lam_init = 0.8 - 0.6 * math.exp(-0.3 * i)
            ot = _diff_attn(q, kc, vtc, kl, vtl, diff_lambda[j], diff_subln[j], n_batch=n_b,
                            tq=tq, tk=tk, lam_init=lam_init)
            if not last:
                otc = _diff_attn(qc, kc, vtc, None, None, diff_lambda[j], diff_subln[j],
                                 n_batch=n_b, tq=tq_c, tk=tk, lam_init=lam_init)
        xs = _ffn(xs, mod5, norm_g4, w_in, w_out, i, 1, tf=tf, pre=(ot, wo), **lat)
        if not last:
            cs = _ffn(cs, mod5, norm_g4, w_in, w_out, i, 1, tf=tf, pre=(otc, wo), **cx)
    return xs.reshape(n_b, t, d)
```

```python
import functools
import math

import jax
import jax.numpy as jnp
from jax import lax
from jax.experimental import pallas as pl
from jax.experimental.pallas import tpu as pltpu

F32 = jnp.float32
BF16 = jnp.bfloat16

HEAD_DIM = 64
N_KV_HEADS = 4
GQA_GROUP = 4
N_DIFF_HEADS = 8
GRID_W = 64
WINDOW = 128
ROPE_BASE = 10000.0
EPS = 1e-6
NEG_INF = -1e30
LOG2E = math.log2(math.e)
Q_SCALE = HEAD_DIM ** -0.5 * LOG2E
FFN_RESIDUAL = 0.5
N_MOD = 9
LANES = 128
MXU_DIM = 256
VMEM_LIMIT = 56 * 1024 * 1024


def _cparams(*sem):
    return pltpu.CompilerParams(dimension_semantics=sem, vmem_limit_bytes=VMEM_LIMIT)


def _silu(a):
    return a * jax.nn.sigmoid(a)


def _norm_mod(x, g, shift, scale):
    y = x * lax.rsqrt(jnp.mean(x * x, axis=-1, keepdims=True) + EPS)
    return (y * g) * (1.0 + scale) + shift


def _ada_kernel(c_ref, w_ref, b_ref, o_ref):
    s = _silu(c_ref[...])
    o_ref[...] = jnp.dot(s, w_ref[...], preferred_element_type=F32) + b_ref[...]


def _ada_all(c8, w_ada, b_ada):
    depth, d, nd = w_ada.shape
    tn = nd // 8
    return pl.pallas_call(
        _ada_kernel,
        grid=(depth, nd // tn),
        in_specs=[
            pl.BlockSpec((8, d), lambda i, j: (0, 0)),
            pl.BlockSpec((None, d, tn), lambda i, j: (i, 0, j)),
            pl.BlockSpec((None, 1, tn), lambda i, j: (i, 0, j)),
        ],
        out_specs=pl.BlockSpec((None, 8, tn), lambda i, j: (i, 0, j)),
        out_shape=jax.ShapeDtypeStruct((depth, 8, nd), F32),
        compiler_params=_cparams("parallel", "parallel"),
        name="adaln",
    )(c8, w_ada, b_ada.reshape(depth, 1, nd))


def _ffn_kernel(*refs, pre, n_f):
    if pre:
        (x_ref, ot_ref, wo_ref, modp_ref, mod_ref, g_ref, wa_ref, wu_ref, wout_ref,
         o_ref, x1_scr, h_scr, acc_scr) = refs
    else:
        x_ref, mod_ref, g_ref, wa_ref, wu_ref, wout_ref, o_ref, h_scr, acc_scr = refs
    f = pl.program_id(1)

    @pl.when(f == 0)
    def _():
        x = x_ref[...]
        if pre:
            y = pl.dot(ot_ref[...], wo_ref[...], trans_a=True)
            x = x + modp_ref[2:3, :] * y
            x1_scr[...] = x
        h = _norm_mod(x, g_ref[...], mod_ref[0:1, :], mod_ref[1:2, :])
        h_scr[...] = h.astype(BF16)
        acc_scr[...] = jnp.zeros_like(acc_scr)

    h = h_scr[...]
    a = jnp.dot(h, wa_ref[...], preferred_element_type=F32)
    u = jnp.dot(h, wu_ref[...], preferred_element_type=F32)
    act = (_silu(a) * u).astype(BF16)
    acc_scr[...] += jnp.dot(act, wout_ref[...], preferred_element_type=F32)

    @pl.when(f == n_f - 1)
    def _():
        x = x1_scr[...] if pre else x_ref[...]
        o_ref[...] = x + (FFN_RESIDUAL * mod_ref[2:3, :]) * acc_scr[...]


def _ffn(x2, mod5, norm_g4, w_in, w_out, layer, half, *, tm, rows_per_batch, ctx_row, tf,
         pre=None):
    r, d = x2.shape
    dff = w_out.shape[2]
    n_f = dff // tf
    tpb = rows_per_batch // tm
    k = 2 * half

    def bidx(i):
        return ctx_row if ctx_row is not None else i // tpb

    def mod_spec(kk):
        return pl.BlockSpec((None, None, None, 3, d),
                            lambda i, f: (layer, bidx(i), kk, 0, 0))

    in_specs = [pl.BlockSpec((tm, d), lambda i, f: (i, 0))]
    args = [x2]
    scratch = []
    if pre is not None:
        ot, wo = pre
        in_specs += [
            pl.BlockSpec((None, d, tm), lambda i, f: (i // tpb, 0, i % tpb)),
            pl.BlockSpec((None, d, d), lambda i, f: (layer, 0, 0)),
            mod_spec(1),
        ]
        args += [ot, wo, mod5]
        scratch.append(pltpu.VMEM((tm, d), F32))
    in_specs += [
        mod_spec(k),
        pl.BlockSpec((None, None, 1, d), lambda i, f: (layer, k, 0, 0)),
        pl.BlockSpec((None, None, d, tf), lambda i, f: (layer, half, 0, f)),
        pl.BlockSpec((None, None, d, tf), lambda i, f: (layer, half, 0, n_f + f)),
        pl.BlockSpec((None, None, tf, d), lambda i, f: (layer, half, f, 0)),
    ]
    args += [mod5, norm_g4, w_in, w_in, w_out]
    scratch += [pltpu.VMEM((tm, d), BF16), pltpu.VMEM((tm, d), F32)]
    return pl.pallas_call(
        functools.partial(_ffn_kernel, pre=pre is not None, n_f=n_f),
        grid=(r // tm, n_f),
        in_specs=in_specs,
        out_specs=pl.BlockSpec((tm, d), lambda i, f: (i, 0)),
        out_shape=jax.ShapeDtypeStruct((r, d), F32),
        scratch_shapes=scratch,
        compiler_params=_cparams("parallel", "arbitrary"),
        name="ffn_pre" if pre is not None else "ffn",
    )(*args)


def _headnorm_rope(z, gain, cos, sin, bd, rope):
    tm, w = z.shape
    lane = lax.broadcasted_iota(jnp.int32, (1, LANES), 1)
    first_half = (lane & 31) < 16
    outs = []
    for j in range(w // MXU_DIM):
        zj = z[:, j * MXU_DIM:(j + 1) * MXU_DIM]
        sq = zj * zj
        hi = sq.astype(BF16)
        lo = (sq - hi.astype(F32)).astype(BF16)
        ss = (jnp.dot(hi, bd, preferred_element_type=F32)
              + jnp.dot(lo, bd, preferred_element_type=F32))
        zn = zj * lax.rsqrt(ss * (1.0 / HEAD_DIM) + EPS)
        for half in range(MXU_DIM // LANES):
            t = zn[:, half * LANES:(half + 1) * LANES] * gain
            if rope:
                partner = jnp.where(first_half, pltpu.roll(t, LANES - 16, 1),
                                    pltpu.roll(t, 16, 1))
                t = t * cos + partner * sin
            outs.append(t)
    return jnp.concatenate(outs, axis=1)


def _proj_kernel(*refs, rope):
    if rope:
        (x_ref, mod_ref, g_ref, wq_ref, wk_ref, wvt_ref, gq_ref, gk_ref, bd_ref, cos_ref,
         sin_ref, q_ref, k_ref, vt_ref) = refs
        cos, sin = cos_ref[...], sin_ref[...]
    else:
        (x_ref, mod_ref, g_ref, wq_ref, wk_ref, wvt_ref, gq_ref, gk_ref, bd_ref,
         q_ref, k_ref, vt_ref) = refs
        cos = sin = None
    h = _norm_mod(x_ref[...], g_ref[...], mod_ref[0:1, :], mod_ref[1:2, :]).astype(BF16)
    bd = bd_ref[...]
    q = jnp.dot(h, wq_ref[...], preferred_element_type=F32)
    q_ref[...] = _headnorm_rope(q, gq_ref[...] * Q_SCALE, cos, sin, bd, rope).astype(BF16)
    k = jnp.dot(h, wk_ref[...], preferred_element_type=F32)
    k_ref[...] = _headnorm_rope(k, gk_ref[...], cos, sin, bd, rope).astype(BF16)
    vt_ref[...] = pl.dot(wvt_ref[...], h, trans_b=True).astype(BF16)


def _proj(x2, mod5, norm_g4, layer, wq, wk, wvt, gq, gk, bd, rope_tabs, *, tm, rows_per_batch,
          ctx_row):
    r, d = x2.shape
    wq_n, wk_n, wv_n = wq.shape[1], wk.shape[1], wvt.shape[0]
    tpb = rows_per_batch // tm
    nb = r // rows_per_batch
    rope = rope_tabs is not None

    def bidx(i):
        return ctx_row if ctx_row is not None else i // tpb

    const = lambda i: (0, 0)
    in_specs = [
        pl.BlockSpec((tm, d), lambda i: (i, 0)),
        pl.BlockSpec((None, None, None, 3, d), lambda i: (layer, bidx(i), 1, 0, 0)),
        pl.BlockSpec((None, None, 1, d), lambda i: (layer, 1, 0, 0)),
        pl.BlockSpec((d, wq_n), const),
        pl.BlockSpec((d, wk_n), const),
        pl.BlockSpec((wv_n, d), const),
        pl.BlockSpec((1, LANES), const),
        pl.BlockSpec((1, LANES), const),
        pl.BlockSpec((MXU_DIM, MXU_DIM), const),
    ]
    args = [x2, mod5, norm_g4, wq, wk, wvt, gq, gk, bd]
    if rope:
        in_specs += [pl.BlockSpec((tm, LANES), lambda i: (i % tpb, 0))] * 2
        args += list(rope_tabs)
    return pl.pallas_call(
        functools.partial(_proj_kernel, rope=rope),
        grid=(r // tm,),
        in_specs=in_specs,
        out_specs=[
            pl.BlockSpec((tm, wq_n), lambda i: (i, 0)),
            pl.BlockSpec((tm, wk_n), lambda i: (i, 0)),
            pl.BlockSpec((None, None, wv_n, tm), lambda i: (i // tpb, i % tpb, 0, 0)),
        ],
        out_shape=[
            jax.ShapeDtypeStruct((r, wq_n), BF16),
            jax.ShapeDtypeStruct((r, wk_n), BF16),
            jax.ShapeDtypeStruct((nb, tpb, wv_n, tm), BF16),
        ],
        compiler_params=_cparams("parallel"),
        name="qkv_proj",
    )(*args)


def _slot_mask(slot):
    lane = lax.broadcasted_iota(jnp.int32, (1, LANES), 1)
    return (lane >= slot * HEAD_DIM) & (lane < (slot + 1) * HEAD_DIM)


def _gqa_q_tile(q_ref, kvh, g):
    j = g + GQA_GROUP * (kvh // 2)
    tile = q_ref[:, j * LANES:(j + 1) * LANES]
    return jnp.where(_slot_mask(kvh % 2), tile, jnp.zeros_like(tile))


_GQA_UNITS = tuple(
    (kvh // 2, kvh * HEAD_DIM,
     tuple((g + GQA_GROUP * (kvh // 2), kvh % 2) for g in range(GQA_GROUP)))
    for kvh in range(N_KV_HEADS))
_DIFF_UNITS = tuple((h, h * 2 * HEAD_DIM, ((h, 0), (h, 1))) for h in range(2))


def _flash_kernel(*refs, units, dv, tq, tk, n_kb, mode, has_sink, lam_init):
    refs = list(refs)
    q_ref, kc_ref, vtc_ref = refs[:3]
    pos = 3
    has_lat = n_kb > 0
    if has_lat:
        kl_ref, vtl_ref = refs[pos:pos + 2]
        pos += 2
    if mode == "diff":
        lam_ref, subln_ref = refs[pos:pos + 2]
        pos += 2
    elif has_sink:
        sink_ref = refs[pos]
        pos += 1
    o_ref, qs_scr, m_scr, l_scr, acc_scr = refs[pos:pos + 5]
    if has_lat:
        s_scr, mx_scr = refs[pos + 5:]
    n_u = len(units)

    for u, (_, _, q_tiles) in enumerate(units):
        for t, (j, slot) in enumerate(q_tiles):
            tile = q_ref[:, j * LANES:(j + 1) * LANES]
            qs_scr[u, t * tq:(t + 1) * tq, :] = jnp.where(_slot_mask(slot), tile,
                                                          jnp.zeros_like(tile))

    for u, (half, v0, _) in enumerate(units):
        s = pl.dot(kc_ref[:, half * LANES:(half + 1) * LANES], qs_scr[u], trans_b=True)
        m = jnp.max(s, axis=0, keepdims=True)
        p = jnp.exp2(s - m)
        m_scr[u] = m
        l_scr[u] = jnp.sum(p, axis=0, keepdims=True)
        acc_scr[u] = jnp.dot(vtc_ref[v0:v0 + dv, :], p.astype(BF16), preferred_element_type=F32)

    if has_lat:
        def scores(kb, u, slot):
            half = units[u][0]
            row0 = pl.multiple_of(kb * tk, tk)
            k = kl_ref[pl.ds(row0, tk), half * LANES:(half + 1) * LANES]
            s = pl.dot(k, qs_scr[u], trans_b=True)
            s_scr[slot] = s
            mx_scr[slot] = jnp.max(s, axis=0, keepdims=True)

        def update(kb, u, slot):
            v0 = units[u][1]
            m_prev = m_scr[u]
            m_new = jnp.maximum(m_prev, mx_scr[slot])
            p = jnp.exp2(s_scr[slot] - m_new)
            alpha = jnp.exp2(m_prev - m_new)
            l_scr[u] = alpha * l_scr[u] + jnp.sum(p, axis=0, keepdims=True)
            acc_scr[u] = alpha * acc_scr[u] + jnp.dot(
                vtl_ref[kb, v0:v0 + dv, :], p.astype(BF16), preferred_element_type=F32)
            m_scr[u] = m_new

        scores(0, 0, 0)

        def body(kb, carry):
            for u in range(n_u):
                if u + 1 < n_u:
                    scores(kb, u + 1, (u + 1) % 2)
                else:
                    scores(jnp.minimum(kb + 1, n_kb - 1), 0, (u + 1) % 2)
                update(kb, u, u % 2)
            return carry

        lax.fori_loop(0, n_kb, body, 0)

    if mode == "diff":
        lp = lam_ref[...]
        lam = (jnp.exp(jnp.sum(lp[0:1] * lp[1:2], axis=1, keepdims=True))
               - jnp.exp(jnp.sum(lp[2:3] * lp[3:4], axis=1, keepdims=True)) + lam_init)
        gain = subln_ref[...]
        for u in range(n_u):
            o12 = acc_scr[u] / l_scr[u]
            o = o12[:, :tq] - lam * o12[:, tq:]
            o = o * lax.rsqrt(jnp.mean(o * o, axis=0, keepdims=True) + EPS)
            o_ref[u * dv:(u + 1) * dv, :] = ((o * gain) * (1.0 - lam_init)).astype(BF16)
    else:
        for u in range(n_u):
            m, l, acc = m_scr[u], l_scr[u], acc_scr[u]
            if has_sink:
                sk = sink_ref[u] * LOG2E
                m_f = jnp.maximum(m, sk)
                w = jnp.exp2(m - m_f)
                l = l * w + jnp.exp2(sk - m_f)
                acc = acc * w
            o = acc / l
            for g in range(GQA_GROUP):
                row = (u * GQA_GROUP + g) * HEAD_DIM
                o_ref[row:row + HEAD_DIM, :] = o[:, g * tq:(g + 1) * tq].astype(BF16)


def _flash_attn(q, kc, vtc, kl, vtl, *, mode, n_batch, tq, sink_rows=None, lam_params=None,
                subln=None, lam_init=0.0):
    r, dq = q.shape
    t_q = r // n_batch
    c = kc.shape[0] // n_batch
    wk = kc.shape[1]
    n_grp = wk // MXU_DIM
    q_cols = dq // n_grp
    units, dv = (_GQA_UNITS, HEAD_DIM) if mode == "gqa" else (_DIFF_UNITS, 2 * HEAD_DIM)
    cols = len(units[0][2]) * tq
    has_lat = kl is not None
    nq = t_q // tq
    in_specs = [
        pl.BlockSpec((tq, q_cols), lambda b, g, i: (b * nq + i, g)),
        pl.BlockSpec((c, MXU_DIM), lambda b, g, i: (b, g)),
        pl.BlockSpec((None, MXU_DIM, c), lambda b, g, i: (b, g, 0)),
    ]
    args = [q, kc, vtc]
    n_kb, tk = 0, 0
    scratch = [
        pltpu.VMEM((len(units), cols, LANES), BF16),
        pltpu.VMEM((len(units), 1, cols), F32),
        pltpu.VMEM((len(units), 1, cols), F32),
        pltpu.VMEM((len(units), dv, cols), F32),
    ]
    if has_lat:
        t = kl.shape[0] // n_batch
        n_kb, tk = vtl.shape[1], vtl.shape[3]
        in_specs += [
            pl.BlockSpec((t, MXU_DIM), lambda b, g, i: (b, g)),
            pl.BlockSpec((None, n_kb, MXU_DIM, tk), lambda b, g, i: (b, 0, g, 0)),
        ]
        args += [kl, vtl]
        scratch += [pltpu.VMEM((2, tk, cols), F32), pltpu.VMEM((2, 1, cols), F32)]
    if mode == "diff":
        in_specs += [
            pl.BlockSpec((4, HEAD_DIM), lambda b, g, i: (0, 0)),
            pl.BlockSpec((dv, 1), lambda b, g, i: (0, 0)),
        ]
        args += [lam_params, subln.reshape(dv, 1)]
    elif sink_rows is not None:
        in_specs.append(pl.BlockSpec((N_KV_HEADS, 1, cols), lambda b, g, i: (0, 0, 0)))
        args.append(sink_rows)
    return pl.pallas_call(
        functools.partial(_flash_kernel, units=units, dv=dv, tq=tq, tk=tk, n_kb=n_kb, mode=mode,
                          has_sink=sink_rows is not None, lam_init=lam_init),
        grid=(n_batch, n_grp, nq),
        in_specs=in_specs,
        out_specs=pl.BlockSpec((None, q_cols, tq), lambda b, g, i: (b, g, i)),
        out_shape=jax.ShapeDtypeStruct((n_batch, dq, t_q), BF16),
        scratch_shapes=scratch,
        compiler_params=_cparams("parallel", "parallel", "arbitrary"),
        name=mode + "_attn",
    )(*args)


def _window_kernel(q_ref, kc_ref, vtc_ref, kp_ref, kcur_ref, kn_ref, vtp_ref, vtcur_ref,
                   vtn_ref, sink_ref, o_ref, *, tq):
    qi = pl.program_id(1)
    nq = pl.num_programs(1)
    c = kc_ref.shape[0]
    n_lat = tq + 2 * WINDOW
    cols = GQA_GROUP * tq
    rr = lax.broadcasted_iota(jnp.int32, (n_lat, 1), 0)
    cc = lax.broadcasted_iota(jnp.int32, (1, cols), 1) & (tq - 1)
    rel = rr - WINDOW - cc
    valid = (jnp.abs(rel) <= WINDOW)
    valid &= (rr >= WINDOW) | (qi > 0)
    valid &= (rr < tq + WINDOW) | (qi < nq - 1)
    for kvh in range(N_KV_HEADS):
        half = kvh // 2
        sl = slice(half * LANES, (half + 1) * LANES)
        qs = jnp.concatenate([_gqa_q_tile(q_ref, kvh, g) for g in range(GQA_GROUP)], axis=0)
        s_ctx = pl.dot(kc_ref[:, sl], qs, trans_b=True)
        k_lat = jnp.concatenate([kp_ref[:, sl], kcur_ref[:, sl], kn_ref[:, sl]], axis=0)
        s_lat = jnp.where(valid, pl.dot(k_lat, qs, trans_b=True), NEG_INF)
        sk = sink_ref[kvh] * LOG2E
        m = jnp.maximum(jnp.maximum(jnp.max(s_ctx, axis=0, keepdims=True),
                                    jnp.max(s_lat, axis=0, keepdims=True)), sk)
        p_ctx = jnp.exp2(s_ctx - m)
        p_lat = jnp.exp2(s_lat - m)
        l = (jnp.sum(p_ctx, axis=0, keepdims=True) + jnp.sum(p_lat, axis=0, keepdims=True)
             + jnp.exp2(sk - m))
        hs = slice(kvh * HEAD_DIM, (kvh + 1) * HEAD_DIM)
        vt_lat = jnp.concatenate([vtp_ref[hs, :], vtcur_ref[hs, :], vtn_ref[hs, :]], axis=1)
        acc = (jnp.dot(vtc_ref[hs, :], p_ctx.astype(BF16), preferred_element_type=F32)
               + jnp.dot(vt_lat, p_lat.astype(BF16), preferred_element_type=F32))
        o = acc / l
        for g in range(GQA_GROUP):
            row = (kvh * GQA_GROUP + g) * HEAD_DIM
            o_ref[row:row + HEAD_DIM, :] = o[:, g * tq:(g + 1) * tq].astype(BF16)


def _window_attn(q, kc, vtc, kl, vtl, sink_rows, *, n_batch, tq):
    r, dq = q.shape
    t = r // n_batch
    c = kc.shape[0] // n_batch
    wk = kc.shape[1]
    tkv = vtl.shape[3]
    nq = t // tq
    rb = tq // WINDOW
    nwb = t // WINDOW
    wpb, qpb = tkv // WINDOW, tkv // tq
    prev = lambda b, i: jnp.maximum(i * rb - 1, 0)
    nxt = lambda b, i: jnp.minimum((i + 1) * rb, nwb - 1)
    in_specs = [
        pl.BlockSpec((tq, dq), lambda b, i: (b * nq + i, 0)),
        pl.BlockSpec((c, wk), lambda b, i: (b, 0)),
        pl.BlockSpec((None, wk, c), lambda b, i: (b, 0, 0)),
        pl.BlockSpec((WINDOW, wk), lambda b, i: (b * nwb + prev(b, i), 0)),
        pl.BlockSpec((tq, wk), lambda b, i: (b * nq + i, 0)),
        pl.BlockSpec((WINDOW, wk), lambda b, i: (b * nwb + nxt(b, i), 0)),
        pl.BlockSpec((None, None, wk, WINDOW),
                     lambda b, i: (b, prev(b, i) // wpb, 0, prev(b, i) % wpb)),
        pl.BlockSpec((None, None, wk, tq), lambda b, i: (b, i // qpb, 0, i % qpb)),
        pl.BlockSpec((None, None, wk, WINDOW),
                     lambda b, i: (b, nxt(b, i) // wpb, 0, nxt(b, i) % wpb)),
        pl.BlockSpec((N_KV_HEADS, 1, GQA_GROUP * tq), lambda b, i: (0, 0, 0)),
    ]
    return pl.pallas_call(
        functools.partial(_window_kernel, tq=tq),
        grid=(n_batch, nq),
        in_specs=in_specs,
        out_specs=pl.BlockSpec((None, dq, tq), lambda b, i: (b, 0, i)),
        out_shape=jax.ShapeDtypeStruct((n_batch, dq, t), BF16),
        compiler_params=_cparams("parallel", "parallel"),
        name="window_attn",
    )(q, kc, vtc, kl, kl, kl, vtl, vtl, vtl, sink_rows)


def _rope_tables(n_tokens):
    rows = n_tokens // GRID_W
    row = jnp.repeat(jnp.arange(rows), GRID_W)
    col = jnp.tile(jnp.arange(GRID_W), rows)
    n_freq = HEAD_DIM // 4
    inv_freq = ROPE_BASE ** (-jnp.arange(n_freq, dtype=F32) / n_freq)
    ang = jnp.stack([row, col], axis=-1).astype(F32)[:, :, None] * inv_freq
    cos, sin = jnp.cos(ang), jnp.sin(ang)
    cos_h = jnp.concatenate([cos, cos], axis=-1).reshape(n_tokens, HEAD_DIM)
    sin_h = jnp.concatenate([-sin, sin], axis=-1).reshape(n_tokens, HEAD_DIM)
    return jnp.tile(cos_h, (1, 2)), jnp.tile(sin_h, (1, 2))


def _gqa_weights(w_qkv):
    d = w_qkv.shape[0]
    n_q = N_KV_HEADS * GQA_GROUP * HEAD_DIM
    n_kv = N_KV_HEADS * HEAD_DIM
    wq = w_qkv[:, :n_q].reshape(d, N_KV_HEADS // 2, 2, GQA_GROUP, HEAD_DIM)
    wq = wq.transpose(0, 1, 3, 2, 4).reshape(d, n_q)
    wk = w_qkv[:, n_q:n_q + n_kv]
    wvt = w_qkv[:, n_q + n_kv:].T
    return wq.astype(BF16), wk.astype(BF16), wvt.astype(BF16)


def _diff_weights(w_qkv):
    n = w_qkv.shape[1] // 3
    return (w_qkv[:, :n].astype(BF16), w_qkv[:, n:2 * n].astype(BF16),
            w_qkv[:, 2 * n:].T.astype(BF16))


def _lane_gain(g):
    return jnp.tile(g.reshape(1, HEAD_DIM), (1, LANES // HEAD_DIM))


def _sink_rows(sink, tq):
    return jnp.repeat(sink.reshape(N_KV_HEADS, 1, GQA_GROUP, 1), tq, axis=3).reshape(
        N_KV_HEADS, 1, GQA_GROUP * tq)


def kernel(x, c, ctx, c_ctx, norm_g, w_ada, b_ada, w_ffn_in, w_ffn_out, w_o, w_qkv_a, qk_norm_a,
           w_qkv_b, qk_norm_b, sink_b, w_qkv_c, qk_norm_c, diff_lambda, diff_subln):
    n_b, t, d = x.shape
    n_c = ctx.shape[1]
    depth = w_ada.shape[0]
    d_ff = w_ffn_out.shape[2]
    assert n_b + 1 <= 8 and t % 512 == 0 and n_c % LANES == 0 and n_c <= 512

    tm = 512
    tf = d_ff // 2
    tq, tq_diff = 256, 512
    tq_c = n_c

    c8 = jnp.zeros((8, d), F32).at[:n_b].set(c).at[n_b].set(c_ctx)
    mod5 = _ada_all(c8, w_ada, b_ada).reshape(depth, 8, 3, 3, d)

    rope_tabs = _rope_tables(t)
    eye = jnp.arange(MXU_DIM) // HEAD_DIM
    bd = (eye[:, None] == eye[None, :]).astype(BF16)
    w_in = w_ffn_in.astype(BF16)
    w_out = w_ffn_out.astype(BF16)
    wo = w_o.astype(BF16)
    norm_g4 = norm_g.reshape(depth, 3, 1, d)

    xs = x.reshape(n_b * t, d)
    cs = ctx.reshape(n_b * n_c, d)
    lat = dict(tm=tm, rows_per_batch=t, ctx_row=None)
    cx = dict(tm=n_c, rows_per_batch=n_c, ctx_row=n_b)

    for i in range(depth):
        last = i == depth - 1
        kind, j = i % 3, i // 3
        xs = _ffn(xs, mod5, norm_g4, w_in, w_out, i, 0, tf=tf, **lat)
        cs = _ffn(cs, mod5, norm_g4, w_in, w_out, i, 0, tf=tf, **cx)
        if kind == 2:
            wq, wk, wvt = _diff_weights(w_qkv_c[j])
            qk_g = qk_norm_c[j]
        else:
            wq, wk, wvt = _gqa_weights((w_qkv_a, w_qkv_b)[kind][j])
            qk_g = (qk_norm_a, qk_norm_b)[kind][j]
        gq, gk = _lane_gain(qk_g[0]), _lane_gain(qk_g[1])
        q, kl, vtl = _proj(xs, mod5, norm_g4, i, wq, wk, wvt, gq, gk, bd, rope_tabs, **lat)
        qc, kc, vtc = _proj(cs, mod5, norm_g4, i, wq, wk, wvt, gq, gk, bd, None, **cx)
        vtc = vtc.reshape(n_b, vtc.shape[2], n_c)
        if kind == 0:
            ot = _flash_attn(q, kc, vtc, kl, vtl, mode="gqa", n_batch=n_b, tq=tq)
            if not last:
                otc = _flash_attn(qc, kc, vtc, None, None, mode="gqa", n_batch=n_b, tq=tq_c)
        elif kind == 1:
            ot = _window_attn(q, kc, vtc, kl, vtl, _sink_rows(sink_b[j], tq), n_batch=n_b, tq=tq)
            if not last:
                otc = _flash_attn(qc, kc, vtc, None, None, mode="gqa", n_batch=n_b, tq=tq_c,
                                  sink_rows=_sink_rows(sink_b[j], tq_c))
        else:
            extra = dict(lam_params=diff_lambda[j], subln=diff_subln[j],
                         lam_init=0.8 - 0.6 * math.exp(-0.3 * i))
            ot = _flash_attn(q, kc, vtc, kl, vtl, mode="diff", n_batch=n_b, tq=tq_diff, **extra)
            if not last:
                otc = _flash_attn(qc, kc, vtc, None, None, mode="diff", n_batch=n_b, tq=tq_c,
                                  **extra)
        xs = _ffn(xs, mod5, norm_g4, w_in, w_out, i, 1, tf=tf, pre=(ot, wo), **lat)
        if not last:
            cs = _ffn(cs, mod5, norm_g4, w_in, w_out, i, 1, tf=tf, pre=(otc, wo), **cx)
    return xs.reshape(n_b, t, d)
```

```python
import functools
import math

import jax
import jax.numpy as jnp
from jax import lax
from jax.experimental import pallas as pl
from jax.experimental.pallas import tpu as pltpu

F32 = jnp.float32
BF16 = jnp.bfloat16

HEAD_DIM = 64
N_KV_HEADS = 4
GQA_GROUP = 4
N_DIFF_HEADS = 8
GRID_W = 64
WINDOW = 128
ROPE_BASE = 10000.0
EPS = 1e-6
NEG_INF = -1e30
LOG2E = math.log2(math.e)
Q_SCALE = HEAD_DIM ** -0.5 * LOG2E
FFN_RESIDUAL = 0.5
N_MOD = 9
LANES = 128
MXU_DIM = 256
SUM_ROWS = 16
VMEM_LIMIT = 56 * 1024 * 1024


def _cparams(*sem):
    return pltpu.CompilerParams(dimension_semantics=sem, vmem_limit_bytes=VMEM_LIMIT)


def _silu(a):
    return a * jax.nn.sigmoid(a)


def _norm_mod(x, g, shift, scale):
    y = x * lax.rsqrt(jnp.mean(x * x, axis=-1, keepdims=True) + EPS)
    return (y * g) * (1.0 + scale) + shift


def _ada_kernel(c_ref, w_ref, b_ref, o_ref):
    s = _silu(c_ref[...])
    o_ref[...] = jnp.dot(s, w_ref[...], preferred_element_type=F32) + b_ref[...]


def _ada_all(c8, w_ada, b_ada):
    depth, d, nd = w_ada.shape
    tn = nd // 8
    return pl.pallas_call(
        _ada_kernel,
        grid=(depth, nd // tn),
        in_specs=[
            pl.BlockSpec((8, d), lambda i, j: (0, 0)),
            pl.BlockSpec((None, d, tn), lambda i, j: (i, 0, j)),
            pl.BlockSpec((None, 1, tn), lambda i, j: (i, 0, j)),
        ],
        out_specs=pl.BlockSpec((None, 8, tn), lambda i, j: (i, 0, j)),
        out_shape=jax.ShapeDtypeStruct((depth, 8, nd), F32),
        compiler_params=_cparams("parallel", "parallel"),
        name="adaln",
    )(c8, w_ada, b_ada.reshape(depth, 1, nd))


def _ffn_kernel(*refs, pre):
    if pre:
        x_ref, ot_ref, wo_ref, modp_ref, mod_ref, g_ref, wa_ref, wu_ref, wout_ref, o_ref = refs
    else:
        x_ref, mod_ref, g_ref, wa_ref, wu_ref, wout_ref, o_ref = refs
    x = x_ref[...]
    if pre:
        x = x + modp_ref[2:3, :] * pl.dot(ot_ref[...], wo_ref[...], trans_a=True)
    h = _norm_mod(x, g_ref[...], mod_ref[0:1, :], mod_ref[1:2, :]).astype(BF16)
    a = jnp.dot(h, wa_ref[...], preferred_element_type=F32)
    u = jnp.dot(h, wu_ref[...], preferred_element_type=F32)
    act = (_silu(a) * u).astype(BF16)
    y = jnp.dot(act, wout_ref[...], preferred_element_type=F32)
    o_ref[...] = x + (FFN_RESIDUAL * mod_ref[2:3, :]) * y


def _ffn(x2, mod5, norm_g4, w_in, w_out, layer, half, *, tm, rows_per_batch, ctx_row, pre=None):
    r, d = x2.shape
    dff = w_out.shape[2]
    tpb = rows_per_batch // tm
    k = 2 * half
    resident = dict(pipeline_mode=pl.Buffered(1))

    def bidx(i):
        return ctx_row if ctx_row is not None else i // tpb

    def mod_spec(kk):
        return pl.BlockSpec((None, None, None, 3, d), lambda i: (layer, bidx(i), kk, 0, 0))

    in_specs = [pl.BlockSpec((tm, d), lambda i: (i, 0))]
    args = [x2]
    if pre is not None:
        ot, wo = pre
        in_specs += [
            pl.BlockSpec((None, d, tm), lambda i: (i // tpb, 0, i % tpb)),
            pl.BlockSpec((None, d, d), lambda i: (layer, 0, 0), **resident),
            mod_spec(1),
        ]
        args += [ot, wo, mod5]
    in_specs += [
        mod_spec(k),
        pl.BlockSpec((None, None, 1, d), lambda i: (layer, k, 0, 0)),
        pl.BlockSpec((None, None, d, dff), lambda i: (layer, half, 0, 0), **resident),
        pl.BlockSpec((None, None, d, dff), lambda i: (layer, half, 0, 1), **resident),
        pl.BlockSpec((None, None, dff, d), lambda i: (layer, half, 0, 0), **resident),
    ]
    args += [mod5, norm_g4, w_in, w_in, w_out]
    return pl.pallas_call(
        functools.partial(_ffn_kernel, pre=pre is not None),
        grid=(r // tm,),
        in_specs=in_specs,
        out_specs=pl.BlockSpec((tm, d), lambda i: (i, 0)),
        out_shape=jax.ShapeDtypeStruct((r, d), F32),
        compiler_params=_cparams("parallel"),
        name="ffn_pre" if pre is not None else "ffn",
    )(*args)


def _headnorm_rope(z, gain, cos, sin, bd, rope):
    tm, w = z.shape
    lane = lax.broadcasted_iota(jnp.int32, (1, LANES), 1)
    first_half = (lane & 31) < 16
    outs = []
    for j in range(w // MXU_DIM):
        zj = z[:, j * MXU_DIM:(j + 1) * MXU_DIM]
        sq = zj * zj
        hi = sq.astype(BF16)
        lo = (sq - hi.astype(F32)).astype(BF16)
        ss = (jnp.dot(hi, bd, preferred_element_type=F32)
              + jnp.dot(lo, bd, preferred_element_type=F32))
        zn = zj * lax.rsqrt(ss * (1.0 / HEAD_DIM) + EPS)
        for half in range(MXU_DIM // LANES):
            t = zn[:, half * LANES:(half + 1) * LANES] * gain
            if rope:
                partner = jnp.where(first_half, pltpu.roll(t, LANES - 16, 1),
                                    pltpu.roll(t, 16, 1))
                t = t * cos + partner * sin
            outs.append(t)
    return jnp.concatenate(outs, axis=1)


def _proj_kernel(*refs, rope):
    if rope:
        (x_ref, mod_ref, g_ref, wq_ref, wk_ref, wvt_ref, gq_ref, gk_ref, bd_ref, cos_ref,
         sin_ref, q_ref, k_ref, vt_ref) = refs
        cos, sin = cos_ref[...], sin_ref[...]
    else:
        (x_ref, mod_ref, g_ref, wq_ref, wk_ref, wvt_ref, gq_ref, gk_ref, bd_ref,
         q_ref, k_ref, vt_ref) = refs
        cos = sin = None
    h = _norm_mod(x_ref[...], g_ref[...], mod_ref[0:1, :], mod_ref[1:2, :]).astype(BF16)
    bd = bd_ref[...]
    q = jnp.dot(h, wq_ref[...], preferred_element_type=F32)
    q_ref[...] = _headnorm_rope(q, gq_ref[...] * Q_SCALE, cos, sin, bd, rope).astype(BF16)
    k = jnp.dot(h, wk_ref[...], preferred_element_type=F32)
    k_ref[...] = _headnorm_rope(k, gk_ref[...], cos, sin, bd, rope).astype(BF16)
    vt_ref[...] = pl.dot(wvt_ref[...], h, trans_b=True).astype(BF16)


def _proj(x2, mod5, norm_g4, layer, wq, wk, wvt, gq, gk, bd, rope_tabs, *, tm, rows_per_batch,
          ctx_row):
    r, d = x2.shape
    wq_n, wk_n, wv_n = wq.shape[1], wk.shape[1], wvt.shape[0]
    tpb = rows_per_batch // tm
    nb = r // rows_per_batch
    rope = rope_tabs is not None

    def bidx(i):
        return ctx_row if ctx_row is not None else i // tpb

    const = lambda i: (0, 0)
    in_specs = [
        pl.BlockSpec((tm, d), lambda i: (i, 0)),
        pl.BlockSpec((None, None, None, 3, d), lambda i: (layer, bidx(i), 1, 0, 0)),
        pl.BlockSpec((None, None, 1, d), lambda i: (layer, 1, 0, 0)),
        pl.BlockSpec((d, wq_n), const),
        pl.BlockSpec((d, wk_n), const),
        pl.BlockSpec((wv_n, d), const),
        pl.BlockSpec((1, LANES), const),
        pl.BlockSpec((1, LANES), const),
        pl.BlockSpec((MXU_DIM, MXU_DIM), const),
    ]
    args = [x2, mod5, norm_g4, wq, wk, wvt, gq, gk, bd]
    if rope:
        in_specs += [pl.BlockSpec((tm, LANES), lambda i: (i % tpb, 0))] * 2
        args += list(rope_tabs)
    return pl.pallas_call(
        functools.partial(_proj_kernel, rope=rope),
        grid=(r // tm,),
        in_specs=in_specs,
        out_specs=[
            pl.BlockSpec((tm, wq_n), lambda i: (i, 0)),
            pl.BlockSpec((tm, wk_n), lambda i: (i, 0)),
            pl.BlockSpec((None, None, wv_n, tm), lambda i: (i // tpb, i % tpb, 0, 0)),
        ],
        out_shape=[
            jax.ShapeDtypeStruct((r, wq_n), BF16),
            jax.ShapeDtypeStruct((r, wk_n), BF16),
            jax.ShapeDtypeStruct((nb, tpb, wv_n, tm), BF16),
        ],
        compiler_params=_cparams("parallel"),
        name="qkv_proj",
    )(*args)


def _slot_mask(slot):
    lane = lax.broadcasted_iota(jnp.int32, (1, LANES), 1)
    return (lane >= slot * HEAD_DIM) & (lane < (slot + 1) * HEAD_DIM)


def _gqa_q_tile(q_ref, kvh, g):
    j = g + GQA_GROUP * (kvh // 2)
    tile = q_ref[:, j * LANES:(j + 1) * LANES]
    return jnp.where(_slot_mask(kvh % 2), tile, jnp.zeros_like(tile))


_GQA_UNITS = tuple(
    (kvh // 2, kvh * HEAD_DIM,
     tuple((g + GQA_GROUP * (kvh // 2), kvh % 2) for g in range(GQA_GROUP)))
    for kvh in range(N_KV_HEADS))
_DIFF_UNITS = tuple((h, h * 2 * HEAD_DIM, ((h, 0), (h, 1))) for h in range(2))


def _flash_kernel(*refs, units, dv, tq, tk, n_kb, mode, has_sink, lam_init):
    refs = list(refs)
    q_ref, kc_ref, vtc_ref = refs[:3]
    pos = 3
    has_lat = n_kb > 0
    if has_lat:
        kl_ref, vtl_ref = refs[pos:pos + 2]
        pos += 2
    if mode == "diff":
        lam_ref, subln_ref = refs[pos:pos + 2]
        pos += 2
    elif has_sink:
        sink_ref = refs[pos]
        pos += 1
    o_ref, qs_scr, m_scr, acc_scr = refs[pos:pos + 4]
    if has_lat:
        s_scr, mx_scr = refs[pos + 4:]
    n_u = len(units)

    def pv(vt, p):
        ones = jnp.ones((SUM_ROWS, vt.shape[1]), BF16)
        return jnp.dot(jnp.concatenate([vt, ones], axis=0), p.astype(BF16),
                       preferred_element_type=F32)

    for u, (_, _, q_tiles) in enumerate(units):
        for t, (j, slot) in enumerate(q_tiles):
            tile = q_ref[:, j * LANES:(j + 1) * LANES]
            qs_scr[u, t * tq:(t + 1) * tq, :] = jnp.where(_slot_mask(slot), tile,
                                                          jnp.zeros_like(tile))

    for u, (half, v0, _) in enumerate(units):
        s = pl.dot(kc_ref[:, half * LANES:(half + 1) * LANES], qs_scr[u], trans_b=True)
        m = jnp.max(s, axis=0, keepdims=True)
        p = jnp.exp2(s - m)
        m_scr[u] = m
        acc_scr[u] = pv(vtc_ref[v0:v0 + dv, :], p)

    if has_lat:
        def scores(kb, u, slot):
            half = units[u][0]
            row0 = pl.multiple_of(kb * tk, tk)
            k = kl_ref[pl.ds(row0, tk), half * LANES:(half + 1) * LANES]
            s = pl.dot(k, qs_scr[u], trans_b=True)
            s_scr[slot] = s
            mx_scr[slot] = jnp.max(s, axis=0, keepdims=True)

        def update(kb, u, slot):
            v0 = units[u][1]
            m_prev = m_scr[u]
            m_new = jnp.maximum(m_prev, mx_scr[slot])
            p = jnp.exp2(s_scr[slot] - m_new)
            alpha = jnp.exp2(m_prev - m_new)
            acc_scr[u] = alpha * acc_scr[u] + pv(vtl_ref[kb, v0:v0 + dv, :], p)
            m_scr[u] = m_new

        scores(0, 0, 0)

        def body(kb, carry):
            for u in range(n_u):
                if u + 1 < n_u:
                    scores(kb, u + 1, (u + 1) % 2)
                else:
                    scores(jnp.minimum(kb + 1, n_kb - 1), 0, (u + 1) % 2)
                update(kb, u, u % 2)
            return carry

        lax.fori_loop(0, n_kb, body, 0)

    if mode == "diff":
        lp = lam_ref[...]
        lam = (jnp.exp(jnp.sum(lp[0:1] * lp[1:2], axis=1, keepdims=True))
               - jnp.exp(jnp.sum(lp[2:3] * lp[3:4], axis=1, keepdims=True)) + lam_init)
        gain = subln_ref[...]
        for u in range(n_u):
            o12 = acc_scr[u, :dv, :] / acc_scr[u, dv:dv + 1, :]
            o = o12[:, :tq] - lam * o12[:, tq:]
            o = o * lax.rsqrt(jnp.mean(o * o, axis=0, keepdims=True) + EPS)
            o_ref[u * dv:(u + 1) * dv, :] = ((o * gain) * (1.0 - lam_init)).astype(BF16)
    else:
        for u in range(n_u):
            m, l, acc = m_scr[u], acc_scr[u, dv:dv + 1, :], acc_scr[u, :dv, :]
            if has_sink:
                sk = sink_ref[u] * LOG2E
                m_f = jnp.maximum(m, sk)
                w = jnp.exp2(m - m_f)
                l = l * w + jnp.exp2(sk - m_f)
                acc = acc * w
            o = acc / l
            for g in range(GQA_GROUP):
                row = (u * GQA_GROUP + g) * HEAD_DIM
                o_ref[row:row + HEAD_DIM, :] = o[:, g * tq:(g + 1) * tq].astype(BF16)


def _flash_attn(q, kc, vtc, kl, vtl, *, mode, n_batch, tq, sink_rows=None, lam_params=None,
                subln=None, lam_init=0.0):
    r, dq = q.shape
    t_q = r // n_batch
    c = kc.shape[0] // n_batch
    wk = kc.shape[1]
    n_grp = wk // MXU_DIM
    q_cols = dq // n_grp
    units, dv = (_GQA_UNITS, HEAD_DIM) if mode == "gqa" else (_DIFF_UNITS, 2 * HEAD_DIM)
    cols = len(units[0][2]) * tq
    has_lat = kl is not None
    nq = t_q // tq
    in_specs = [
        pl.BlockSpec((tq, q_cols), lambda b, g, i: (b * nq + i, g)),
        pl.BlockSpec((c, MXU_DIM), lambda b, g, i: (b, g)),
        pl.BlockSpec((None, MXU_DIM, c), lambda b, g, i: (b, g, 0)),
    ]
    args = [q, kc, vtc]
    n_kb, tk = 0, 0
    scratch = [
        pltpu.VMEM((len(units), cols, LANES), BF16),
        pltpu.VMEM((len(units), 1, cols), F32),
        pltpu.VMEM((len(units), dv + SUM_ROWS, cols), F32),
    ]
    if has_lat:
        t = kl.shape[0] // n_batch
        n_kb, tk = vtl.shape[1], vtl.shape[3]
        in_specs += [
            pl.BlockSpec((t, MXU_DIM), lambda b, g, i: (b, g)),
            pl.BlockSpec((None, n_kb, MXU_DIM, tk), lambda b, g, i: (b, 0, g, 0)),
        ]
        args += [kl, vtl]
        scratch += [pltpu.VMEM((2, tk, cols), F32), pltpu.VMEM((2, 1, cols), F32)]
    if mode == "diff":
        in_specs += [
            pl.BlockSpec((4, HEAD_DIM), lambda b, g, i: (0, 0)),
            pl.BlockSpec((dv, 1), lambda b, g, i: (0, 0)),
        ]
        args += [lam_params, subln.reshape(dv, 1)]
    elif sink_rows is not None:
        in_specs.append(pl.BlockSpec((N_KV_HEADS, 1, cols), lambda b, g, i: (0, 0, 0)))
        args.append(sink_rows)
    return pl.pallas_call(
        functools.partial(_flash_kernel, units=units, dv=dv, tq=tq, tk=tk, n_kb=n_kb, mode=mode,
                          has_sink=sink_rows is not None, lam_init=lam_init),
        grid=(n_batch, n_grp, nq),
        in_specs=in_specs,
        out_specs=pl.BlockSpec((None, q_cols, tq), lambda b, g, i: (b, g, i)),
        out_shape=jax.ShapeDtypeStruct((n_batch, dq, t_q), BF16),
        scratch_shapes=scratch,
        compiler_params=_cparams("parallel", "parallel", "arbitrary"),
        name=mode + "_attn",
    )(*args)


def _window_kernel(q_ref, kc_ref, vtc_ref, kp_ref, kcur_ref, kn_ref, vtp_ref, vtcur_ref,
                   vtn_ref, sink_ref, o_ref, *, tq):
    qi = pl.program_id(1)
    nq = pl.num_programs(1)
    c = kc_ref.shape[0]
    n_lat = tq + 2 * WINDOW
    cols = GQA_GROUP * tq
    rr = lax.broadcasted_iota(jnp.int32, (n_lat, 1), 0)
    cc = lax.broadcasted_iota(jnp.int32, (1, cols), 1) & (tq - 1)
    rel = rr - WINDOW - cc
    valid = (jnp.abs(rel) <= WINDOW)
    valid &= (rr >= WINDOW) | (qi > 0)
    valid &= (rr < tq + WINDOW) | (qi < nq - 1)
    for kvh in range(N_KV_HEADS):
        half = kvh // 2
        sl = slice(half * LANES, (half + 1) * LANES)
        qs = jnp.concatenate([_gqa_q_tile(q_ref, kvh, g) for g in range(GQA_GROUP)], axis=0)
        s_ctx = pl.dot(kc_ref[:, sl], qs, trans_b=True)
        k_lat = jnp.concatenate([kp_ref[:, sl], kcur_ref[:, sl], kn_ref[:, sl]], axis=0)
        s_lat = jnp.where(valid, pl.dot(k_lat, qs, trans_b=True), NEG_INF)
        sk = sink_ref[kvh] * LOG2E
        m = jnp.maximum(jnp.maximum(jnp.max(s_ctx, axis=0, keepdims=True),
                                    jnp.max(s_lat, axis=0, keepdims=True)), sk)
        p_ctx = jnp.exp2(s_ctx - m)
        p_lat = jnp.exp2(s_lat - m)
        l = (jnp.sum(p_ctx, axis=0, keepdims=True) + jnp.sum(p_lat, axis=0, keepdims=True)
             + jnp.exp2(sk - m))
        hs = slice(kvh * HEAD_DIM, (kvh + 1) * HEAD_DIM)
        vt_lat = jnp.concatenate([vtp_ref[hs, :], vtcur_ref[hs, :], vtn_ref[hs, :]], axis=1)
        acc = (jnp.dot(vtc_ref[hs, :], p_ctx.astype(BF16), preferred_element_type=F32)
               + jnp.dot(vt_lat, p_lat.astype(BF16), preferred_element_type=F32))
        o = acc / l
        for g in range(GQA_GROUP):
            row = (kvh * GQA_GROUP + g) * HEAD_DIM
            o_ref[row:row + HEAD_DIM, :] = o[:, g * tq:(g + 1) * tq].astype(BF16)


def _window_attn(q, kc, vtc, kl, vtl, sink_rows, *, n_batch, tq):
    r, dq = q.shape
    t = r // n_batch
    c = kc.shape[0] // n_batch
    wk = kc.shape[1]
    tkv = vtl.shape[3]
    nq = t // tq
    rb = tq // WINDOW
    nwb = t // WINDOW
    wpb, qpb = tkv // WINDOW, tkv // tq
    prev = lambda b, i: jnp.maximum(i * rb - 1, 0)
    nxt = lambda b, i: jnp.minimum((i + 1) * rb, nwb - 1)
    in_specs = [
        pl.BlockSpec((tq, dq), lambda b, i: (b * nq + i, 0)),
        pl.BlockSpec((c, wk), lambda b, i: (b, 0)),
        pl.BlockSpec((None, wk, c), lambda b, i: (b, 0, 0)),
        pl.BlockSpec((WINDOW, wk), lambda b, i: (b * nwb + prev(b, i), 0)),
        pl.BlockSpec((tq, wk), lambda b, i: (b * nq + i, 0)),
        pl.BlockSpec((WINDOW, wk), lambda b, i: (b * nwb + nxt(b, i), 0)),
        pl.BlockSpec((None, None, wk, WINDOW),
                     lambda b, i: (b, prev(b, i) // wpb, 0, prev(b, i) % wpb)),
        pl.BlockSpec((None, None, wk, tq), lambda b, i: (b, i // qpb, 0, i % qpb)),
        pl.BlockSpec((None, None, wk, WINDOW),
                     lambda b, i: (b, nxt(b, i) // wpb, 0, nxt(b, i) % wpb)),
        pl.BlockSpec((N_KV_HEADS, 1, GQA_GROUP * tq), lambda b, i: (0, 0, 0)),
    ]
    return pl.pallas_call(
        functools.partial(_window_kernel, tq=tq),
        grid=(n_batch, nq),
        in_specs=in_specs,
        out_specs=pl.BlockSpec((None, dq, tq), lambda b, i: (b, 0, i)),
        out_shape=jax.ShapeDtypeStruct((n_batch, dq, t), BF16),
        compiler_params=_cparams("parallel", "parallel"),
        name="window_attn",
    )(q, kc, vtc, kl, kl, kl, vtl, vtl, vtl, sink_rows)


def _rope_tables(n_tokens):
    rows = n_tokens // GRID_W
    row = jnp.repeat(jnp.arange(rows), GRID_W)
    col = jnp.tile(jnp.arange(GRID_W), rows)
    n_freq = HEAD_DIM // 4
    inv_freq = ROPE_BASE ** (-jnp.arange(n_freq, dtype=F32) / n_freq)
    ang = jnp.stack([row, col], axis=-1).astype(F32)[:, :, None] * inv_freq
    cos, sin = jnp.cos(ang), jnp.sin(ang)
    cos_h = jnp.concatenate([cos, cos], axis=-1).reshape(n_tokens, HEAD_DIM)
    sin_h = jnp.concatenate([-sin, sin], axis=-1).reshape(n_tokens, HEAD_DIM)
    return jnp.tile(cos_h, (1, 2)), jnp.tile(sin_h, (1, 2))


def _gqa_weights(w_qkv):
    d = w_qkv.shape[0]
    n_q = N_KV_HEADS * GQA_GROUP * HEAD_DIM
    n_kv = N_KV_HEADS * HEAD_DIM
    wq = w_qkv[:, :n_q].reshape(d, N_KV_HEADS // 2, 2, GQA_GROUP, HEAD_DIM)
    wq = wq.transpose(0, 1, 3, 2, 4).reshape(d, n_q)
    wk = w_qkv[:, n_q:n_q + n_kv]
    wvt = w_qkv[:, n_q + n_kv:].T
    return wq.astype(BF16), wk.astype(BF16), wvt.astype(BF16)


def _diff_weights(w_qkv):
    n = w_qkv.shape[1] // 3
    return (w_qkv[:, :n].astype(BF16), w_qkv[:, n:2 * n].astype(BF16),
            w_qkv[:, 2 * n:].T.astype(BF16))


def _lane_gain(g):
    return jnp.tile(g.reshape(1, HEAD_DIM), (1, LANES // HEAD_DIM))


def _sink_rows(sink, tq):
    return jnp.repeat(sink.reshape(N_KV_HEADS, 1, GQA_GROUP, 1), tq, axis=3).reshape(
        N_KV_HEADS, 1, GQA_GROUP * tq)


def kernel(x, c, ctx, c_ctx, norm_g, w_ada, b_ada, w_ffn_in, w_ffn_out, w_o, w_qkv_a, qk_norm_a,
           w_qkv_b, qk_norm_b, sink_b, w_qkv_c, qk_norm_c, diff_lambda, diff_subln):
    n_b, t, d = x.shape
    n_c = ctx.shape[1]
    depth = w_ada.shape[0]
    d_ff = w_ffn_out.shape[2]
    assert n_b + 1 <= 8 and t % 1024 == 0 and n_c % LANES == 0 and n_c <= 512

    tm = 512
    tq, tq_diff = 256, 512
    tq_win = 256
    tq_c = n_c

    c8 = jnp.zeros((8, d), F32).at[:n_b].set(c).at[n_b].set(c_ctx)
    mod5 = _ada_all(c8, w_ada, b_ada).reshape(depth, 8, 3, 3, d)

    rope_tabs = _rope_tables(t)
    eye = jnp.arange(MXU_DIM) // HEAD_DIM
    bd = (eye[:, None] == eye[None, :]).astype(BF16)
    w_in = w_ffn_in.astype(BF16)
    w_out = w_ffn_out.astype(BF16)
    wo = w_o.astype(BF16)
    norm_g4 = norm_g.reshape(depth, 3, 1, d)

    xs = x.reshape(n_b * t, d)
    cs = ctx.reshape(n_b * n_c, d)
    lat = dict(tm=tm, rows_per_batch=t, ctx_row=None)
    cx = dict(tm=n_c, rows_per_batch=n_c, ctx_row=n_b)

    for i in range(depth):
        last = i == depth - 1
        kind, j = i % 3, i // 3
        xs = _ffn(xs, mod5, norm_g4, w_in, w_out, i, 0, **lat)
        cs = _ffn(cs, mod5, norm_g4, w_in, w_out, i, 0, **cx)
        if kind == 2:
            wq, wk, wvt = _diff_weights(w_qkv_c[j])
            qk_g = qk_norm_c[j]
        else:
            wq, wk, wvt = _gqa_weights((w_qkv_a, w_qkv_b)[kind][j])
            qk_g = (qk_norm_a, qk_norm_b)[kind][j]
        gq, gk = _lane_gain(qk_g[0]), _lane_gain(qk_g[1])
        q, kl, vtl = _proj(xs, mod5, norm_g4, i, wq, wk, wvt, gq, gk, bd, rope_tabs, **lat)
        qc, kc, vtc = _proj(cs, mod5, norm_g4, i, wq, wk, wvt, gq, gk, bd, None, **cx)
        vtc = vtc.reshape(n_b, vtc.shape[2], n_c)
        if kind == 0:
            ot = _flash_attn(q, kc, vtc, kl, vtl, mode="gqa", n_batch=n_b, tq=tq)
            if not last:
                otc = _flash_attn(qc, kc, vtc, None, None, mode="gqa", n_batch=n_b, tq=tq_c)
        elif kind == 1:
            ot = _window_attn(q, kc, vtc, kl, vtl, _sink_rows(sink_b[j], tq_win), n_batch=n_b,
                              tq=tq_win)
            if not last:
                otc = _flash_attn(qc, kc, vtc, None, None, mode="gqa", n_batch=n_b, tq=tq_c,
                                  sink_rows=_sink_rows(sink_b[j], tq_c))
        else:
            extra = dict(lam_params=diff_lambda[j], subln=diff_subln[j],
                         lam_init=0.8 - 0.6 * math.exp(-0.3 * i))
            ot = _flash_attn(q, kc, vtc, kl, vtl, mode="diff", n_batch=n_b, tq=tq_diff, **extra)
            if not last:
                otc = _flash_attn(qc, kc, vtc, None, None, mode="diff", n_batch=n_b, tq=tq_c,
                                  **extra)
        xs = _ffn(xs, mod5, norm_g4, w_in, w_out, i, 1, pre=(ot, wo), **lat)
        if not last:
            cs = _ffn(cs, mod5, norm_g4, w_in, w_out, i, 1, pre=(otc, wo), **cx)
    return xs.reshape(n_b, t, d)
```

```python
import functools
import math

import jax
import jax.numpy as jnp
from jax import lax
from jax.experimental import pallas as pl
from jax.experimental.pallas import tpu as pltpu

F32 = jnp.float32
BF16 = jnp.bfloat16

HEAD_DIM = 64
N_KV_HEADS = 4
GQA_GROUP = 4
N_DIFF_HEADS = 8
GRID_W = 64
WINDOW = 128
ROPE_BASE = 10000.0
EPS = 1e-6
NEG_INF = -1e30
LOG2E = math.log2(math.e)
Q_SCALE = HEAD_DIM ** -0.5 * LOG2E
FFN_RESIDUAL = 0.5
N_MOD = 9
LANES = 128
MXU_DIM = 256
SUM_ROWS = 16
PIPE_SLOTS = 4
LOOKAHEAD = 2
VMEM_LIMIT = 56 * 1024 * 1024


def _cparams(*sem):
    return pltpu.CompilerParams(dimension_semantics=sem, vmem_limit_bytes=VMEM_LIMIT)


def _silu(a):
    return a * jax.nn.sigmoid(a)


def _norm_mod(x, g, shift, scale):
    y = x * lax.rsqrt(jnp.mean(x * x, axis=-1, keepdims=True) + EPS)
    return (y * g) * (1.0 + scale) + shift


def _ada_kernel(c_ref, w_ref, b_ref, o_ref):
    s = _silu(c_ref[...])
    o_ref[...] = jnp.dot(s, w_ref[...], preferred_element_type=F32) + b_ref[...]


def _ada_all(c8, w_ada, b_ada):
    depth, d, nd = w_ada.shape
    tn = nd // 8
    return pl.pallas_call(
        _ada_kernel,
        grid=(depth, nd // tn),
        in_specs=[
            pl.BlockSpec((8, d), lambda i, j: (0, 0)),
            pl.BlockSpec((None, d, tn), lambda i, j: (i, 0, j)),
            pl.BlockSpec((None, 1, tn), lambda i, j: (i, 0, j)),
        ],
        out_specs=pl.BlockSpec((None, 8, tn), lambda i, j: (i, 0, j)),
        out_shape=jax.ShapeDtypeStruct((depth, 8, nd), F32),
        compiler_params=_cparams("parallel", "parallel"),
        name="adaln",
    )(c8, w_ada, b_ada.reshape(depth, 1, nd))


def _ffn_kernel(*refs, pre):
    if pre:
        x_ref, ot_ref, wo_ref, modp_ref, mod_ref, g_ref, wa_ref, wu_ref, wout_ref, o_ref = refs
    else:
        x_ref, mod_ref, g_ref, wa_ref, wu_ref, wout_ref, o_ref = refs
    x = x_ref[...]
    if pre:
        x = x + modp_ref[2:3, :] * pl.dot(ot_ref[...], wo_ref[...], trans_a=True)
    h = _norm_mod(x, g_ref[...], mod_ref[0:1, :], mod_ref[1:2, :]).astype(BF16)
    a = jnp.dot(h, wa_ref[...], preferred_element_type=F32)
    u = jnp.dot(h, wu_ref[...], preferred_element_type=F32)
    act = (_silu(a) * u).astype(BF16)
    y = jnp.dot(act, wout_ref[...], preferred_element_type=F32)
    o_ref[...] = x + (FFN_RESIDUAL * mod_ref[2:3, :]) * y


def _ffn(x2, mod5, norm_g4, w_in, w_out, layer, half, *, tm, rows_per_batch, ctx_row, pre=None):
    r, d = x2.shape
    dff = w_out.shape[2]
    tpb = rows_per_batch // tm
    k = 2 * half
    resident = dict(pipeline_mode=pl.Buffered(1))

    def bidx(i):
        return ctx_row if ctx_row is not None else i // tpb

    def mod_spec(kk):
        return pl.BlockSpec((None, None, None, 3, d), lambda i: (layer, bidx(i), kk, 0, 0))

    in_specs = [pl.BlockSpec((tm, d), lambda i: (i, 0))]
    args = [x2]
    if pre is not None:
        ot, wo = pre
        in_specs += [
            pl.BlockSpec((None, d, tm), lambda i: (i // tpb, 0, i % tpb)),
            pl.BlockSpec((None, d, d), lambda i: (layer, 0, 0), **resident),
            mod_spec(1),
        ]
        args += [ot, wo, mod5]
    in_specs += [
        mod_spec(k),
        pl.BlockSpec((None, None, 1, d), lambda i: (layer, k, 0, 0)),
        pl.BlockSpec((None, None, d, dff), lambda i: (layer, half, 0, 0), **resident),
        pl.BlockSpec((None, None, d, dff), lambda i: (layer, half, 0, 1), **resident),
        pl.BlockSpec((None, None, dff, d), lambda i: (layer, half, 0, 0), **resident),
    ]
    args += [mod5, norm_g4, w_in, w_in, w_out]
    return pl.pallas_call(
        functools.partial(_ffn_kernel, pre=pre is not None),
        grid=(r // tm,),
        in_specs=in_specs,
        out_specs=pl.BlockSpec((tm, d), lambda i: (i, 0)),
        out_shape=jax.ShapeDtypeStruct((r, d), F32),
        compiler_params=_cparams("parallel"),
        name="ffn_pre" if pre is not None else "ffn",
    )(*args)


def _headnorm_rope(z, gain, cos, sin, bd, rope):
    tm, w = z.shape
    lane = lax.broadcasted_iota(jnp.int32, (1, LANES), 1)
    first_half = (lane & 31) < 16
    outs = []
    for j in range(w // MXU_DIM):
        zj = z[:, j * MXU_DIM:(j + 1) * MXU_DIM]
        sq = zj * zj
        hi = sq.astype(BF16)
        lo = (sq - hi.astype(F32)).astype(BF16)
        ss = (jnp.dot(hi, bd, preferred_element_type=F32)
              + jnp.dot(lo, bd, preferred_element_type=F32))
        zn = zj * lax.rsqrt(ss * (1.0 / HEAD_DIM) + EPS)
        for half in range(MXU_DIM // LANES):
            t = zn[:, half * LANES:(half + 1) * LANES] * gain
            if rope:
                partner = jnp.where(first_half, pltpu.roll(t, LANES - 16, 1),
                                    pltpu.roll(t, 16, 1))
                t = t * cos + partner * sin
            outs.append(t)
    return jnp.concatenate(outs, axis=1)


def _proj_kernel(*refs, rope):
    if rope:
        (x_ref, mod_ref, g_ref, wq_ref, wk_ref, wvt_ref, gq_ref, gk_ref, bd_ref, cos_ref,
         sin_ref, q_ref, k_ref, vt_ref) = refs
        cos, sin = cos_ref[...], sin_ref[...]
    else:
        (x_ref, mod_ref, g_ref, wq_ref, wk_ref, wvt_ref, gq_ref, gk_ref, bd_ref,
         q_ref, k_ref, vt_ref) = refs
        cos = sin = None
    h = _norm_mod(x_ref[...], g_ref[...], mod_ref[0:1, :], mod_ref[1:2, :]).astype(BF16)
    bd = bd_ref[...]
    q = jnp.dot(h, wq_ref[...], preferred_element_type=F32)
    q_ref[...] = _headnorm_rope(q, gq_ref[...] * Q_SCALE, cos, sin, bd, rope).astype(BF16)
    k = jnp.dot(h, wk_ref[...], preferred_element_type=F32)
    k_ref[...] = _headnorm_rope(k, gk_ref[...], cos, sin, bd, rope).astype(BF16)
    vt_ref[...] = pl.dot(wvt_ref[...], h, trans_b=True).astype(BF16)


def _proj(x2, mod5, norm_g4, layer, wq, wk, wvt, gq, gk, bd, rope_tabs, *, tm, rows_per_batch,
          ctx_row):
    r, d = x2.shape
    wq_n, wk_n, wv_n = wq.shape[1], wk.shape[1], wvt.shape[0]
    tpb = rows_per_batch // tm
    nb = r // rows_per_batch
    rope = rope_tabs is not None

    def bidx(i):
        return ctx_row if ctx_row is not None else i // tpb

    const = lambda i: (0, 0)
    in_specs = [
        pl.BlockSpec((tm, d), lambda i: (i, 0)),
        pl.BlockSpec((None, None, None, 3, d), lambda i: (layer, bidx(i), 1, 0, 0)),
        pl.BlockSpec((None, None, 1, d), lambda i: (layer, 1, 0, 0)),
        pl.BlockSpec((d, wq_n), const),
        pl.BlockSpec((d, wk_n), const),
        pl.BlockSpec((wv_n, d), const),
        pl.BlockSpec((1, LANES), const),
        pl.BlockSpec((1, LANES), const),
        pl.BlockSpec((MXU_DIM, MXU_DIM), const),
    ]
    args = [x2, mod5, norm_g4, wq, wk, wvt, gq, gk, bd]
    if rope:
        in_specs += [pl.BlockSpec((tm, LANES), lambda i: (i % tpb, 0))] * 2
        args += list(rope_tabs)
    return pl.pallas_call(
        functools.partial(_proj_kernel, rope=rope),
        grid=(r // tm,),
        in_specs=in_specs,
        out_specs=[
            pl.BlockSpec((tm, wq_n), lambda i: (i, 0)),
            pl.BlockSpec((tm, wk_n), lambda i: (i, 0)),
            pl.BlockSpec((None, None, wv_n, tm), lambda i: (i // tpb, i % tpb, 0, 0)),
        ],
        out_shape=[
            jax.ShapeDtypeStruct((r, wq_n), BF16),
            jax.ShapeDtypeStruct((r, wk_n), BF16),
            jax.ShapeDtypeStruct((nb, tpb, wv_n, tm), BF16),
        ],
        compiler_params=_cparams("parallel"),
        name="qkv_proj",
    )(*args)


def _slot_mask(slot):
    lane = lax.broadcasted_iota(jnp.int32, (1, LANES), 1)
    return (lane >= slot * HEAD_DIM) & (lane < (slot + 1) * HEAD_DIM)


def _gqa_q_tile(q_ref, kvh, g):
    j = g + GQA_GROUP * (kvh // 2)
    tile = q_ref[:, j * LANES:(j + 1) * LANES]
    return jnp.where(_slot_mask(kvh % 2), tile, jnp.zeros_like(tile))


_GQA_UNITS = tuple(
    (kvh // 2, kvh * HEAD_DIM,
     tuple((g + GQA_GROUP * (kvh // 2), kvh % 2) for g in range(GQA_GROUP)))
    for kvh in range(N_KV_HEADS))
_DIFF_UNITS = tuple((h, h * 2 * HEAD_DIM, ((h, 0), (h, 1))) for h in range(2))


def _flash_kernel(*refs, units, dv, tq, tk, n_kb, mode, has_sink, lam_init):
    refs = list(refs)
    q_ref, kc_ref, vtc_ref = refs[:3]
    pos = 3
    has_lat = n_kb > 0
    if has_lat:
        kl_ref, vtl_ref = refs[pos:pos + 2]
        pos += 2
    if mode == "diff":
        lam_ref, subln_ref = refs[pos:pos + 2]
        pos += 2
    elif has_sink:
        sink_ref = refs[pos]
        pos += 1
    o_ref, qs_scr, m_scr, acc_scr = refs[pos:pos + 4]
    if has_lat:
        s_scr, mx_scr = refs[pos + 4:]
    n_u = len(units)

    def pv(vt, p):
        ones = jnp.ones((SUM_ROWS, vt.shape[1]), BF16)
        return jnp.dot(jnp.concatenate([vt, ones], axis=0), p.astype(BF16),
                       preferred_element_type=F32)

    for u, (_, _, q_tiles) in enumerate(units):
        for t, (j, slot) in enumerate(q_tiles):
            tile = q_ref[:, j * LANES:(j + 1) * LANES]
            tile = jnp.where(_slot_mask(slot), tile, jnp.zeros_like(tile))
            qs_scr[u, :, t * tq:(t + 1) * tq] = tile.astype(F32).T.astype(BF16)

    for u, (half, v0, _) in enumerate(units):
        s = jnp.dot(kc_ref[:, half * LANES:(half + 1) * LANES], qs_scr[u],
                    preferred_element_type=F32)
        m = jnp.max(s, axis=0, keepdims=True)
        p = jnp.exp2(s - m)
        m_scr[u] = m
        acc_scr[u] = pv(vtc_ref[v0:v0 + dv, :], p)

    if has_lat:
        col_tiles = [slice(c0, c0 + MXU_DIM) for c0 in range(0, s_scr.shape[2], MXU_DIM)]

        def scores(kb, u, slot, cs):
            half = units[u][0]
            row0 = pl.multiple_of(kb * tk, tk)
            k = kl_ref[pl.ds(row0, tk), half * LANES:(half + 1) * LANES]
            s = jnp.dot(k, qs_scr[u, :, cs], preferred_element_type=F32)
            s_scr[slot, :, cs] = s
            mx_scr[slot, :, cs] = jnp.max(s, axis=0, keepdims=True)

        def probs(u, slot, cs):
            m_prev = m_scr[u, :, cs]
            m_new = jnp.maximum(m_prev, mx_scr[slot, :, cs])
            m_scr[u, :, cs] = m_new
            return (jnp.exp2(s_scr[slot, :, cs] - m_new).astype(BF16),
                    jnp.exp2(m_prev - m_new))

        def accumulate(kb, u, cs, p, alpha):
            v0 = units[u][1]
            acc_scr[u, :, cs] = alpha * acc_scr[u, :, cs] + pv(vtl_ref[kb, v0:v0 + dv, :], p)

        bpt = PIPE_SLOTS // n_u
        n_trips = n_kb // bpt

        def trip(it, last):
            pending = None
            for j in range(PIPE_SLOTS):
                ahead = j + LOOKAHEAD
                for cs in col_tiles:
                    if ahead < PIPE_SLOTS:
                        scores(it * bpt + ahead // n_u, ahead % n_u, ahead, cs)
                    elif not last:
                        a2 = ahead - PIPE_SLOTS
                        scores((it + 1) * bpt + a2 // n_u, a2 % n_u, a2, cs)
                    p, alpha = probs(j % n_u, j, cs)
                    if pending is not None:
                        accumulate(*pending)
                    pending = (it * bpt + j // n_u, j % n_u, cs, p, alpha)
            accumulate(*pending)

        for j in range(LOOKAHEAD):
            for cs in col_tiles:
                scores(j // n_u, j % n_u, j, cs)
        lax.fori_loop(0, n_trips - 1, lambda it, c: (trip(it, False), c)[1], 0)
        trip(n_trips - 1, True)

    if mode == "diff":
        lp = lam_ref[...]
        lam = (jnp.exp(jnp.sum(lp[0:1] * lp[1:2], axis=1, keepdims=True))
               - jnp.exp(jnp.sum(lp[2:3] * lp[3:4], axis=1, keepdims=True)) + lam_init)
        gain = subln_ref[...]
        for u in range(n_u):
            o12 = acc_scr[u, :dv, :] / acc_scr[u, dv:dv + 1, :]
            o = o12[:, :tq] - lam * o12[:, tq:]
            o = o * lax.rsqrt(jnp.mean(o * o, axis=0, keepdims=True) + EPS)
            o_ref[u * dv:(u + 1) * dv, :] = ((o * gain) * (1.0 - lam_init)).astype(BF16)
    else:
        for u in range(n_u):
            m, l, acc = m_scr[u], acc_scr[u, dv:dv + 1, :], acc_scr[u, :dv, :]
            if has_sink:
                sk = sink_ref[u] * LOG2E
                m_f = jnp.maximum(m, sk)
                w = jnp.exp2(m - m_f)
                l = l * w + jnp.exp2(sk - m_f)
                acc = acc * w
            o = acc / l
            for g in range(GQA_GROUP):
                row = (u * GQA_GROUP + g) * HEAD_DIM
                o_ref[row:row + HEAD_DIM, :] = o[:, g * tq:(g + 1) * tq].astype(BF16)


def _flash_attn(q, kc, vtc, kl, vtl, *, mode, n_batch, tq, sink_rows=None, lam_params=None,
                subln=None, lam_init=0.0):
    r, dq = q.shape
    t_q = r // n_batch
    c = kc.shape[0] // n_batch
    wk = kc.shape[1]
    n_grp = wk // MXU_DIM
    q_cols = dq // n_grp
    units, dv = (_GQA_UNITS, HEAD_DIM) if mode == "gqa" else (_DIFF_UNITS, 2 * HEAD_DIM)
    cols = len(units[0][2]) * tq
    has_lat = kl is not None
    nq = t_q // tq
    in_specs = [
        pl.BlockSpec((tq, q_cols), lambda b, g, i: (b * nq + i, g)),
        pl.BlockSpec((c, MXU_DIM), lambda b, g, i: (b, g)),
        pl.BlockSpec((None, MXU_DIM, c), lambda b, g, i: (b, g, 0)),
    ]
    args = [q, kc, vtc]
    n_kb, tk = 0, 0
    scratch = [
        pltpu.VMEM((len(units), LANES, cols), BF16),
        pltpu.VMEM((len(units), 1, cols), F32),
        pltpu.VMEM((len(units), dv + SUM_ROWS, cols), F32),
    ]
    if has_lat:
        t = kl.shape[0] // n_batch
        n_kb, tk = vtl.shape[1], vtl.shape[3]
        in_specs += [
            pl.BlockSpec((t, MXU_DIM), lambda b, g, i: (b, g)),
            pl.BlockSpec((None, n_kb, MXU_DIM, tk), lambda b, g, i: (b, 0, g, 0)),
        ]
        args += [kl, vtl]
        assert PIPE_SLOTS % len(units) == 0 and n_kb % (PIPE_SLOTS // len(units)) == 0
        scratch += [pltpu.VMEM((PIPE_SLOTS, tk, cols), F32),
                    pltpu.VMEM((PIPE_SLOTS, 1, cols), F32)]
    if mode == "diff":
        in_specs += [
            pl.BlockSpec((4, HEAD_DIM), lambda b, g, i: (0, 0)),
            pl.BlockSpec((dv, 1), lambda b, g, i: (0, 0)),
        ]
        args += [lam_params, subln.reshape(dv, 1)]
    elif sink_rows is not None:
        in_specs.append(pl.BlockSpec((N_KV_HEADS, 1, cols), lambda b, g, i: (0, 0, 0)))
        args.append(sink_rows)
    return pl.pallas_call(
        functools.partial(_flash_kernel, units=units, dv=dv, tq=tq, tk=tk, n_kb=n_kb, mode=mode,
                          has_sink=sink_rows is not None, lam_init=lam_init),
        grid=(n_batch, n_grp, nq),
        in_specs=in_specs,
        out_specs=pl.BlockSpec((None, q_cols, tq), lambda b, g, i: (b, g, i)),
        out_shape=jax.ShapeDtypeStruct((n_batch, dq, t_q), BF16),
        scratch_shapes=scratch,
        compiler_params=_cparams("parallel", "parallel", "arbitrary"),
        name=mode + "_attn",
    )(*args)


def _window_kernel(q_ref, kc_ref, vtc_ref, kp_ref, kcur_ref, kn_ref, vtp_ref, vtcur_ref,
                   vtn_ref, sink_ref, o_ref, *, tq):
    qi = pl.program_id(1)
    nq = pl.num_programs(1)
    c = kc_ref.shape[0]
    n_lat = tq + 2 * WINDOW
    cols = GQA_GROUP * tq
    rr = lax.broadcasted_iota(jnp.int32, (n_lat, 1), 0)
    cc = lax.broadcasted_iota(jnp.int32, (1, cols), 1) & (tq - 1)
    rel = rr - WINDOW - cc
    valid = (jnp.abs(rel) <= WINDOW)
    valid &= (rr >= WINDOW) | (qi > 0)
    valid &= (rr < tq + WINDOW) | (qi < nq - 1)
    for kvh in range(N_KV_HEADS):
        half = kvh // 2
        sl = slice(half * LANES, (half + 1) * LANES)
        qs = jnp.concatenate([_gqa_q_tile(q_ref, kvh, g) for g in range(GQA_GROUP)], axis=0)
        s_ctx = pl.dot(kc_ref[:, sl], qs, trans_b=True)
        k_lat = jnp.concatenate([kp_ref[:, sl], kcur_ref[:, sl], kn_ref[:, sl]], axis=0)
        s_lat = jnp.where(valid, pl.dot(k_lat, qs, trans_b=True), NEG_INF)
        sk = sink_ref[kvh] * LOG2E
        m = jnp.maximum(jnp.maximum(jnp.max(s_ctx, axis=0, keepdims=True),
                                    jnp.max(s_lat, axis=0, keepdims=True)), sk)
        p_ctx = jnp.exp2(s_ctx - m)
        p_lat = jnp.exp2(s_lat - m)
        l = (jnp.sum(p_ctx, axis=0, keepdims=True) + jnp.sum(p_lat, axis=0, keepdims=True)
             + jnp.exp2(sk - m))
        hs = slice(kvh * HEAD_DIM, (kvh + 1) * HEAD_DIM)
        vt_lat = jnp.concatenate([vtp_ref[hs, :], vtcur_ref[hs, :], vtn_ref[hs, :]], axis=1)
        acc = (jnp.dot(vtc_ref[hs, :], p_ctx.astype(BF16), preferred_element_type=F32)
               + jnp.dot(vt_lat, p_lat.astype(BF16), preferred_element_type=F32))
        o = acc / l
        for g in range(GQA_GROUP):
            row = (kvh * GQA_GROUP + g) * HEAD_DIM
            o_ref[row:row + HEAD_DIM, :] = o[:, g * tq:(g + 1) * tq].astype(BF16)


def _window_attn(q, kc, vtc, kl, vtl, sink_rows, *, n_batch, tq):
    r, dq = q.shape
    t = r // n_batch
    c = kc.shape[0] // n_batch
    wk = kc.shape[1]
    tkv = vtl.shape[3]
    nq = t // tq
    rb = tq // WINDOW
    nwb = t // WINDOW
    wpb, qpb = tkv // WINDOW, tkv // tq
    prev = lambda b, i: jnp.maximum(i * rb - 1, 0)
    nxt = lambda b, i: jnp.minimum((i + 1) * rb, nwb - 1)
    in_specs = [
        pl.BlockSpec((tq, dq), lambda b, i: (b * nq + i, 0)),
        pl.BlockSpec((c, wk), lambda b, i: (b, 0)),
        pl.BlockSpec((None, wk, c), lambda b, i: (b, 0, 0)),
        pl.BlockSpec((WINDOW, wk), lambda b, i: (b * nwb + prev(b, i), 0)),
        pl.BlockSpec((tq, wk), lambda b, i: (b * nq + i, 0)),
        pl.BlockSpec((WINDOW, wk), lambda b, i: (b * nwb + nxt(b, i), 0)),
        pl.BlockSpec((None, None, wk, WINDOW),
                     lambda b, i: (b, prev(b, i) // wpb, 0, prev(b, i) % wpb)),
        pl.BlockSpec((None, None, wk, tq), lambda b, i: (b, i // qpb, 0, i % qpb)),
        pl.BlockSpec((None, None, wk, WINDOW),
                     lambda b, i: (b, nxt(b, i) // wpb, 0, nxt(b, i) % wpb)),
        pl.BlockSpec((N_KV_HEADS, 1, GQA_GROUP * tq), lambda b, i: (0, 0, 0)),
    ]
    return pl.pallas_call(
        functools.partial(_window_kernel, tq=tq),
        grid=(n_batch, nq),
        in_specs=in_specs,
        out_specs=pl.BlockSpec((None, dq, tq), lambda b, i: (b, 0, i)),
        out_shape=jax.ShapeDtypeStruct((n_batch, dq, t), BF16),
        compiler_params=_cparams("parallel", "parallel"),
        name="window_attn",
    )(q, kc, vtc, kl, kl, kl, vtl, vtl, vtl, sink_rows)


def _rope_tables(n_tokens):
    rows = n_tokens // GRID_W
    row = jnp.repeat(jnp.arange(rows), GRID_W)
    col = jnp.tile(jnp.arange(GRID_W), rows)
    n_freq = HEAD_DIM // 4
    inv_freq = ROPE_BASE ** (-jnp.arange(n_freq, dtype=F32) / n_freq)
    ang = jnp.stack([row, col], axis=-1).astype(F32)[:, :, None] * inv_freq
    cos, sin = jnp.cos(ang), jnp.sin(ang)
    cos_h = jnp.concatenate([cos, cos], axis=-1).reshape(n_tokens, HEAD_DIM)
    sin_h = jnp.concatenate([-sin, sin], axis=-1).reshape(n_tokens, HEAD_DIM)
    return jnp.tile(cos_h, (1, 2)), jnp.tile(sin_h, (1, 2))


def _gqa_weights(w_qkv):
    d = w_qkv.shape[0]
    n_q = N_KV_HEADS * GQA_GROUP * HEAD_DIM
    n_kv = N_KV_HEADS * HEAD_DIM
    wq = w_qkv[:, :n_q].reshape(d, N_KV_HEADS // 2, 2, GQA_GROUP, HEAD_DIM)
    wq = wq.transpose(0, 1, 3, 2, 4).reshape(d, n_q)
    wk = w_qkv[:, n_q:n_q + n_kv]
    wvt = w_qkv[:, n_q + n_kv:].T
    return wq.astype(BF16), wk.astype(BF16), wvt.astype(BF16)


def _diff_weights(w_qkv):
    n = w_qkv.shape[1] // 3
    return (w_qkv[:, :n].astype(BF16), w_qkv[:, n:2 * n].astype(BF16),
            w_qkv[:, 2 * n:].T.astype(BF16))


def _lane_gain(g):
    return jnp.tile(g.reshape(1, HEAD_DIM), (1, LANES // HEAD_DIM))


def _sink_rows(sink, tq):
    return jnp.repeat(sink.reshape(N_KV_HEADS, 1, GQA_GROUP, 1), tq, axis=3).reshape(
        N_KV_HEADS, 1, GQA_GROUP * tq)


def kernel(x, c, ctx, c_ctx, norm_g, w_ada, b_ada, w_ffn_in, w_ffn_out, w_o, w_qkv_a, qk_norm_a,
           w_qkv_b, qk_norm_b, sink_b, w_qkv_c, qk_norm_c, diff_lambda, diff_subln):
    n_b, t, d = x.shape
    n_c = ctx.shape[1]
    depth = w_ada.shape[0]
    d_ff = w_ffn_out.shape[2]
    assert n_b + 1 <= 8 and t % 1024 == 0 and n_c % LANES == 0 and n_c <= 512

    tm = 512
    tq, tq_diff = 256, 512
    tq_win = 256
    tq_c = n_c

    c8 = jnp.zeros((8, d), F32).at[:n_b].set(c).at[n_b].set(c_ctx)
    mod5 = _ada_all(c8, w_ada, b_ada).reshape(depth, 8, 3, 3, d)

    rope_tabs = _rope_tables(t)
    eye = jnp.arange(MXU_DIM) // HEAD_DIM
    bd = (eye[:, None] == eye[None, :]).astype(BF16)
    w_in = w_ffn_in.astype(BF16)
    w_out = w_ffn_out.astype(BF16)
    wo = w_o.astype(BF16)
    norm_g4 = norm_g.reshape(depth, 3, 1, d)

    xs = x.reshape(n_b * t, d)
    cs = ctx.reshape(n_b * n_c, d)
    lat = dict(tm=tm, rows_per_batch=t, ctx_row=None)
    cx = dict(tm=n_c, rows_per_batch=n_c, ctx_row=n_b)

    for i in range(depth):
        last = i == depth - 1
        kind, j = i % 3, i // 3
        xs = _ffn(xs, mod5, norm_g4, w_in, w_out, i, 0, **lat)
        cs = _ffn(cs, mod5, norm_g4, w_in, w_out, i, 0, **cx)
        if kind == 2:
            wq, wk, wvt = _diff_weights(w_qkv_c[j])
            qk_g = qk_norm_c[j]
        else:
            wq, wk, wvt = _gqa_weights((w_qkv_a, w_qkv_b)[kind][j])
            qk_g = (qk_norm_a, qk_norm_b)[kind][j]
        gq, gk = _lane_gain(qk_g[0]), _lane_gain(qk_g[1])
        q, kl, vtl = _proj(xs, mod5, norm_g4, i, wq, wk, wvt, gq, gk, bd, rope_tabs, **lat)
        qc, kc, vtc = _proj(cs, mod5, norm_g4, i, wq, wk, wvt, gq, gk, bd, None, **cx)
        vtc = vtc.reshape(n_b, vtc.shape[2], n_c)
        if kind == 0:
            ot = _flash_attn(q, kc, vtc, kl, vtl, mode="gqa", n_batch=n_b, tq=tq)
            if not last:
                otc = _flash_attn(qc, kc, vtc, None, None, mode="gqa", n_batch=n_b, tq=tq_c)
        elif kind == 1:
            ot = _window_attn(q, kc, vtc, kl, vtl, _sink_rows(sink_b[j], tq_win), n_batch=n_b,
                              tq=tq_win)
            if not last:
                otc = _flash_attn(qc, kc, vtc, None, None, mode="gqa", n_batch=n_b, tq=tq_c,
                                  sink_rows=_sink_rows(sink_b[j], tq_c))
        else:
            extra = dict(lam_params=diff_lambda[j], subln=diff_subln[j],
                         lam_init=0.8 - 0.6 * math.exp(-0.3 * i))
            ot = _flash_attn(q, kc, vtc, kl, vtl, mode="diff", n_batch=n_b, tq=tq_diff, **extra)
            if not last:
                otc = _flash_attn(qc, kc, vtc, None, None, mode="diff", n_batch=n_b, tq=tq_c,
                                  **extra)
        xs = _ffn(xs, mod5, norm_g4, w_in, w_out, i, 1, pre=(ot, wo), **lat)
        if not last:
            cs = _ffn(cs, mod5, norm_g4, w_in, w_out, i, 1, pre=(otc, wo), **cx)
    return xs.reshape(n_b, t, d)
```

```python
import functools
import math

import jax
import jax.numpy as jnp
from jax import lax
from jax.experimental import pallas as pl
from jax.experimental.pallas import tpu as pltpu

F32 = jnp.float32
BF16 = jnp.bfloat16

HEAD_DIM = 64
N_KV_HEADS = 4
GQA_GROUP = 4
N_DIFF_HEADS = 8
GRID_W = 64
WINDOW = 128
ROPE_BASE = 10000.0
EPS = 1e-6
NEG_INF = -1e30
LOG2E = math.log2(math.e)
Q_SCALE = HEAD_DIM ** -0.5 * LOG2E
FFN_RESIDUAL = 0.5
N_MOD = 9
LANES = 128
MXU_DIM = 256
SUM_ROWS = 16
PIPE_SLOTS = 4
LOOKAHEAD = 2
VMEM_LIMIT = 56 * 1024 * 1024


def _cparams(*sem):
    return pltpu.CompilerParams(dimension_semantics=sem, vmem_limit_bytes=VMEM_LIMIT)


def _silu(a):
    return a * jax.nn.sigmoid(a)


def _norm_mod(x, g, shift, scale):
    y = x * lax.rsqrt(jnp.mean(x * x, axis=-1, keepdims=True) + EPS)
    return (y * g) * (1.0 + scale) + shift


def _ada_kernel(c_ref, w_ref, b_ref, o_ref):
    s = _silu(c_ref[...])
    o_ref[...] = jnp.dot(s, w_ref[...], preferred_element_type=F32) + b_ref[...]


def _ada_all(c8, w_ada, b_ada):
    depth, d, nd = w_ada.shape
    tn = nd // 8
    return pl.pallas_call(
        _ada_kernel,
        grid=(depth, nd // tn),
        in_specs=[
            pl.BlockSpec((8, d), lambda i, j: (0, 0)),
            pl.BlockSpec((None, d, tn), lambda i, j: (i, 0, j)),
            pl.BlockSpec((None, 1, tn), lambda i, j: (i, 0, j)),
        ],
        out_specs=pl.BlockSpec((None, 8, tn), lambda i, j: (i, 0, j)),
        out_shape=jax.ShapeDtypeStruct((depth, 8, nd), F32),
        compiler_params=_cparams("parallel", "parallel"),
        name="adaln",
    )(c8, w_ada, b_ada.reshape(depth, 1, nd))


def _ffn_kernel(*refs, pre):
    if pre:
        x_ref, ot_ref, wo_ref, modp_ref, mod_ref, g_ref, wa_ref, wu_ref, wout_ref, o_ref = refs
    else:
        x_ref, mod_ref, g_ref, wa_ref, wu_ref, wout_ref, o_ref = refs
    x = x_ref[...]
    if pre:
        x = x + modp_ref[2:3, :] * pl.dot(ot_ref[...], wo_ref[...], trans_a=True)
    h = _norm_mod(x, g_ref[...], mod_ref[0:1, :], mod_ref[1:2, :]).astype(BF16)
    a = jnp.dot(h, wa_ref[...], preferred_element_type=F32)
    u = jnp.dot(h, wu_ref[...], preferred_element_type=F32)
    act = (_silu(a) * u).astype(BF16)
    y = jnp.dot(act, wout_ref[...], preferred_element_type=F32)
    o_ref[...] = x + (FFN_RESIDUAL * mod_ref[2:3, :]) * y


def _ffn(x2, mod5, norm_g4, w_in, w_out, layer, half, *, tm, rows_per_batch, ctx_row, pre=None):
    r, d = x2.shape
    dff = w_out.shape[2]
    tpb = rows_per_batch // tm
    k = 2 * half
    resident = dict(pipeline_mode=pl.Buffered(1))

    def bidx(i):
        return ctx_row if ctx_row is not None else i // tpb

    def mod_spec(kk):
        return pl.BlockSpec((None, None, None, 3, d), lambda i: (layer, bidx(i), kk, 0, 0))

    in_specs = [pl.BlockSpec((tm, d), lambda i: (i, 0))]
    args = [x2]
    if pre is not None:
        ot, wo = pre
        in_specs += [
            pl.BlockSpec((None, d, tm), lambda i: (i // tpb, 0, i % tpb)),
            pl.BlockSpec((None, d, d), lambda i: (layer, 0, 0), **resident),
            mod_spec(1),
        ]
        args += [ot, wo, mod5]
    in_specs += [
        mod_spec(k),
        pl.BlockSpec((None, None, 1, d), lambda i: (layer, k, 0, 0)),
        pl.BlockSpec((None, None, d, dff), lambda i: (layer, half, 0, 0), **resident),
        pl.BlockSpec((None, None, d, dff), lambda i: (layer, half, 0, 1), **resident),
        pl.BlockSpec((None, None, dff, d), lambda i: (layer, half, 0, 0), **resident),
    ]
    args += [mod5, norm_g4, w_in, w_in, w_out]
    return pl.pallas_call(
        functools.partial(_ffn_kernel, pre=pre is not None),
        grid=(r // tm,),
        in_specs=in_specs,
        out_specs=pl.BlockSpec((tm, d), lambda i: (i, 0)),
        out_shape=jax.ShapeDtypeStruct((r, d), F32),
        compiler_params=_cparams("parallel"),
        name="ffn_pre" if pre is not None else "ffn",
    )(*args)


def _headnorm_rope(z, gain, cos, sin, bd, rope):
    tm, w = z.shape
    lane = lax.broadcasted_iota(jnp.int32, (1, LANES), 1)
    first_half = (lane & 31) < 16
    outs = []
    for j in range(w // MXU_DIM):
        zj = z[:, j * MXU_DIM:(j + 1) * MXU_DIM]
        sq = zj * zj
        hi = sq.astype(BF16)
        lo = (sq - hi.astype(F32)).astype(BF16)
        ss = (jnp.dot(hi, bd, preferred_element_type=F32)
              + jnp.dot(lo, bd, preferred_element_type=F32))
        zn = zj * lax.rsqrt(ss * (1.0 / HEAD_DIM) + EPS)
        for half in range(MXU_DIM // LANES):
            t = zn[:, half * LANES:(half + 1) * LANES] * gain
            if rope:
                partner = jnp.where(first_half, pltpu.roll(t, LANES - 16, 1),
                                    pltpu.roll(t, 16, 1))
                t = t * cos + partner * sin
            outs.append(t)
    return jnp.concatenate(outs, axis=1)


def _proj_kernel(*refs, rope):
    if rope:
        (x_ref, mod_ref, g_ref, wq_ref, wk_ref, wvt_ref, gq_ref, gk_ref, bd_ref, cos_ref,
         sin_ref, q_ref, k_ref, vt_ref) = refs
        cos, sin = cos_ref[...], sin_ref[...]
    else:
        (x_ref, mod_ref, g_ref, wq_ref, wk_ref, wvt_ref, gq_ref, gk_ref, bd_ref,
         q_ref, k_ref, vt_ref) = refs
        cos = sin = None
    h = _norm_mod(x_ref[...], g_ref[...], mod_ref[0:1, :], mod_ref[1:2, :]).astype(BF16)
    bd = bd_ref[...]
    q = jnp.dot(h, wq_ref[...], preferred_element_type=F32)
    q_ref[...] = _headnorm_rope(q, gq_ref[...] * Q_SCALE, cos, sin, bd, rope).astype(BF16)
    k = jnp.dot(h, wk_ref[...], preferred_element_type=F32)
    k_ref[...] = _headnorm_rope(k, gk_ref[...], cos, sin, bd, rope).astype(BF16)
    vt = pl.dot(wvt_ref[...], h, trans_b=True).astype(BF16)
    tkv = vt_ref.shape[2]
    for c in range(vt_ref.shape[0]):
        vt_ref[c] = vt[:, c * tkv:(c + 1) * tkv]


def _proj(x2, mod5, norm_g4, layer, wq, wk, wvt, gq, gk, bd, rope_tabs, *, tm, rows_per_batch,
          ctx_row, tkv):
    r, d = x2.shape
    wq_n, wk_n, wv_n = wq.shape[1], wk.shape[1], wvt.shape[0]
    tpb = rows_per_batch // tm
    kpt = tm // tkv
    nb = r // rows_per_batch
    rope = rope_tabs is not None

    def bidx(i):
        return ctx_row if ctx_row is not None else i // tpb

    const = lambda i: (0, 0)
    in_specs = [
        pl.BlockSpec((tm, d), lambda i: (i, 0)),
        pl.BlockSpec((None, None, None, 3, d), lambda i: (layer, bidx(i), 1, 0, 0)),
        pl.BlockSpec((None, None, 1, d), lambda i: (layer, 1, 0, 0)),
        pl.BlockSpec((d, wq_n), const),
        pl.BlockSpec((d, wk_n), const),
        pl.BlockSpec((wv_n, d), const),
        pl.BlockSpec((1, LANES), const),
        pl.BlockSpec((1, LANES), const),
        pl.BlockSpec((MXU_DIM, MXU_DIM), const),
    ]
    args = [x2, mod5, norm_g4, wq, wk, wvt, gq, gk, bd]
    if rope:
        in_specs += [pl.BlockSpec((tm, LANES), lambda i: (i % tpb, 0))] * 2
        args += list(rope_tabs)
    return pl.pallas_call(
        functools.partial(_proj_kernel, rope=rope),
        grid=(r // tm,),
        in_specs=in_specs,
        out_specs=[
            pl.BlockSpec((tm, wq_n), lambda i: (i, 0)),
            pl.BlockSpec((tm, wk_n), lambda i: (i, 0)),
            pl.BlockSpec((None, kpt, wv_n, tkv), lambda i: (i // tpb, i % tpb, 0, 0)),
        ],
        out_shape=[
            jax.ShapeDtypeStruct((r, wq_n), BF16),
            jax.ShapeDtypeStruct((r, wk_n), BF16),
            jax.ShapeDtypeStruct((nb, tpb * kpt, wv_n, tkv), BF16),
        ],
        compiler_params=_cparams("parallel"),
        name="qkv_proj",
    )(*args)


def _slot_mask(slot):
    lane = lax.broadcasted_iota(jnp.int32, (1, LANES), 1)
    return (lane >= slot * HEAD_DIM) & (lane < (slot + 1) * HEAD_DIM)


def _gqa_q_tile(q_ref, kvh, g):
    j = g + GQA_GROUP * (kvh // 2)
    tile = q_ref[:, j * LANES:(j + 1) * LANES]
    return jnp.where(_slot_mask(kvh % 2), tile, jnp.zeros_like(tile))


_GQA_UNITS = tuple(
    (kvh // 2, kvh * HEAD_DIM,
     tuple((g + GQA_GROUP * (kvh // 2), kvh % 2) for g in range(GQA_GROUP)))
    for kvh in range(N_KV_HEADS))
_DIFF_UNITS = tuple((h, h * 2 * HEAD_DIM, ((h, 0), (h, 1))) for h in range(2))


def _flash_kernel(*refs, units, dv, tq, tk, n_kb, mode, has_sink, lam_init):
    refs = list(refs)
    q_ref, kc_ref, vtc_ref = refs[:3]
    pos = 3
    has_lat = n_kb > 0
    if has_lat:
        kl_ref, vtl_ref = refs[pos:pos + 2]
        pos += 2
    if mode == "diff":
        lam_ref, subln_ref = refs[pos:pos + 2]
        pos += 2
    elif has_sink:
        sink_ref = refs[pos]
        pos += 1
    o_ref, qs_scr, m_scr, acc_scr = refs[pos:pos + 4]
    if has_lat:
        s_scr, mx_scr = refs[pos + 4:]
    n_u = len(units)

    def pv(vt, p):
        ones = jnp.ones((SUM_ROWS, vt.shape[1]), BF16)
        return jnp.dot(jnp.concatenate([vt, ones], axis=0), p.astype(BF16),
                       preferred_element_type=F32)

    for u, (_, _, q_tiles) in enumerate(units):
        for t, (j, slot) in enumerate(q_tiles):
            tile = q_ref[:, j * LANES:(j + 1) * LANES]
            tile = jnp.where(_slot_mask(slot), tile, jnp.zeros_like(tile))
            qs_scr[u, :, t * tq:(t + 1) * tq] = tile.astype(F32).T.astype(BF16)

    for u, (half, v0, _) in enumerate(units):
        s = jnp.dot(kc_ref[:, half * LANES:(half + 1) * LANES], qs_scr[u],
                    preferred_element_type=F32)
        m = jnp.max(s, axis=0, keepdims=True)
        p = jnp.exp2(s - m)
        m_scr[u] = m
        acc_scr[u] = pv(vtc_ref[v0:v0 + dv, :], p)

    if has_lat:
        col_tiles = [slice(c0, c0 + MXU_DIM) for c0 in range(0, s_scr.shape[2], MXU_DIM)]

        def scores(kb, u, slot, cs):
            half = units[u][0]
            row0 = pl.multiple_of(kb * tk, tk)
            k = kl_ref[pl.ds(row0, tk), half * LANES:(half + 1) * LANES]
            s = jnp.dot(k, qs_scr[u, :, cs], preferred_element_type=F32)
            s_scr[slot, :, cs] = s
            mx_scr[slot, :, cs] = jnp.max(s, axis=0, keepdims=True)

        def probs(u, slot, cs):
            m_prev = m_scr[u, :, cs]
            m_new = jnp.maximum(m_prev, mx_scr[slot, :, cs])
            m_scr[u, :, cs] = m_new
            return (jnp.exp2(s_scr[slot, :, cs] - m_new).astype(BF16),
                    jnp.exp2(m_prev - m_new))

        def accumulate(kb, u, cs, p, alpha):
            v0 = units[u][1]
            acc_scr[u, :, cs] = alpha * acc_scr[u, :, cs] + pv(vtl_ref[kb, v0:v0 + dv, :], p)

        bpt = PIPE_SLOTS // n_u
        n_trips = n_kb // bpt

        def trip(it, last):
            pending = None
            for j in range(PIPE_SLOTS):
                ahead = j + LOOKAHEAD
                for cs in col_tiles:
                    if ahead < PIPE_SLOTS:
                        scores(it * bpt + ahead // n_u, ahead % n_u, ahead, cs)
                    elif not last:
                        a2 = ahead - PIPE_SLOTS
                        scores((it + 1) * bpt + a2 // n_u, a2 % n_u, a2, cs)
                    p, alpha = probs(j % n_u, j, cs)
                    if pending is not None:
                        accumulate(*pending)
                    pending = (it * bpt + j // n_u, j % n_u, cs, p, alpha)
            accumulate(*pending)

        for j in range(LOOKAHEAD):
            for cs in col_tiles:
                scores(j // n_u, j % n_u, j, cs)
        lax.fori_loop(0, n_trips - 1, lambda it, c: (trip(it, False), c)[1], 0)
        trip(n_trips - 1, True)

    if mode == "diff":
        lp = lam_ref[...]
        lam = (jnp.exp(jnp.sum(lp[0:1] * lp[1:2], axis=1, keepdims=True))
               - jnp.exp(jnp.sum(lp[2:3] * lp[3:4], axis=1, keepdims=True)) + lam_init)
        gain = subln_ref[...]
        for u in range(n_u):
            o12 = acc_scr[u, :dv, :] / acc_scr[u, dv:dv + 1, :]
            o = o12[:, :tq] - lam * o12[:, tq:]
            o = o * lax.rsqrt(jnp.mean(o * o, axis=0, keepdims=True) + EPS)
            o_ref[u * dv:(u + 1) * dv, :] = ((o * gain) * (1.0 - lam_init)).astype(BF16)
    else:
        for u in range(n_u):
            m, l, acc = m_scr[u], acc_scr[u, dv:dv + 1, :], acc_scr[u, :dv, :]
            if has_sink:
                sk = sink_ref[u] * LOG2E
                m_f = jnp.maximum(m, sk)
                w = jnp.exp2(m - m_f)
                l = l * w + jnp.exp2(sk - m_f)
                acc = acc * w
            o = acc / l
            for g in range(GQA_GROUP):
                row = (u * GQA_GROUP + g) * HEAD_DIM
                o_ref[row:row + HEAD_DIM, :] = o[:, g * tq:(g + 1) * tq].astype(BF16)


def _flash_attn(q, kc, vtc, kl, vtl, *, mode, n_batch, tq, sink_rows=None, lam_params=None,
                subln=None, lam_init=0.0):
    r, dq = q.shape
    t_q = r // n_batch
    c = kc.shape[0] // n_batch
    wk = kc.shape[1]
    n_grp = wk // MXU_DIM
    q_cols = dq // n_grp
    units, dv = (_GQA_UNITS, HEAD_DIM) if mode == "gqa" else (_DIFF_UNITS, 2 * HEAD_DIM)
    cols = len(units[0][2]) * tq
    has_lat = kl is not None
    nq = t_q // tq
    in_specs = [
        pl.BlockSpec((tq, q_cols), lambda b, g, i: (b * nq + i, g)),
        pl.BlockSpec((c, MXU_DIM), lambda b, g, i: (b, g)),
        pl.BlockSpec((None, MXU_DIM, c), lambda b, g, i: (b, g, 0)),
    ]
    args = [q, kc, vtc]
    n_kb, tk = 0, 0
    scratch = [
        pltpu.VMEM((len(units), LANES, cols), BF16),
        pltpu.VMEM((len(units), 1, cols), F32),
        pltpu.VMEM((len(units), dv + SUM_ROWS, cols), F32),
    ]
    if has_lat:
        t = kl.shape[0] // n_batch
        n_kb, tk = vtl.shape[1], vtl.shape[3]
        in_specs += [
            pl.BlockSpec((t, MXU_DIM), lambda b, g, i: (b, g)),
            pl.BlockSpec((None, n_kb, MXU_DIM, tk), lambda b, g, i: (b, 0, g, 0)),
        ]
        args += [kl, vtl]
        assert PIPE_SLOTS % len(units) == 0 and n_kb % (PIPE_SLOTS // len(units)) == 0
        scratch += [pltpu.VMEM((PIPE_SLOTS, tk, cols), F32),
                    pltpu.VMEM((PIPE_SLOTS, 1, cols), F32)]
    if mode == "diff":
        in_specs += [
            pl.BlockSpec((4, HEAD_DIM), lambda b, g, i: (0, 0)),
            pl.BlockSpec((dv, 1), lambda b, g, i: (0, 0)),
        ]
        args += [lam_params, subln.reshape(dv, 1)]
    elif sink_rows is not None:
        in_specs.append(pl.BlockSpec((N_KV_HEADS, 1, cols), lambda b, g, i: (0, 0, 0)))
        args.append(sink_rows)
    return pl.pallas_call(
        functools.partial(_flash_kernel, units=units, dv=dv, tq=tq, tk=tk, n_kb=n_kb, mode=mode,
                          has_sink=sink_rows is not None, lam_init=lam_init),
        grid=(n_batch, n_grp, nq),
        in_specs=in_specs,
        out_specs=pl.BlockSpec((None, q_cols, tq), lambda b, g, i: (b, g, i)),
        out_shape=jax.ShapeDtypeStruct((n_batch, dq, t_q), BF16),
        scratch_shapes=scratch,
        compiler_params=_cparams("parallel", "parallel", "arbitrary"),
        name=mode + "_attn",
    )(*args)


def _window_kernel(q_ref, kc_ref, vtc_ref, kp_ref, kcur_ref, kn_ref, vtp_ref, vtcur_ref,
                   vtn_ref, sink_ref, o_ref, *, tq):
    qi = pl.program_id(1)
    nq = pl.num_programs(1)
    c = kc_ref.shape[0]
    n_lat = tq + 2 * WINDOW
    cols = GQA_GROUP * tq
    rr = lax.broadcasted_iota(jnp.int32, (n_lat, 1), 0)
    cc = lax.broadcasted_iota(jnp.int32, (1, cols), 1) & (tq - 1)
    rel = rr - WINDOW - cc
    valid = (jnp.abs(rel) <= WINDOW)
    valid &= (rr >= WINDOW) | (qi > 0)
    valid &= (rr < tq + WINDOW) | (qi < nq - 1)
    for kvh in range(N_KV_HEADS):
        half = kvh // 2
        sl = slice(half * LANES, (half + 1) * LANES)
        qs = jnp.concatenate([_gqa_q_tile(q_ref, kvh, g) for g in range(GQA_GROUP)], axis=0)
        s_ctx = pl.dot(kc_ref[:, sl], qs, trans_b=True)
        k_lat = jnp.concatenate([kp_ref[:, sl], kcur_ref[:, sl], kn_ref[:, sl]], axis=0)
        s_lat = jnp.where(valid, pl.dot(k_lat, qs, trans_b=True), NEG_INF)
        sk = sink_ref[kvh] * LOG2E
        m = jnp.maximum(jnp.maximum(jnp.max(s_ctx, axis=0, keepdims=True),
                                    jnp.max(s_lat, axis=0, keepdims=True)), sk)
        p_ctx = jnp.exp2(s_ctx - m)
        p_lat = jnp.exp2(s_lat - m)
        l = (jnp.sum(p_ctx, axis=0, keepdims=True) + jnp.sum(p_lat, axis=0, keepdims=True)
             + jnp.exp2(sk - m))
        hs = slice(kvh * HEAD_DIM, (kvh + 1) * HEAD_DIM)
        vt_lat = jnp.concatenate([vtp_ref[hs, :], vtcur_ref[hs, :], vtn_ref[hs, :]], axis=1)
        acc = (jnp.dot(vtc_ref[hs, :], p_ctx.astype(BF16), preferred_element_type=F32)
               + jnp.dot(vt_lat, p_lat.astype(BF16), preferred_element_type=F32))
        o = acc / l
        for g in range(GQA_GROUP):
            row = (kvh * GQA_GROUP + g) * HEAD_DIM
            o_ref[row:row + HEAD_DIM, :] = o[:, g * tq:(g + 1) * tq].astype(BF16)


def _window_attn(q, kc, vtc, kl, vtl, sink_rows, *, n_batch, tq):
    r, dq = q.shape
    t = r // n_batch
    c = kc.shape[0] // n_batch
    wk = kc.shape[1]
    tkv = vtl.shape[3]
    nq = t // tq
    rb = tq // WINDOW
    nwb = t // WINDOW
    wpb, qpb = tkv // WINDOW, tkv // tq
    prev = lambda b, i: jnp.maximum(i * rb - 1, 0)
    nxt = lambda b, i: jnp.minimum((i + 1) * rb, nwb - 1)
    in_specs = [
        pl.BlockSpec((tq, dq), lambda b, i: (b * nq + i, 0)),
        pl.BlockSpec((c, wk), lambda b, i: (b, 0)),
        pl.BlockSpec((None, wk, c), lambda b, i: (b, 0, 0)),
        pl.BlockSpec((WINDOW, wk), lambda b, i: (b * nwb + prev(b, i), 0)),
        pl.BlockSpec((tq, wk), lambda b, i: (b * nq + i, 0)),
        pl.BlockSpec((WINDOW, wk), lambda b, i: (b * nwb + nxt(b, i), 0)),
        pl.BlockSpec((None, None, wk, WINDOW),
                     lambda b, i: (b, prev(b, i) // wpb, 0, prev(b, i) % wpb)),
        pl.BlockSpec((None, None, wk, tq), lambda b, i: (b, i // qpb, 0, i % qpb)),
        pl.BlockSpec((None, None, wk, WINDOW),
                     lambda b, i: (b, nxt(b, i) // wpb, 0, nxt(b, i) % wpb)),
        pl.BlockSpec((N_KV_HEADS, 1, GQA_GROUP * tq), lambda b, i: (0, 0, 0)),
    ]
    return pl.pallas_call(
        functools.partial(_window_kernel, tq=tq),
        grid=(n_batch, nq),
        in_specs=in_specs,
        out_specs=pl.BlockSpec((None, dq, tq), lambda b, i: (b, 0, i)),
        out_shape=jax.ShapeDtypeStruct((n_batch, dq, t), BF16),
        compiler_params=_cparams("parallel", "parallel"),
        name="window_attn",
    )(q, kc, vtc, kl, kl, kl, vtl, vtl, vtl, sink_rows)


def _rope_tables(n_tokens):
    rows = n_tokens // GRID_W
    row = jnp.repeat(jnp.arange(rows), GRID_W)
    col = jnp.tile(jnp.arange(GRID_W), rows)
    n_freq = HEAD_DIM // 4
    inv_freq = ROPE_BASE ** (-jnp.arange(n_freq, dtype=F32) / n_freq)
    ang = jnp.stack([row, col], axis=-1).astype(F32)[:, :, None] * inv_freq
    cos, sin = jnp.cos(ang), jnp.sin(ang)
    cos_h = jnp.concatenate([cos, cos], axis=-1).reshape(n_tokens, HEAD_DIM)
    sin_h = jnp.concatenate([-sin, sin], axis=-1).reshape(n_tokens, HEAD_DIM)
    return jnp.tile(cos_h, (1, 2)), jnp.tile(sin_h, (1, 2))


def _gqa_weights(w_qkv):
    d = w_qkv.shape[0]
    n_q = N_KV_HEADS * GQA_GROUP * HEAD_DIM
    n_kv = N_KV_HEADS * HEAD_DIM
    wq = w_qkv[:, :n_q].reshape(d, N_KV_HEADS // 2, 2, GQA_GROUP, HEAD_DIM)
    wq = wq.transpose(0, 1, 3, 2, 4).reshape(d, n_q)
    wk = w_qkv[:, n_q:n_q + n_kv]
    wvt = w_qkv[:, n_q + n_kv:].T
    return wq.astype(BF16), wk.astype(BF16), wvt.astype(BF16)


def _diff_weights(w_qkv):
    n = w_qkv.shape[1] // 3
    return (w_qkv[:, :n].astype(BF16), w_qkv[:, n:2 * n].astype(BF16),
            w_qkv[:, 2 * n:].T.astype(BF16))


def _lane_gain(g):
    return jnp.tile(g.reshape(1, HEAD_DIM), (1, LANES // HEAD_DIM))


def _sink_rows(sink, tq):
    return jnp.repeat(sink.reshape(N_KV_HEADS, 1, GQA_GROUP, 1), tq, axis=3).reshape(
        N_KV_HEADS, 1, GQA_GROUP * tq)


def kernel(x, c, ctx, c_ctx, norm_g, w_ada, b_ada, w_ffn_in, w_ffn_out, w_o, w_qkv_a, qk_norm_a,
           w_qkv_b, qk_norm_b, sink_b, w_qkv_c, qk_norm_c, diff_lambda, diff_subln):
    n_b, t, d = x.shape
    n_c = ctx.shape[1]
    depth = w_ada.shape[0]
    d_ff = w_ffn_out.shape[2]
    assert n_b + 1 <= 8 and t % 1024 == 0 and n_c % LANES == 0 and n_c <= 512

    tm = 512
    tq, tq_diff = 512, 1024
    tq_win = 256
    tq_c = n_c

    c8 = jnp.zeros((8, d), F32).at[:n_b].set(c).at[n_b].set(c_ctx)
    mod5 = _ada_all(c8, w_ada, b_ada).reshape(depth, 8, 3, 3, d)

    rope_tabs = _rope_tables(t)
    eye = jnp.arange(MXU_DIM) // HEAD_DIM
    bd = (eye[:, None] == eye[None, :]).astype(BF16)
    w_in = w_ffn_in.astype(BF16)
    w_out = w_ffn_out.astype(BF16)
    wo = w_o.astype(BF16)
    norm_g4 = norm_g.reshape(depth, 3, 1, d)

    xs = x.reshape(n_b * t, d)
    cs = ctx.reshape(n_b * n_c, d)
    lat = dict(tm=tm, rows_per_batch=t, ctx_row=None)
    cx = dict(tm=n_c, rows_per_batch=n_c, ctx_row=n_b)
    tkv = 512

    for i in range(depth):
        last = i == depth - 1
        kind, j = i % 3, i // 3
        xs = _ffn(xs, mod5, norm_g4, w_in, w_out, i, 0, **lat)
        cs = _ffn(cs, mod5, norm_g4, w_in, w_out, i, 0, **cx)
        if kind == 2:
            wq, wk, wvt = _diff_weights(w_qkv_c[j])
            qk_g = qk_norm_c[j]
        else:
            wq, wk, wvt = _gqa_weights((w_qkv_a, w_qkv_b)[kind][j])
            qk_g = (qk_norm_a, qk_norm_b)[kind][j]
        gq, gk = _lane_gain(qk_g[0]), _lane_gain(qk_g[1])
        q, kl, vtl = _proj(xs, mod5, norm_g4, i, wq, wk, wvt, gq, gk, bd, rope_tabs, tkv=tkv,
                           **lat)
        qc, kc, vtc = _proj(cs, mod5, norm_g4, i, wq, wk, wvt, gq, gk, bd, None, tkv=n_c, **cx)
        vtc = vtc.reshape(n_b, vtc.shape[2], n_c)
        if kind == 0:
            ot = _flash_attn(q, kc, vtc, kl, vtl, mode="gqa", n_batch=n_b, tq=tq)
            if not last:
                otc = _flash_attn(qc, kc, vtc, None, None, mode="gqa", n_batch=n_b, tq=tq_c)
        elif kind == 1:
            ot = _window_attn(q, kc, vtc, kl, vtl, _sink_rows(sink_b[j], tq_win), n_batch=n_b,
                              tq=tq_win)
            if not last:
                otc = _flash_attn(qc, kc, vtc, None, None, mode="gqa", n_batch=n_b, tq=tq_c,
                                  sink_rows=_sink_rows(sink_b[j], tq_c))
        else:
            extra = dict(lam_params=diff_lambda[j], subln=diff_subln[j],
                         lam_init=0.8 - 0.6 * math.exp(-0.3 * i))
            ot = _flash_attn(q, kc, vtc, kl, vtl, mode="diff", n_batch=n_b, tq=tq_diff, **extra)
            if not last:
                otc = _flash_attn(qc, kc, vtc, None, None, mode="diff", n_batch=n_b, tq=tq_c,
                                  **extra)
        xs = _ffn(xs, mod5, norm_g4, w_in, w_out, i, 1, pre=(ot, wo), **lat)
        if not last:
            cs = _ffn(cs, mod5, norm_g4, w_in, w_out, i, 1, pre=(otc, wo), **cx)
    return xs.reshape(n_b, t, d)
```

```python
import functools
import math

import jax
import jax.numpy as jnp
from jax import lax
from jax.experimental import pallas as pl
from jax.experimental.pallas import tpu as pltpu

F32 = jnp.float32
BF16 = jnp.bfloat16

HEAD_DIM = 64
N_KV_HEADS = 4
GQA_GROUP = 4
GRID_W = 64
WINDOW = 128
ROPE_BASE = 10000.0
EPS = 1e-6
NEG_INF = -1e30
LOG2E = math.log2(math.e)
Q_SCALE = HEAD_DIM ** -0.5 * LOG2E
FFN_RESIDUAL = 0.5
LANES = 128
MXU_DIM = 256
SUM_ROWS = 16
PIPE_SLOTS = 4
LOOKAHEAD = 1
V7X_VMEM_BYTES = 64 * 1024 * 1024
SUBLANES = 8


def _tile_bytes(shape, dtype):
    item = jnp.dtype(dtype).itemsize
    dims = [1 if s is None else s for s in shape]
    dims[-1] = pl.cdiv(dims[-1], LANES) * LANES
    if len(dims) > 1:
        rows = SUBLANES * 4 // item
        dims[-2] = pl.cdiv(dims[-2], rows) * rows
    return math.prod(dims) * item


def _pallas(body, *, name, grid, sem, in_specs, args, out_specs, out_shape, scratch=(), temps=0):
    outs = out_shape if isinstance(out_shape, (list, tuple)) else [out_shape]
    o_specs = out_specs if isinstance(out_specs, (list, tuple)) else [out_specs]
    need = temps + sum(_tile_bytes(s.shape, s.dtype) for s in scratch)
    for spec, a in list(zip(in_specs, args)) + list(zip(o_specs, outs)):
        n_buf = spec.pipeline_mode.buffer_count if spec.pipeline_mode is not None else 2
        need += n_buf * _tile_bytes(spec.block_shape, a.dtype)
    return pl.pallas_call(
        body, grid=grid, in_specs=in_specs, out_specs=out_specs, out_shape=out_shape,
        scratch_shapes=list(scratch), name=name,
        compiler_params=pltpu.CompilerParams(dimension_semantics=sem,
                                             vmem_limit_bytes=min(need, V7X_VMEM_BYTES)),
    )(*args)


def _silu(a):
    return a * jax.nn.sigmoid(a)


def _norm_mod(x, g, shift, scale):
    y = x * lax.rsqrt(jnp.mean(x * x, axis=-1, keepdims=True) + EPS)
    return (y * g) * (1.0 + scale) + shift


def _ada_kernel(c_ref, w_ref, b_ref, o_ref):
    s = _silu(c_ref[...])
    o_ref[...] = jnp.dot(s, w_ref[...], preferred_element_type=F32) + b_ref[...]


def _ada_all(c8, w_ada, b_ada):
    depth, d, nd = w_ada.shape
    tn = nd // 8
    return _pallas(
        _ada_kernel, name="adaln", grid=(depth, nd // tn), sem=("parallel", "parallel"),
        in_specs=[
            pl.BlockSpec((8, d), lambda i, j: (0, 0)),
            pl.BlockSpec((None, d, tn), lambda i, j: (i, 0, j)),
            pl.BlockSpec((None, 1, tn), lambda i, j: (i, 0, j)),
        ],
        args=[c8, w_ada, b_ada.reshape(depth, 1, nd)],
        out_specs=pl.BlockSpec((None, 8, tn), lambda i, j: (i, 0, j)),
        out_shape=jax.ShapeDtypeStruct((depth, 8, nd), F32),
        temps=_tile_bytes((8, d), F32) + 2 * _tile_bytes((8, tn), F32))


def _ffn_kernel(*refs, pre):
    if pre:
        x_ref, ot_ref, wo_ref, modp_ref, mod_ref, g_ref, wa_ref, wu_ref, wout_ref, o_ref = refs
    else:
        x_ref, mod_ref, g_ref, wa_ref, wu_ref, wout_ref, o_ref = refs
    x = x_ref[...]
    if pre:
        x = x + modp_ref[2:3, :] * pl.dot(ot_ref[...], wo_ref[...], trans_a=True)
    h = _norm_mod(x, g_ref[...], mod_ref[0:1, :], mod_ref[1:2, :]).astype(BF16)
    a = jnp.dot(h, wa_ref[...], preferred_element_type=F32)
    u = jnp.dot(h, wu_ref[...], preferred_element_type=F32)
    act = (_silu(a) * u).astype(BF16)
    y = jnp.dot(act, wout_ref[...], preferred_element_type=F32)
    o_ref[...] = x + (FFN_RESIDUAL * mod_ref[2:3, :]) * y


def _ffn(x2, mod5, norm_g4, w_in, w_out, layer, half, *, tm, rows_per_batch, ctx_row, pre=None):
    r, d = x2.shape
    dff = w_out.shape[2]
    tpb = rows_per_batch // tm
    k = 2 * half
    resident = dict(pipeline_mode=pl.Buffered(1))

    def bidx(i):
        return ctx_row if ctx_row is not None else i // tpb

    def mod_spec(kk):
        return pl.BlockSpec((None, None, None, 3, d), lambda i: (layer, bidx(i), kk, 0, 0))

    in_specs = [pl.BlockSpec((tm, d), lambda i: (i, 0))]
    args = [x2]
    if pre is not None:
        ot, wo = pre
        in_specs += [
            pl.BlockSpec((None, d, tm), lambda i: (i // tpb, 0, i % tpb)),
            pl.BlockSpec((None, d, d), lambda i: (layer, 0, 0), **resident),
            mod_spec(1),
        ]
        args += [ot, wo, mod5]
    in_specs += [
        mod_spec(k),
        pl.BlockSpec((None, None, 1, d), lambda i: (layer, k, 0, 0)),
        pl.BlockSpec((None, None, d, dff), lambda i: (layer, half, 0, 0), **resident),
        pl.BlockSpec((None, None, d, dff), lambda i: (layer, half, 0, 1), **resident),
        pl.BlockSpec((None, None, dff, d), lambda i: (layer, half, 0, 0), **resident),
    ]
    args += [mod5, norm_g4, w_in, w_in, w_out]
    temps = tm * dff * (4 + 4 + 2) + tm * d * (4 + 4 + 2)
    return _pallas(
        functools.partial(_ffn_kernel, pre=pre is not None),
        name="ffn_pre" if pre is not None else "ffn", grid=(r // tm,), sem=("parallel",),
        in_specs=in_specs, args=args,
        out_specs=pl.BlockSpec((tm, d), lambda i: (i, 0)),
        out_shape=jax.ShapeDtypeStruct((r, d), F32), temps=temps)


def _headnorm_rope(z, gain, cos, sin, bd, rope):
    tm, w = z.shape
    lane = lax.broadcasted_iota(jnp.int32, (1, LANES), 1)
    first_half = (lane & 31) < 16
    outs = []
    for j in range(w // MXU_DIM):
        zj = z[:, j * MXU_DIM:(j + 1) * MXU_DIM]
        sq = zj * zj
        hi = sq.astype(BF16)
        lo = (sq - hi.astype(F32)).astype(BF16)
        ss = (jnp.dot(hi, bd, preferred_element_type=F32)
              + jnp.dot(lo, bd, preferred_element_type=F32))
        zn = zj * lax.rsqrt(ss * (1.0 / HEAD_DIM) + EPS)
        for half in range(MXU_DIM // LANES):
            t = zn[:, half * LANES:(half + 1) * LANES] * gain
            if rope:
                partner = jnp.where(first_half, pltpu.roll(t, LANES - 16, 1),
                                    pltpu.roll(t, 16, 1))
                t = t * cos + partner * sin
            outs.append(t)
    return jnp.concatenate(outs, axis=1)


def _proj_kernel(*refs, rope):
    if rope:
        (x_ref, mod_ref, g_ref, wq_ref, wk_ref, wvt_ref, gq_ref, gk_ref, bd_ref, cos_ref,
         sin_ref, q_ref, k_ref, vt_ref) = refs
        cos, sin = cos_ref[...], sin_ref[...]
    else:
        (x_ref, mod_ref, g_ref, wq_ref, wk_ref, wvt_ref, gq_ref, gk_ref, bd_ref,
         q_ref, k_ref, vt_ref) = refs
        cos = sin = None
    h = _norm_mod(x_ref[...], g_ref[...], mod_ref[0:1, :], mod_ref[1:2, :]).astype(BF16)
    bd = bd_ref[...]
    q = jnp.dot(h, wq_ref[...], preferred_element_type=F32)
    q_ref[...] = _headnorm_rope(q, gq_ref[...] * Q_SCALE, cos, sin, bd, rope).astype(BF16)
    k = jnp.dot(h, wk_ref[...], preferred_element_type=F32)
    k_ref[...] = _headnorm_rope(k, gk_ref[...], cos, sin, bd, rope).astype(BF16)
    vt = pl.dot(wvt_ref[...], h, trans_b=True).astype(BF16)
    tkv = vt_ref.shape[2]
    for c in range(vt_ref.shape[0]):
        vt_ref[c] = vt[:, c * tkv:(c + 1) * tkv]


def _proj(x2, mod5, norm_g4, layer, wq, wk, wvt, gq, gk, bd, rope_tabs, *, tm, rows_per_batch,
          ctx_row, tkv):
    r, d = x2.shape
    wq_n, wk_n, wv_n = wq.shape[1], wk.shape[1], wvt.shape[0]
    tpb = rows_per_batch // tm
    kpt = tm // tkv
    nb = r // rows_per_batch
    rope = rope_tabs is not None

    def bidx(i):
        return ctx_row if ctx_row is not None else i // tpb

    const = lambda i: (0, 0)
    in_specs = [
        pl.BlockSpec((tm, d), lambda i: (i, 0)),
        pl.BlockSpec((None, None, None, 3, d), lambda i: (layer, bidx(i), 1, 0, 0)),
        pl.BlockSpec((None, None, 1, d), lambda i: (layer, 1, 0, 0)),
        pl.BlockSpec((d, wq_n), const),
        pl.BlockSpec((d, wk_n), const),
        pl.BlockSpec((wv_n, d), const),
        pl.BlockSpec((1, LANES), const),
        pl.BlockSpec((1, LANES), const),
        pl.BlockSpec((MXU_DIM, MXU_DIM), const),
    ]
    args = [x2, mod5, norm_g4, wq, wk, wvt, gq, gk, bd]
    if rope:
        in_specs += [pl.BlockSpec((tm, LANES), lambda i: (i % tpb, 0))] * 2
        args += list(rope_tabs)
    temps = tm * d * (4 + 2) + 2 * tm * (wq_n + wk_n + wv_n) * 4
    return _pallas(
        functools.partial(_proj_kernel, rope=rope), name="qkv_proj", grid=(r // tm,),
        sem=("parallel",), in_specs=in_specs, args=args,
        out_specs=[
            pl.BlockSpec((tm, wq_n), lambda i: (i, 0)),
            pl.BlockSpec((tm, wk_n), lambda i: (i, 0)),
            pl.BlockSpec((None, kpt, wv_n, tkv), lambda i: (i // tpb, i % tpb, 0, 0)),
        ],
        out_shape=[
            jax.ShapeDtypeStruct((r, wq_n), BF16),
            jax.ShapeDtypeStruct((r, wk_n), BF16),
            jax.ShapeDtypeStruct((nb, tpb * kpt, wv_n, tkv), BF16),
        ],
        temps=temps)


def _slot_mask(slot):
    lane = lax.broadcasted_iota(jnp.int32, (1, LANES), 1)
    return (lane >= slot * HEAD_DIM) & (lane < (slot + 1) * HEAD_DIM)


def _gqa_q_tile(q_ref, kvh, g):
    j = g + GQA_GROUP * (kvh // 2)
    tile = q_ref[:, j * LANES:(j + 1) * LANES]
    return jnp.where(_slot_mask(kvh % 2), tile, jnp.zeros_like(tile))


_GQA_UNITS = tuple(
    (kvh // 2, kvh * HEAD_DIM,
     tuple((g + GQA_GROUP * (kvh // 2), kvh % 2) for g in range(GQA_GROUP)))
    for kvh in range(N_KV_HEADS))
_DIFF_UNITS = tuple((h, h * 2 * HEAD_DIM, ((h, 0), (h, 1))) for h in range(2))


def _flash_kernel(*refs, units, dv, tq, tk, n_kb, mode, has_sink, lam_init):
    refs = list(refs)
    q_ref, kc_ref, vtc_ref = refs[:3]
    pos = 3
    has_lat = n_kb > 0
    if has_lat:
        kl_ref, vtl_ref = refs[pos:pos + 2]
        pos += 2
    if mode == "diff":
        lam_ref, subln_ref = refs[pos:pos + 2]
        pos += 2
    elif has_sink:
        sink_ref = refs[pos]
        pos += 1
    o_ref, qs_scr, m_scr, acc_scr = refs[pos:pos + 4]
    if has_lat:
        s_scr, mx_scr = refs[pos + 4:]
    n_u = len(units)

    def pv(vt, p):
        ones = jnp.ones((SUM_ROWS, vt.shape[1]), BF16)
        return jnp.dot(jnp.concatenate([vt, ones], axis=0), p.astype(BF16),
                       preferred_element_type=F32)

    for u, (_, _, q_tiles) in enumerate(units):
        for t, (j, slot) in enumerate(q_tiles):
            tile = q_ref[:, j * LANES:(j + 1) * LANES]
            tile = jnp.where(_slot_mask(slot), tile, jnp.zeros_like(tile))
            qs_scr[u, :, t * tq:(t + 1) * tq] = tile.astype(F32).T.astype(BF16)

    for u, (half, v0, _) in enumerate(units):
        s = jnp.dot(kc_ref[:, half * LANES:(half + 1) * LANES], qs_scr[u],
                    preferred_element_type=F32)
        m = jnp.max(s, axis=0, keepdims=True)
        p = jnp.exp2(s - m)
        m_scr[u] = m
        acc_scr[u] = pv(vtc_ref[v0:v0 + dv, :], p)

    if has_lat:
        col_tiles = [slice(c0, c0 + MXU_DIM) for c0 in range(0, s_scr.shape[2], MXU_DIM)]

        key_halves = [slice(r0, r0 + MXU_DIM) for r0 in range(0, tk, MXU_DIM)]

        def scores(kb, u, slot, cs, rs):
            half = units[u][0]
            row0 = pl.multiple_of(kb * tk, tk) + rs.start
            k = kl_ref[pl.ds(row0, MXU_DIM), half * LANES:(half + 1) * LANES]
            s = jnp.dot(k, qs_scr[u, :, cs], preferred_element_type=F32)
            s_scr[slot, rs, cs] = s
            return jnp.max(s, axis=0, keepdims=True)

        def probs(u, slot, cs):
            m_prev = m_scr[u, :, cs]
            m_new = jnp.maximum(m_prev, mx_scr[slot, :, cs])
            m_scr[u, :, cs] = m_new
            return (jnp.exp2(s_scr[slot, :, cs] - m_new).astype(BF16),
                    jnp.exp2(m_prev - m_new))

        def pv_part(kb, u, p, rs):
            v0 = units[u][1]
            return pv(vtl_ref[kb, v0:v0 + dv, rs], p[rs, :])

        bpt = PIPE_SLOTS // n_u
        n_trips = n_kb // bpt

        def trip(it, last):
            pending = None
            for j in range(PIPE_SLOTS):
                ahead = j + LOOKAHEAD
                if ahead < PIPE_SLOTS:
                    sc = (it * bpt + ahead // n_u, ahead % n_u, ahead)
                elif not last:
                    a2 = ahead - PIPE_SLOTS
                    sc = ((it + 1) * bpt + a2 // n_u, a2 % n_u, a2)
                else:
                    sc = None
                for cs in col_tiles:
                    fresh = (it * bpt + j // n_u, j % n_u, cs, *probs(j % n_u, j, cs))
                    mx, parts = None, []
                    for rs in key_halves:
                        if sc is not None:
                            part = scores(*sc, cs, rs)
                            mx = part if mx is None else jnp.maximum(mx, part)
                        if pending is not None:
                            parts.append(pv_part(pending[0], pending[1], pending[3], rs))
                    if sc is not None:
                        mx_scr[sc[2], :, cs] = mx
                    if pending is not None:
                        _, u_p, cs_p, _, alpha_p = pending
                        acc_scr[u_p, :, cs_p] = alpha_p * acc_scr[u_p, :, cs_p] + sum(parts)
                    pending = fresh
            kb_p, u_p, cs_p, p, alpha_p = pending
            acc_scr[u_p, :, cs_p] = alpha_p * acc_scr[u_p, :, cs_p] + sum(
                pv_part(kb_p, u_p, p, rs) for rs in key_halves)

        for j in range(LOOKAHEAD):
            for cs in col_tiles:
                mx_scr[j, :, cs] = functools.reduce(
                    jnp.maximum, [scores(j // n_u, j % n_u, j, cs, rs) for rs in key_halves])
        lax.fori_loop(0, n_trips - 1, lambda it, c: (trip(it, False), c)[1], 0)
        trip(n_trips - 1, True)

    if mode == "diff":
        lp = lam_ref[...]
        lam = (jnp.exp(jnp.sum(lp[0:1] * lp[1:2], axis=1, keepdims=True))
               - jnp.exp(jnp.sum(lp[2:3] * lp[3:4], axis=1, keepdims=True)) + lam_init)
        gain = subln_ref[...]
        for u in range(n_u):
            o12 = acc_scr[u, :dv, :] / acc_scr[u, dv:dv + 1, :]
            o = o12[:, :tq] - lam * o12[:, tq:]
            o = o * lax.rsqrt(jnp.mean(o * o, axis=0, keepdims=True) + EPS)
            o_ref[u * dv:(u + 1) * dv, :] = ((o * gain) * (1.0 - lam_init)).astype(BF16)
    else:
        for u in range(n_u):
            m, l, acc = m_scr[u], acc_scr[u, dv:dv + 1, :], acc_scr[u, :dv, :]
            if has_sink:
                sk = sink_ref[u] * LOG2E
                m_f = jnp.maximum(m, sk)
                w = jnp.exp2(m - m_f)
                l = l * w + jnp.exp2(sk - m_f)
                acc = acc * w
            o = acc / l
            for g in range(GQA_GROUP):
                row = (u * GQA_GROUP + g) * HEAD_DIM
                o_ref[row:row + HEAD_DIM, :] = o[:, g * tq:(g + 1) * tq].astype(BF16)


def _flash_attn(q, kc, vtc, kl, vtl, *, mode, n_batch, tq, sink_rows=None, lam_params=None,
                subln=None, lam_init=0.0):
    r, dq = q.shape
    t_q = r // n_batch
    c = kc.shape[0] // n_batch
    wk = kc.shape[1]
    n_grp = wk // MXU_DIM
    q_cols = dq // n_grp
    units, dv = (_GQA_UNITS, HEAD_DIM) if mode == "gqa" else (_DIFF_UNITS, 2 * HEAD_DIM)
    cols = len(units[0][2]) * tq
    has_lat = kl is not None
    nq = t_q // tq
    in_specs = [
        pl.BlockSpec((tq, q_cols), lambda b, g, i: (b * nq + i, g)),
        pl.BlockSpec((c, MXU_DIM), lambda b, g, i: (b, g)),
        pl.BlockSpec((None, MXU_DIM, c), lambda b, g, i: (b, g, 0)),
    ]
    args = [q, kc, vtc]
    n_kb, tk = 0, 0
    scratch = [
        pltpu.VMEM((len(units), LANES, cols), BF16),
        pltpu.VMEM((len(units), 1, cols), F32),
        pltpu.VMEM((len(units), dv + SUM_ROWS, cols), F32),
    ]
    if has_lat:
        t = kl.shape[0] // n_batch
        n_kb, tk = vtl.shape[1], vtl.shape[3]
        in_specs += [
            pl.BlockSpec((t, MXU_DIM), lambda b, g, i: (b, g)),
            pl.BlockSpec((None, n_kb, MXU_DIM, tk), lambda b, g, i: (b, 0, g, 0)),
        ]
        args += [kl, vtl]
        assert PIPE_SLOTS % len(units) == 0 and n_kb % (PIPE_SLOTS // len(units)) == 0
        scratch += [pltpu.VMEM((PIPE_SLOTS, tk, cols), F32),
                    pltpu.VMEM((PIPE_SLOTS, 1, cols), F32)]
    if mode == "diff":
        in_specs += [
            pl.BlockSpec((4, HEAD_DIM), lambda b, g, i: (0, 0)),
            pl.BlockSpec((dv, 1), lambda b, g, i: (0, 0)),
        ]
        args += [lam_params, subln.reshape(dv, 1)]
    elif sink_rows is not None:
        in_specs.append(pl.BlockSpec((N_KV_HEADS, 1, cols), lambda b, g, i: (0, 0, 0)))
        args.append(sink_rows)
    temps = c * cols * (4 + 4 + 2) + 4 * max(tk, c) * MXU_DIM * (4 + 4 + 2)
    return _pallas(
        functools.partial(_flash_kernel, units=units, dv=dv, tq=tq, tk=tk, n_kb=n_kb, mode=mode,
                          has_sink=sink_rows is not None, lam_init=lam_init),
        name=mode + "_attn", grid=(n_batch, n_grp, nq),
        sem=("parallel", "parallel", "arbitrary"), in_specs=in_specs, args=args,
        out_specs=pl.BlockSpec((None, q_cols, tq), lambda b, g, i: (b, g, i)),
        out_shape=jax.ShapeDtypeStruct((n_batch, dq, t_q), BF16),
        scratch=scratch, temps=temps)


def _window_kernel(q_ref, kc_ref, vtc_ref, kp_ref, kcur_ref, kn_ref, vtp_ref, vtcur_ref,
                   vtn_ref, sink_ref, o_ref, *, tq):
    qi = pl.program_id(1)
    nq = pl.num_programs(1)
    n_lat = tq + 2 * WINDOW
    cols = GQA_GROUP * tq
    rr = lax.broadcasted_iota(jnp.int32, (n_lat, 1), 0)
    cc = lax.broadcasted_iota(jnp.int32, (1, cols), 1) & (tq - 1)
    rel = rr - WINDOW - cc
    valid = (jnp.abs(rel) <= WINDOW)
    valid &= (rr >= WINDOW) | (qi > 0)
    valid &= (rr < tq + WINDOW) | (qi < nq - 1)
    for kvh in range(N_KV_HEADS):
        half = kvh // 2
        sl = slice(half * LANES, (half + 1) * LANES)
        qs = jnp.concatenate([_gqa_q_tile(q_ref, kvh, g) for g in range(GQA_GROUP)], axis=0)
        s_ctx = pl.dot(kc_ref[:, sl], qs, trans_b=True)
        k_lat = jnp.concatenate([kp_ref[:, sl], kcur_ref[:, sl], kn_ref[:, sl]], axis=0)
        s_lat = jnp.where(valid, pl.dot(k_lat, qs, trans_b=True), NEG_INF)
        sk = sink_ref[kvh] * LOG2E
        m = jnp.maximum(jnp.maximum(jnp.max(s_ctx, axis=0, keepdims=True),
                                    jnp.max(s_lat, axis=0, keepdims=True)), sk)
        p_ctx = jnp.exp2(s_ctx - m)
        p_lat = jnp.exp2(s_lat - m)
        l = (jnp.sum(p_ctx, axis=0, keepdims=True) + jnp.sum(p_lat, axis=0, keepdims=True)
             + jnp.exp2(sk - m))
        hs = slice(kvh * HEAD_DIM, (kvh + 1) * HEAD_DIM)
        vt_lat = jnp.concatenate([vtp_ref[hs, :], vtcur_ref[hs, :], vtn_ref[hs, :]], axis=1)
        acc = (jnp.dot(vtc_ref[hs, :], p_ctx.astype(BF16), preferred_element_type=F32)
               + jnp.dot(vt_lat, p_lat.astype(BF16), preferred_element_type=F32))
        o = acc / l
        for g in range(GQA_GROUP):
            row = (kvh * GQA_GROUP + g) * HEAD_DIM
            o_ref[row:row + HEAD_DIM, :] = o[:, g * tq:(g + 1) * tq].astype(BF16)


def _window_attn(q, kc, vtc, kl, vtl, sink_rows, *, n_batch, tq):
    r, dq = q.shape
    t = r // n_batch
    c = kc.shape[0] // n_batch
    wk = kc.shape[1]
    tkv = vtl.shape[3]
    nq = t // tq
    rb = tq // WINDOW
    nwb = t // WINDOW
    wpb, qpb = tkv // WINDOW, tkv // tq
    prev = lambda b, i: jnp.maximum(i * rb - 1, 0)
    nxt = lambda b, i: jnp.minimum((i + 1) * rb, nwb - 1)
    in_specs = [
        pl.BlockSpec((tq, dq), lambda b, i: (b * nq + i, 0)),
        pl.BlockSpec((c, wk), lambda b, i: (b, 0)),
        pl.BlockSpec((None, wk, c), lambda b, i: (b, 0, 0)),
        pl.BlockSpec((WINDOW, wk), lambda b, i: (b * nwb + prev(b, i), 0)),
        pl.BlockSpec((tq, wk), lambda b, i: (b * nq + i, 0)),
        pl.BlockSpec((WINDOW, wk), lambda b, i: (b * nwb + nxt(b, i), 0)),
        pl.BlockSpec((None, None, wk, WINDOW),
                     lambda b, i: (b, prev(b, i) // wpb, 0, prev(b, i) % wpb)),
        pl.BlockSpec((None, None, wk, tq), lambda b, i: (b, i // qpb, 0, i % qpb)),
        pl.BlockSpec((None, None, wk, WINDOW),
                     lambda b, i: (b, nxt(b, i) // wpb, 0, nxt(b, i) % wpb)),
        pl.BlockSpec((N_KV_HEADS, 1, GQA_GROUP * tq), lambda b, i: (0, 0, 0)),
    ]
    n_keys = c + tq + 2 * WINDOW
    temps = n_keys * GQA_GROUP * tq * (4 + 4 + 2) + GQA_GROUP * tq * LANES * 2
    return _pallas(
        functools.partial(_window_kernel, tq=tq), name="window_attn", grid=(n_batch, nq),
        sem=("parallel", "parallel"), in_specs=in_specs,
        args=[q, kc, vtc, kl, kl, kl, vtl, vtl, vtl, sink_rows],
        out_specs=pl.BlockSpec((None, dq, tq), lambda b, i: (b, 0, i)),
        out_shape=jax.ShapeDtypeStruct((n_batch, dq, t), BF16), temps=temps)


def _rope_tables(n_tokens):
    rows = n_tokens // GRID_W
    row = jnp.repeat(jnp.arange(rows), GRID_W)
    col = jnp.tile(jnp.arange(GRID_W), rows)
    n_freq = HEAD_DIM // 4
    inv_freq = ROPE_BASE ** (-jnp.arange(n_freq, dtype=F32) / n_freq)
    ang = jnp.stack([row, col], axis=-1).astype(F32)[:, :, None] * inv_freq
    cos, sin = jnp.cos(ang), jnp.sin(ang)
    cos_h = jnp.concatenate([cos, cos], axis=-1).reshape(n_tokens, HEAD_DIM)
    sin_h = jnp.concatenate([-sin, sin], axis=-1).reshape(n_tokens, HEAD_DIM)
    return jnp.tile(cos_h, (1, 2)), jnp.tile(sin_h, (1, 2))


def _gqa_weights(w_qkv):
    d = w_qkv.shape[0]
    n_q = N_KV_HEADS * GQA_GROUP * HEAD_DIM
    n_kv = N_KV_HEADS * HEAD_DIM
    wq = w_qkv[:, :n_q].reshape(d, N_KV_HEADS // 2, 2, GQA_GROUP, HEAD_DIM)
    wq = wq.transpose(0, 1, 3, 2, 4).reshape(d, n_q)
    wk = w_qkv[:, n_q:n_q + n_kv]
    wvt = w_qkv[:, n_q + n_kv:].T
    return wq.astype(BF16), wk.astype(BF16), wvt.astype(BF16)


def _diff_weights(w_qkv):
    n = w_qkv.shape[1] // 3
    return (w_qkv[:, :n].astype(BF16), w_qkv[:, n:2 * n].astype(BF16),
            w_qkv[:, 2 * n:].T.astype(BF16))


def _lane_gain(g):
    return jnp.tile(g.reshape(1, HEAD_DIM), (1, LANES // HEAD_DIM))


def _sink_rows(sink, tq):
    return jnp.repeat(sink.reshape(N_KV_HEADS, 1, GQA_GROUP, 1), tq, axis=3).reshape(
        N_KV_HEADS, 1, GQA_GROUP * tq)


def kernel(x, c, ctx, c_ctx, norm_g, w_ada, b_ada, w_ffn_in, w_ffn_out, w_o, w_qkv_a, qk_norm_a,
           w_qkv_b, qk_norm_b, sink_b, w_qkv_c, qk_norm_c, diff_lambda, diff_subln):
    n_b, t, d = x.shape
    n_c = ctx.shape[1]
    depth = w_ada.shape[0]
    d_ff = w_ffn_out.shape[2]
    assert n_b + 1 <= 8 and t % 1024 == 0 and n_c % LANES == 0 and n_c <= 512

    tm = 512
    tq, tq_diff = 512, 1024
    tq_win = 256
    tq_c = n_c

    c8 = jnp.zeros((8, d), F32).at[:n_b].set(c).at[n_b].set(c_ctx)
    mod5 = _ada_all(c8, w_ada, b_ada).reshape(depth, 8, 3, 3, d)

    rope_tabs = _rope_tables(t)
    eye = jnp.arange(MXU_DIM) // HEAD_DIM
    bd = (eye[:, None] == eye[None, :]).astype(BF16)
    w_in = w_ffn_in.astype(BF16)
    w_out = w_ffn_out.astype(BF16)
    wo = w_o.astype(BF16)
    norm_g4 = norm_g.reshape(depth, 3, 1, d)

    xs = x.reshape(n_b * t, d)
    cs = ctx.reshape(n_b * n_c, d)
    lat = dict(tm=tm, rows_per_batch=t, ctx_row=None)
    cx = dict(tm=n_c, rows_per_batch=n_c, ctx_row=n_b)
    tkv = 512

    for i in range(depth):
        last = i == depth - 1
        kind, j = i % 3, i // 3
        xs = _ffn(xs, mod5, norm_g4, w_in, w_out, i, 0, **lat)
        cs = _ffn(cs, mod5, norm_g4, w_in, w_out, i, 0, **cx)
        if kind == 2:
            wq, wk, wvt = _diff_weights(w_qkv_c[j])
            qk_g = qk_norm_c[j]
        else:
            wq, wk, wvt = _gqa_weights((w_qkv_a, w_qkv_b)[kind][j])
            qk_g = (qk_norm_a, qk_norm_b)[kind][j]
        gq, gk = _lane_gain(qk_g[0]), _lane_gain(qk_g[1])
        q, kl, vtl = _proj(xs, mod5, norm_g4, i, wq, wk, wvt, gq, gk, bd, rope_tabs, tkv=tkv,
                           **lat)
        qc, kc, vtc = _proj(cs, mod5, norm_g4, i, wq, wk, wvt, gq, gk, bd, None, tkv=n_c, **cx)
        vtc = vtc.reshape(n_b, vtc.shape[2], n_c)
        if kind == 0:
            ot = _flash_attn(q, kc, vtc, kl, vtl, mode="gqa", n_batch=n_b, tq=tq)
            if not last:
                otc = _flash_attn(qc, kc, vtc, None, None, mode="gqa", n_batch=n_b, tq=tq_c)
        elif kind == 1:
            ot = _window_attn(q, kc, vtc, kl, vtl, _sink_rows(sink_b[j], tq_win), n_batch=n_b,
                              tq=tq_win)
            if not last:
                otc = _flash_attn(qc, kc, vtc, None, None, mode="gqa", n_batch=n_b, tq=tq_c,
                                  sink_rows=_sink_rows(sink_b[j], tq_c))
        else:
            extra = dict(lam_params=diff_lambda[j], subln=diff_subln[j],
                         lam_init=0.8 - 0.6 * math.exp(-0.3 * i))
            ot = _flash_attn(q, kc, vtc, kl, vtl, mode="diff", n_batch=n_b, tq=tq_diff, **extra)
            if not last:
                otc = _flash_attn(qc, kc, vtc, None, None, mode="diff", n_batch=n_b, tq=tq_c,
                                  **extra)
        xs = _ffn(xs, mod5, norm_g4, w_in, w_out, i, 1, pre=(ot, wo), **lat)
        if not last:
            cs = _ffn(cs, mod5, norm_g4, w_in, w_out, i, 1, pre=(otc, wo), **cx)
    return xs.reshape(n_b, t, d)
```

```python
import functools
import math

import jax
import jax.numpy as jnp
from jax import lax
from jax.experimental import pallas as pl
from jax.experimental.pallas import tpu as pltpu

F32 = jnp.float32
BF16 = jnp.bfloat16

HEAD_DIM = 64
N_KV_HEADS = 4
GQA_GROUP = 4
GRID_W = 64
WINDOW = 128
ROPE_BASE = 10000.0
EPS = 1e-6
NEG_INF = -1e30
LOG2E = math.log2(math.e)
Q_SCALE = HEAD_DIM ** -0.5 * LOG2E
FFN_RESIDUAL = 0.5
LANES = 128
MXU_DIM = 256
SUM_ROWS = 16
PIPE_SLOTS = 4
LOOKAHEAD = 1
V7X_VMEM_BYTES = 64 * 1024 * 1024
SUBLANES = 8


def _tile_bytes(shape, dtype):
    item = jnp.dtype(dtype).itemsize
    dims = [1 if s is None else s for s in shape]
    dims[-1] = pl.cdiv(dims[-1], LANES) * LANES
    if len(dims) > 1:
        rows = SUBLANES * 4 // item
        dims[-2] = pl.cdiv(dims[-2], rows) * rows
    return math.prod(dims) * item


def _pallas(body, *, name, grid, sem, in_specs, args, out_specs, out_shape, scratch=(), temps=0):
    outs = out_shape if isinstance(out_shape, (list, tuple)) else [out_shape]
    o_specs = out_specs if isinstance(out_specs, (list, tuple)) else [out_specs]
    need = temps + sum(_tile_bytes(s.shape, s.dtype) for s in scratch)
    for spec, a in list(zip(in_specs, args)) + list(zip(o_specs, outs)):
        n_buf = spec.pipeline_mode.buffer_count if spec.pipeline_mode is not None else 2
        need += n_buf * _tile_bytes(spec.block_shape, a.dtype)
    return pl.pallas_call(
        body, grid=grid, in_specs=in_specs, out_specs=out_specs, out_shape=out_shape,
        scratch_shapes=list(scratch), name=name,
        compiler_params=pltpu.CompilerParams(dimension_semantics=sem,
                                             vmem_limit_bytes=min(need, V7X_VMEM_BYTES)),
    )(*args)


def _silu(a):
    return a * jax.nn.sigmoid(a)


def _norm_mod(x, g, shift, scale):
    y = x * lax.rsqrt(jnp.mean(x * x, axis=-1, keepdims=True) + EPS)
    return (y * g) * (1.0 + scale) + shift


def _ada_kernel(c_ref, w_ref, b_ref, o_ref):
    s = _silu(c_ref[...])
    o_ref[...] = jnp.dot(s, w_ref[...], preferred_element_type=F32) + b_ref[...]


def _ada_all(c8, w_ada, b_ada):
    depth, d, nd = w_ada.shape
    tn = nd // 8
    return _pallas(
        _ada_kernel, name="adaln", grid=(depth, nd // tn), sem=("parallel", "parallel"),
        in_specs=[
            pl.BlockSpec((8, d), lambda i, j: (0, 0)),
            pl.BlockSpec((None, d, tn), lambda i, j: (i, 0, j)),
            pl.BlockSpec((None, 1, tn), lambda i, j: (i, 0, j)),
        ],
        args=[c8, w_ada, b_ada.reshape(depth, 1, nd)],
        out_specs=pl.BlockSpec((None, 8, tn), lambda i, j: (i, 0, j)),
        out_shape=jax.ShapeDtypeStruct((depth, 8, nd), F32),
        temps=_tile_bytes((8, d), F32) + 2 * _tile_bytes((8, tn), F32))


def _ffn_kernel(*refs, pre):
    if pre:
        x_ref, ot_ref, wo_ref, modp_ref, mod_ref, g_ref, wa_ref, wu_ref, wout_ref, o_ref = refs
    else:
        x_ref, mod_ref, g_ref, wa_ref, wu_ref, wout_ref, o_ref = refs
    x = x_ref[...]
    if pre:
        x = x + modp_ref[2:3, :] * pl.dot(ot_ref[...], wo_ref[...], trans_a=True)
    h = _norm_mod(x, g_ref[...], mod_ref[0:1, :], mod_ref[1:2, :]).astype(BF16)
    a = jnp.dot(h, wa_ref[...], preferred_element_type=F32)
    u = jnp.dot(h, wu_ref[...], preferred_element_type=F32)
    act = (_silu(a) * u).astype(BF16)
    y = jnp.dot(act, wout_ref[...], preferred_element_type=F32)
    o_ref[...] = x + (FFN_RESIDUAL * mod_ref[2:3, :]) * y


def _ffn(x2, mod5, norm_g4, w_in, w_out, layer, half, *, tm, rows_per_batch, ctx_row, pre=None):
    r, d = x2.shape
    dff = w_out.shape[2]
    tpb = rows_per_batch // tm
    k = 2 * half
    resident = dict(pipeline_mode=pl.Buffered(1))

    def bidx(i):
        return ctx_row if ctx_row is not None else i // tpb

    def mod_spec(kk):
        return pl.BlockSpec((None, None, None, 3, d), lambda i: (layer, bidx(i), kk, 0, 0))

    in_specs = [pl.BlockSpec((tm, d), lambda i: (i, 0))]
    args = [x2]
    if pre is not None:
        ot, wo = pre
        in_specs += [
            pl.BlockSpec((None, d, tm), lambda i: (i // tpb, 0, i % tpb)),
            pl.BlockSpec((None, d, d), lambda i: (layer, 0, 0), **resident),
            mod_spec(1),
        ]
        args += [ot, wo, mod5]
    in_specs += [
        mod_spec(k),
        pl.BlockSpec((None, None, 1, d), lambda i: (layer, k, 0, 0)),
        pl.BlockSpec((None, None, d, dff), lambda i: (layer, half, 0, 0), **resident),
        pl.BlockSpec((None, None, d, dff), lambda i: (layer, half, 0, 1), **resident),
        pl.BlockSpec((None, None, dff, d), lambda i: (layer, half, 0, 0), **resident),
    ]
    args += [mod5, norm_g4, w_in, w_in, w_out]
    temps = tm * dff * (4 + 4 + 2) + tm * d * (4 + 4 + 2)
    return _pallas(
        functools.partial(_ffn_kernel, pre=pre is not None),
        name="ffn_pre" if pre is not None else "ffn", grid=(r // tm,), sem=("parallel",),
        in_specs=in_specs, args=args,
        out_specs=pl.BlockSpec((tm, d), lambda i: (i, 0)),
        out_shape=jax.ShapeDtypeStruct((r, d), F32), temps=temps)


def _headnorm_rope(z, gain, cos, sin, bd, rope):
    tm, w = z.shape
    lane = lax.broadcasted_iota(jnp.int32, (1, LANES), 1)
    first_half = (lane & 31) < 16
    outs = []
    for j in range(w // MXU_DIM):
        zj = z[:, j * MXU_DIM:(j + 1) * MXU_DIM]
        sq = zj * zj
        hi = sq.astype(BF16)
        lo = (sq - hi.astype(F32)).astype(BF16)
        ss = (jnp.dot(hi, bd, preferred_element_type=F32)
              + jnp.dot(lo, bd, preferred_element_type=F32))
        zn = zj * lax.rsqrt(ss * (1.0 / HEAD_DIM) + EPS)
        for half in range(MXU_DIM // LANES):
            t = zn[:, half * LANES:(half + 1) * LANES] * gain
            if rope:
                partner = jnp.where(first_half, pltpu.roll(t, LANES - 16, 1),
                                    pltpu.roll(t, 16, 1))
                t = t * cos + partner * sin
            outs.append(t)
    return jnp.concatenate(outs, axis=1)


def _proj_kernel(*refs, rope):
    if rope:
        (x_ref, mod_ref, g_ref, wq_ref, wk_ref, wvt_ref, gq_ref, gk_ref, bd_ref, cos_ref,
         sin_ref, q_ref, k_ref, vt_ref) = refs
        cos, sin = cos_ref[...], sin_ref[...]
    else:
        (x_ref, mod_ref, g_ref, wq_ref, wk_ref, wvt_ref, gq_ref, gk_ref, bd_ref,
         q_ref, k_ref, vt_ref) = refs
        cos = sin = None
    h = _norm_mod(x_ref[...], g_ref[...], mod_ref[0:1, :], mod_ref[1:2, :]).astype(BF16)
    bd = bd_ref[...]
    q = jnp.dot(h, wq_ref[...], preferred_element_type=F32)
    q_ref[...] = _headnorm_rope(q, gq_ref[...] * Q_SCALE, cos, sin, bd, rope).astype(BF16)
    k = jnp.dot(h, wk_ref[...], preferred_element_type=F32)
    k_ref[...] = _headnorm_rope(k, gk_ref[...], cos, sin, bd, rope).astype(BF16)
    vt = pl.dot(wvt_ref[...], h, trans_b=True).astype(BF16)
    tkv = vt_ref.shape[2]
    for c in range(vt_ref.shape[0]):
        vt_ref[c] = vt[:, c * tkv:(c + 1) * tkv]


def _proj(x2, mod5, norm_g4, layer, wq, wk, wvt, gq, gk, bd, rope_tabs, *, tm, rows_per_batch,
          ctx_row, tkv):
    r, d = x2.shape
    wq_n, wk_n, wv_n = wq.shape[1], wk.shape[1], wvt.shape[0]
    tpb = rows_per_batch // tm
    kpt = tm // tkv
    nb = r // rows_per_batch
    rope = rope_tabs is not None

    def bidx(i):
        return ctx_row if ctx_row is not None else i // tpb

    const = lambda i: (0, 0)
    in_specs = [
        pl.BlockSpec((tm, d), lambda i: (i, 0)),
        pl.BlockSpec((None, None, None, 3, d), lambda i: (layer, bidx(i), 1, 0, 0)),
        pl.BlockSpec((None, None, 1, d), lambda i: (layer, 1, 0, 0)),
        pl.BlockSpec((d, wq_n), const),
        pl.BlockSpec((d, wk_n), const),
        pl.BlockSpec((wv_n, d), const),
        pl.BlockSpec((1, LANES), const),
        pl.BlockSpec((1, LANES), const),
        pl.BlockSpec((MXU_DIM, MXU_DIM), const),
    ]
    args = [x2, mod5, norm_g4, wq, wk, wvt, gq, gk, bd]
    if rope:
        in_specs += [pl.BlockSpec((tm, LANES), lambda i: (i % tpb, 0))] * 2
        args += list(rope_tabs)
    temps = tm * d * (4 + 2) + 2 * tm * (wq_n + wk_n + wv_n) * 4
    return _pallas(
        functools.partial(_proj_kernel, rope=rope), name="qkv_proj", grid=(r // tm,),
        sem=("parallel",), in_specs=in_specs, args=args,
        out_specs=[
            pl.BlockSpec((tm, wq_n), lambda i: (i, 0)),
            pl.BlockSpec((tm, wk_n), lambda i: (i, 0)),
            pl.BlockSpec((None, kpt, wv_n, tkv), lambda i: (i // tpb, i % tpb, 0, 0)),
        ],
        out_shape=[
            jax.ShapeDtypeStruct((r, wq_n), BF16),
            jax.ShapeDtypeStruct((r, wk_n), BF16),
            jax.ShapeDtypeStruct((nb, tpb * kpt, wv_n, tkv), BF16),
        ],
        temps=temps)


def _slot_mask(slot):
    lane = lax.broadcasted_iota(jnp.int32, (1, LANES), 1)
    return (lane >= slot * HEAD_DIM) & (lane < (slot + 1) * HEAD_DIM)


def _gqa_q_tile(q_ref, kvh, g):
    j = g + GQA_GROUP * (kvh // 2)
    tile = q_ref[:, j * LANES:(j + 1) * LANES]
    return jnp.where(_slot_mask(kvh % 2), tile, jnp.zeros_like(tile))


_GQA_UNITS = tuple(
    (kvh // 2, kvh * HEAD_DIM,
     tuple((g + GQA_GROUP * (kvh // 2), kvh % 2) for g in range(GQA_GROUP)))
    for kvh in range(N_KV_HEADS))
_DIFF_UNITS = tuple((h, h * 2 * HEAD_DIM, ((h, 0), (h, 1))) for h in range(2))


def _flash_kernel(*refs, units, dv, tq, tk, n_kb, mode, has_sink, lam_init):
    refs = list(refs)
    q_ref, kc_ref, vtc_ref = refs[:3]
    pos = 3
    has_lat = n_kb > 0
    if has_lat:
        kl_ref, vtl_ref = refs[pos:pos + 2]
        pos += 2
    if mode == "diff":
        lam_ref, subln_ref = refs[pos:pos + 2]
        pos += 2
    elif has_sink:
        sink_ref = refs[pos]
        pos += 1
    o_ref, qs_scr, m_scr, acc_scr = refs[pos:pos + 4]
    if has_lat:
        s_scr, mx_scr = refs[pos + 4:]
    n_u = len(units)

    def pv(vt, p):
        ones = jnp.ones((SUM_ROWS, vt.shape[1]), BF16)
        return jnp.dot(jnp.concatenate([vt, ones], axis=0), p.astype(BF16),
                       preferred_element_type=F32)

    for u, (_, _, q_tiles) in enumerate(units):
        for t, (j, slot) in enumerate(q_tiles):
            tile = q_ref[:, j * LANES:(j + 1) * LANES]
            tile = jnp.where(_slot_mask(slot), tile, jnp.zeros_like(tile))
            qs_scr[u, :, t * tq:(t + 1) * tq] = tile.astype(F32).T.astype(BF16)

    for u, (half, v0, _) in enumerate(units):
        s = jnp.dot(kc_ref[:, half * LANES:(half + 1) * LANES], qs_scr[u],
                    preferred_element_type=F32)
        m = jnp.max(s, axis=0, keepdims=True)
        p = jnp.exp2(s - m)
        m_scr[u] = m
        acc_scr[u] = pv(vtc_ref[v0:v0 + dv, :], p)

    if has_lat:
        col_tiles = [slice(c0, c0 + MXU_DIM) for c0 in range(0, s_scr.shape[2], MXU_DIM)]

        key_halves = [slice(r0, r0 + MXU_DIM) for r0 in range(0, tk, MXU_DIM)]

        def scores(kb, u, slot, cs, rs):
            half = units[u][0]
            row0 = pl.multiple_of(kb * tk, tk) + rs.start
            k = kl_ref[pl.ds(row0, MXU_DIM), half * LANES:(half + 1) * LANES]
            s = jnp.dot(k, qs_scr[u, :, cs], preferred_element_type=F32)
            s_scr[slot, rs, cs] = s
            return jnp.max(s, axis=0, keepdims=True)

        def probs(u, slot, cs):
            m_prev = m_scr[u, :, cs]
            m_new = jnp.maximum(m_prev, mx_scr[slot, :, cs])
            m_scr[u, :, cs] = m_new
            return (jnp.exp2(s_scr[slot, :, cs] - m_new).astype(BF16),
                    jnp.exp2(m_prev - m_new))

        def pv_part(kb, u, p, rs):
            v0 = units[u][1]
            return pv(vtl_ref[kb, v0:v0 + dv, rs], p[rs, :])

        bpt = PIPE_SLOTS // n_u
        n_trips = n_kb // bpt

        def trip(it, last):
            pending = None
            for j in range(PIPE_SLOTS):
                ahead = j + LOOKAHEAD
                if ahead < PIPE_SLOTS:
                    sc = (it * bpt + ahead // n_u, ahead % n_u, ahead)
                elif not last:
                    a2 = ahead - PIPE_SLOTS
                    sc = ((it + 1) * bpt + a2 // n_u, a2 % n_u, a2)
                else:
                    sc = None
                for cs in col_tiles:
                    fresh = (it * bpt + j // n_u, j % n_u, cs, *probs(j % n_u, j, cs))
                    mx, parts = None, []
                    for rs in key_halves:
                        if sc is not None:
                            part = scores(*sc, cs, rs)
                            mx = part if mx is None else jnp.maximum(mx, part)
                        if pending is not None:
                            parts.append(pv_part(pending[0], pending[1], pending[3], rs))
                    if sc is not None:
                        mx_scr[sc[2], :, cs] = mx
                    if pending is not None:
                        _, u_p, cs_p, _, alpha_p = pending
                        acc_scr[u_p, :, cs_p] = alpha_p * acc_scr[u_p, :, cs_p] + sum(parts)
                    pending = fresh
            kb_p, u_p, cs_p, p, alpha_p = pending
            acc_scr[u_p, :, cs_p] = alpha_p * acc_scr[u_p, :, cs_p] + sum(
                pv_part(kb_p, u_p, p, rs) for rs in key_halves)

        for j in range(LOOKAHEAD):
            for cs in col_tiles:
                mx_scr[j, :, cs] = functools.reduce(
                    jnp.maximum, [scores(j // n_u, j % n_u, j, cs, rs) for rs in key_halves])
        lax.fori_loop(0, n_trips - 1, lambda it, c: (trip(it, False), c)[1], 0)
        trip(n_trips - 1, True)

    if mode == "diff":
        lp = lam_ref[...]
        lam = (jnp.exp(jnp.sum(lp[0:1] * lp[1:2], axis=1, keepdims=True))
               - jnp.exp(jnp.sum(lp[2:3] * lp[3:4], axis=1, keepdims=True)) + lam_init)
        gain = subln_ref[...]
        for u in range(n_u):
            o12 = acc_scr[u, :dv, :] / acc_scr[u, dv:dv + 1, :]
            o = o12[:, :tq] - lam * o12[:, tq:]
            o = o * lax.rsqrt(jnp.mean(o * o, axis=0, keepdims=True) + EPS)
            o_ref[u * dv:(u + 1) * dv, :] = ((o * gain) * (1.0 - lam_init)).astype(BF16)
    else:
        for u in range(n_u):
            m, l, acc = m_scr[u], acc_scr[u, dv:dv + 1, :], acc_scr[u, :dv, :]
            if has_sink:
                sk = sink_ref[u] * LOG2E
                m_f = jnp.maximum(m, sk)
                w = jnp.exp2(m - m_f)
                l = l * w + jnp.exp2(sk - m_f)
                acc = acc * w
            o = acc / l
            for g in range(GQA_GROUP):
                row = (u * GQA_GROUP + g) * HEAD_DIM
                o_ref[row:row + HEAD_DIM, :] = o[:, g * tq:(g + 1) * tq].astype(BF16)


def _flash_attn(q, kc, vtc, kl, vtl, *, mode, n_batch, tq, sink_rows=None, lam_params=None,
                subln=None, lam_init=0.0):
    r, dq = q.shape
    t_q = r // n_batch
    c = kc.shape[0] // n_batch
    wk = kc.shape[1]
    n_grp = wk // MXU_DIM
    q_cols = dq // n_grp
    units, dv = (_GQA_UNITS, HEAD_DIM) if mode == "gqa" else (_DIFF_UNITS, 2 * HEAD_DIM)
    cols = len(units[0][2]) * tq
    has_lat = kl is not None
    nq = t_q // tq
    in_specs = [
        pl.BlockSpec((tq, q_cols), lambda b, g, i: (b * nq + i, g)),
        pl.BlockSpec((c, MXU_DIM), lambda b, g, i: (b, g)),
        pl.BlockSpec((None, MXU_DIM, c), lambda b, g, i: (b, g, 0)),
    ]
    args = [q, kc, vtc]
    n_kb, tk = 0, 0
    scratch = [
        pltpu.VMEM((len(units), LANES, cols), BF16),
        pltpu.VMEM((len(units), 1, cols), F32),
        pltpu.VMEM((len(units), dv + SUM_ROWS, cols), F32),
    ]
    if has_lat:
        t = kl.shape[0] // n_batch
        n_kb, tk = vtl.shape[1], vtl.shape[3]
        in_specs += [
            pl.BlockSpec((t, MXU_DIM), lambda b, g, i: (b, g)),
            pl.BlockSpec((None, n_kb, MXU_DIM, tk), lambda b, g, i: (b, 0, g, 0)),
        ]
        args += [kl, vtl]
        assert PIPE_SLOTS % len(units) == 0 and n_kb % (PIPE_SLOTS // len(units)) == 0
        scratch += [pltpu.VMEM((PIPE_SLOTS, tk, cols), F32),
                    pltpu.VMEM((PIPE_SLOTS, 1, cols), F32)]
    if mode == "diff":
        in_specs += [
            pl.BlockSpec((4, HEAD_DIM), lambda b, g, i: (0, 0)),
            pl.BlockSpec((dv, 1), lambda b, g, i: (0, 0)),
        ]
        args += [lam_params, subln.reshape(dv, 1)]
    elif sink_rows is not None:
        in_specs.append(pl.BlockSpec((N_KV_HEADS, 1, cols), lambda b, g, i: (0, 0, 0)))
        args.append(sink_rows)
    temps = c * cols * (4 + 4 + 2) + 4 * max(tk, c) * MXU_DIM * (4 + 4 + 2)
    return _pallas(
        functools.partial(_flash_kernel, units=units, dv=dv, tq=tq, tk=tk, n_kb=n_kb, mode=mode,
                          has_sink=sink_rows is not None, lam_init=lam_init),
        name=mode + "_attn", grid=(n_batch, n_grp, nq),
        sem=("parallel", "parallel", "arbitrary"), in_specs=in_specs, args=args,
        out_specs=pl.BlockSpec((None, q_cols, tq), lambda b, g, i: (b, g, i)),
        out_shape=jax.ShapeDtypeStruct((n_batch, dq, t_q), BF16),
        scratch=scratch, temps=temps)


def _window_kernel(q_ref, kc_ref, vtc_ref, kp_ref, kcur_ref, kn_ref, vtp_ref, vtcur_ref,
                   vtn_ref, sink_ref, o_ref, *, tq):
    qi = pl.program_id(1)
    nq = pl.num_programs(1)
    n_lat = tq + 2 * WINDOW
    rr = lax.broadcasted_iota(jnp.int32, (n_lat, 1), 0)
    cc = lax.broadcasted_iota(jnp.int32, (1, tq), 1)
    valid = jnp.abs(rr - WINDOW - cc) <= WINDOW
    valid &= (rr >= WINDOW) | (qi > 0)
    valid &= (rr < tq + WINDOW) | (qi < nq - 1)
    bias = jnp.where(valid, 0.0, NEG_INF)

    k_lat = [jnp.concatenate([r[:, h * LANES:(h + 1) * LANES] for r in (kp_ref, kcur_ref, kn_ref)],
                             axis=0) for h in range(2)]

    def pv(vt, p):
        ones = jnp.ones((SUM_ROWS, vt.shape[1]), BF16)
        return jnp.dot(jnp.concatenate([vt, ones], axis=0), p, preferred_element_type=F32)

    def scores(kvh, g):
        half = kvh // 2
        qt = _gqa_q_tile(q_ref, kvh, g).astype(F32).T.astype(BF16)
        s_ctx = jnp.dot(kc_ref[:, half * LANES:(half + 1) * LANES], qt,
                        preferred_element_type=F32)
        s_lat = jnp.dot(k_lat[half], qt, preferred_element_type=F32) + bias
        sk = sink_ref[kvh, :, g * tq:(g + 1) * tq] * LOG2E
        m = jnp.maximum(jnp.maximum(jnp.max(s_ctx, axis=0, keepdims=True),
                                    jnp.max(s_lat, axis=0, keepdims=True)), sk)
        return s_ctx, s_lat, m, jnp.exp2(sk - m)

    def probs(s_ctx, s_lat, m, p_sink):
        return jnp.exp2(s_ctx - m).astype(BF16), jnp.exp2(s_lat - m).astype(BF16), p_sink

    def output(kvh, g, p_ctx, p_lat, p_sink):
        hs = slice(kvh * HEAD_DIM, (kvh + 1) * HEAD_DIM)
        vt_lat = jnp.concatenate([vtp_ref[hs, :], vtcur_ref[hs, :], vtn_ref[hs, :]], axis=1)
        acc = pv(vtc_ref[hs, :], p_ctx) + pv(vt_lat, p_lat)
        l = acc[HEAD_DIM:HEAD_DIM + 1, :] + p_sink
        row = (kvh * GQA_GROUP + g) * HEAD_DIM
        o_ref[row:row + HEAD_DIM, :] = (acc[:HEAD_DIM, :] / l).astype(BF16)

    tiles = [(kvh, g) for kvh in range(N_KV_HEADS) for g in range(GQA_GROUP)]
    scored, pending = scores(*tiles[0]), None
    for t, tile in enumerate(tiles):
        ahead = scores(*tiles[t + 1]) if t + 1 < len(tiles) else None
        fresh = (*tile, *probs(*scored))
        if pending is not None:
            output(*pending)
        scored, pending = ahead, fresh
    output(*pending)


def _window_attn(q, kc, vtc, kl, vtl, sink_rows, *, n_batch, tq):
    r, dq = q.shape
    t = r // n_batch
    c = kc.shape[0] // n_batch
    wk = kc.shape[1]
    tkv = vtl.shape[3]
    nq = t // tq
    rb = tq // WINDOW
    nwb = t // WINDOW
    wpb, qpb = tkv // WINDOW, tkv // tq
    prev = lambda b, i: jnp.maximum(i * rb - 1, 0)
    nxt = lambda b, i: jnp.minimum((i + 1) * rb, nwb - 1)
    in_specs = [
        pl.BlockSpec((tq, dq), lambda b, i: (b * nq + i, 0)),
        pl.BlockSpec((c, wk), lambda b, i: (b, 0)),
        pl.BlockSpec((None, wk, c), lambda b, i: (b, 0, 0)),
        pl.BlockSpec((WINDOW, wk), lambda b, i: (b * nwb + prev(b, i), 0)),
        pl.BlockSpec((tq, wk), lambda b, i: (b * nq + i, 0)),
        pl.BlockSpec((WINDOW, wk), lambda b, i: (b * nwb + nxt(b, i), 0)),
        pl.BlockSpec((None, None, wk, WINDOW),
                     lambda b, i: (b, prev(b, i) // wpb, 0, prev(b, i) % wpb)),
        pl.BlockSpec((None, None, wk, tq), lambda b, i: (b, i // qpb, 0, i % qpb)),
        pl.BlockSpec((None, None, wk, WINDOW),
                     lambda b, i: (b, nxt(b, i) // wpb, 0, nxt(b, i) % wpb)),
        pl.BlockSpec((N_KV_HEADS, 1, GQA_GROUP * tq), lambda b, i: (0, 0, 0)),
    ]
    n_keys = c + tq + 2 * WINDOW
    temps = n_keys * GQA_GROUP * tq * (4 + 4 + 2) + GQA_GROUP * tq * LANES * 2
    return _pallas(
        functools.partial(_window_kernel, tq=tq), name="window_attn", grid=(n_batch, nq),
        sem=("parallel", "parallel"), in_specs=in_specs,
        args=[q, kc, vtc, kl, kl, kl, vtl, vtl, vtl, sink_rows],
        out_specs=pl.BlockSpec((None, dq, tq), lambda b, i: (b, 0, i)),
        out_shape=jax.ShapeDtypeStruct((n_batch, dq, t), BF16), temps=temps)


def _rope_tables(n_tokens):
    rows = n_tokens // GRID_W
    row = jnp.repeat(jnp.arange(rows), GRID_W)
    col = jnp.tile(jnp.arange(GRID_W), rows)
    n_freq = HEAD_DIM // 4
    inv_freq = ROPE_BASE ** (-jnp.arange(n_freq, dtype=F32) / n_freq)
    ang = jnp.stack([row, col], axis=-1).astype(F32)[:, :, None] * inv_freq
    cos, sin = jnp.cos(ang), jnp.sin(ang)
    cos_h = jnp.concatenate([cos, cos], axis=-1).reshape(n_tokens, HEAD_DIM)
    sin_h = jnp.concatenate([-sin, sin], axis=-1).reshape(n_tokens, HEAD_DIM)
    return jnp.tile(cos_h, (1, 2)), jnp.tile(sin_h, (1, 2))


def _gqa_weights(w_qkv):
    d = w_qkv.shape[0]
    n_q = N_KV_HEADS * GQA_GROUP * HEAD_DIM
    n_kv = N_KV_HEADS * HEAD_DIM
    wq = w_qkv[:, :n_q].reshape(d, N_KV_HEADS // 2, 2, GQA_GROUP, HEAD_DIM)
    wq = wq.transpose(0, 1, 3, 2, 4).reshape(d, n_q)
    wk = w_qkv[:, n_q:n_q + n_kv]
    wvt = w_qkv[:, n_q + n_kv:].T
    return wq.astype(BF16), wk.astype(BF16), wvt.astype(BF16)


def _diff_weights(w_qkv):
    n = w_qkv.shape[1] // 3
    return (w_qkv[:, :n].astype(BF16), w_qkv[:, n:2 * n].astype(BF16),
            w_qkv[:, 2 * n:].T.astype(BF16))


def _lane_gain(g):
    return jnp.tile(g.reshape(1, HEAD_DIM), (1, LANES // HEAD_DIM))


def _sink_rows(sink, tq):
    return jnp.repeat(sink.reshape(N_KV_HEADS, 1, GQA_GROUP, 1), tq, axis=3).reshape(
        N_KV_HEADS, 1, GQA_GROUP * tq)


def kernel(x, c, ctx, c_ctx, norm_g, w_ada, b_ada, w_ffn_in, w_ffn_out, w_o, w_qkv_a, qk_norm_a,
           w_qkv_b, qk_norm_b, sink_b, w_qkv_c, qk_norm_c, diff_lambda, diff_subln):
    n_b, t, d = x.shape
    n_c = ctx.shape[1]
    depth = w_ada.shape[0]
    d_ff = w_ffn_out.shape[2]
    assert n_b + 1 <= 8 and t % 1024 == 0 and n_c % LANES == 0 and n_c <= 512

    tm = 512
    tm_ffn = 1024
    tq, tq_diff = 512, 1024
    tq_win = 256
    tq_c = n_c

    c8 = jnp.zeros((8, d), F32).at[:n_b].set(c).at[n_b].set(c_ctx)
    mod5 = _ada_all(c8, w_ada, b_ada).reshape(depth, 8, 3, 3, d)

    rope_tabs = _rope_tables(t)
    eye = jnp.arange(MXU_DIM) // HEAD_DIM
    bd = (eye[:, None] == eye[None, :]).astype(BF16)
    w_in = w_ffn_in.astype(BF16)
    w_out = w_ffn_out.astype(BF16)
    wo = w_o.astype(BF16)
    norm_g4 = norm_g.reshape(depth, 3, 1, d)

    xs = x.reshape(n_b * t, d)
    cs = ctx.reshape(n_b * n_c, d)
    lat = dict(tm=tm, rows_per_batch=t, ctx_row=None)
    lat_ffn = dict(tm=tm_ffn, rows_per_batch=t, ctx_row=None)
    cx = dict(tm=n_c, rows_per_batch=n_c, ctx_row=n_b)
    tkv = 512

    for i in range(depth):
        last = i == depth - 1
        kind, j = i % 3, i // 3
        xs = _ffn(xs, mod5, norm_g4, w_in, w_out, i, 0, **lat_ffn)
        cs = _ffn(cs, mod5, norm_g4, w_in, w_out, i, 0, **cx)
        if kind == 2:
            wq, wk, wvt = _diff_weights(w_qkv_c[j])
            qk_g = qk_norm_c[j]
        else:
            wq, wk, wvt = _gqa_weights((w_qkv_a, w_qkv_b)[kind][j])
            qk_g = (qk_norm_a, qk_norm_b)[kind][j]
        gq, gk = _lane_gain(qk_g[0]), _lane_gain(qk_g[1])
        q, kl, vtl = _proj(xs, mod5, norm_g4, i, wq, wk, wvt, gq, gk, bd, rope_tabs, tkv=tkv,
                           **lat)
        qc, kc, vtc = _proj(cs, mod5, norm_g4, i, wq, wk, wvt, gq, gk, bd, None, tkv=n_c, **cx)
        vtc = vtc.reshape(n_b, vtc.shape[2], n_c)
        if kind == 0:
            ot = _flash_attn(q, kc, vtc, kl, vtl, mode="gqa", n_batch=n_b, tq=tq)
            if not last:
                otc = _flash_attn(qc, kc, vtc, None, None, mode="gqa", n_batch=n_b, tq=tq_c)
        elif kind == 1:
            ot = _window_attn(q, kc, vtc, kl, vtl, _sink_rows(sink_b[j], tq_win), n_batch=n_b,
                              tq=tq_win)
            if not last:
                otc = _flash_attn(qc, kc, vtc, None, None, mode="gqa", n_batch=n_b, tq=tq_c,
                                  sink_rows=_sink_rows(sink_b[j], tq_c))
        else:
            extra = dict(lam_params=diff_lambda[j], subln=diff_subln[j],
                         lam_init=0.8 - 0.6 * math.exp(-0.3 * i))
            ot = _flash_attn(q, kc, vtc, kl, vtl, mode="diff", n_batch=n_b, tq=tq_diff, **extra)
            if not last:
                otc = _flash_attn(qc, kc, vtc, None, None, mode="diff", n_batch=n_b, tq=tq_c,
                                  **extra)
        xs = _ffn(xs, mod5, norm_g4, w_in, w_out, i, 1, pre=(ot, wo), **lat_ffn)
        if not last:
            cs = _ffn(cs, mod5, norm_g4, w_in, w_out, i, 1, pre=(otc, wo), **cx)
    return xs.reshape(n_b, t, d)
```

```python
import functools
import math

import jax
import jax.numpy as jnp
from jax import lax
from jax.experimental import pallas as pl
from jax.experimental.pallas import tpu as pltpu

F32 = jnp.float32
BF16 = jnp.bfloat16

HEAD_DIM = 64
N_KV_HEADS = 4
GQA_GROUP = 4
GRID_W = 64
WINDOW = 128
ROPE_BASE = 10000.0
N_FREQ = HEAD_DIM // 4
EPS = 1e-6
NEG_INF = -1e30
LOG2E = math.log2(math.e)
Q_SCALE = HEAD_DIM ** -0.5 * LOG2E
FFN_RESIDUAL = 0.5
LANES = 128
MXU_DIM = 256
SUM_ROWS = 16
PIPE_SLOTS = 4
LOOKAHEAD = 1
SCORE_BUFS = LOOKAHEAD + 1
V7X_VMEM_BYTES = 64 * 1024 * 1024
SUBLANES = 8


def _tile_bytes(shape, dtype):
    item = jnp.dtype(dtype).itemsize
    dims = [1 if s is None else s for s in shape]
    dims[-1] = pl.cdiv(dims[-1], LANES) * LANES
    if len(dims) > 1:
        rows = SUBLANES * 4 // item
        dims[-2] = pl.cdiv(dims[-2], rows) * rows
    return math.prod(dims) * item


def _pallas(body, *, name, grid, sem, in_specs, args, out_specs, out_shape, scratch=(), temps=0):
    outs = out_shape if isinstance(out_shape, (list, tuple)) else [out_shape]
    o_specs = out_specs if isinstance(out_specs, (list, tuple)) else [out_specs]
    need = temps + sum(_tile_bytes(s.shape, s.dtype) for s in scratch)
    for spec, a in list(zip(in_specs, args)) + list(zip(o_specs, outs)):
        n_buf = spec.pipeline_mode.buffer_count if spec.pipeline_mode is not None else 2
        need += n_buf * _tile_bytes(spec.block_shape, a.dtype)
    return pl.pallas_call(
        body, grid=grid, in_specs=in_specs, out_specs=out_specs, out_shape=out_shape,
        scratch_shapes=list(scratch), name=name,
        compiler_params=pltpu.CompilerParams(dimension_semantics=sem,
                                             vmem_limit_bytes=min(need, V7X_VMEM_BYTES)),
    )(*args)


def _silu(a):
    return a * jax.nn.sigmoid(a)


def _norm_mod(x, g, shift, scale):
    y = x * lax.rsqrt(jnp.mean(x * x, axis=-1, keepdims=True) + EPS)
    return (y * g) * (1.0 + scale) + shift


def _ada_kernel(c_ref, w_ref, b_ref, o_ref):
    s = _silu(c_ref[...])
    o_ref[...] = jnp.dot(s, w_ref[...], preferred_element_type=F32) + b_ref[...]


def _ada_all(c8, w_ada, b_ada):
    depth, d, nd = w_ada.shape
    rows = c8.shape[0]
    tn = nd // 8
    return _pallas(
        _ada_kernel, name="adaln", grid=(depth, nd // tn), sem=("parallel", "parallel"),
        in_specs=[
            pl.BlockSpec((rows, d), lambda i, j: (0, 0)),
            pl.BlockSpec((None, d, tn), lambda i, j: (i, 0, j)),
            pl.BlockSpec((None, 1, tn), lambda i, j: (i, 0, j)),
        ],
        args=[c8, w_ada, b_ada.reshape(depth, 1, nd)],
        out_specs=pl.BlockSpec((None, rows, tn), lambda i, j: (i, 0, j)),
        out_shape=jax.ShapeDtypeStruct((depth, rows, nd), F32),
        temps=_tile_bytes((rows, d), F32) + 2 * _tile_bytes((rows, tn), F32))


def _ffn_kernel(*refs, pre):
    if pre:
        x_ref, ot_ref, wo_ref, modp_ref, mod_ref, g_ref, wa_ref, wu_ref, wout_ref, o_ref = refs
    else:
        x_ref, mod_ref, g_ref, wa_ref, wu_ref, wout_ref, o_ref = refs
    x = x_ref[...]
    if pre:
        x = x + modp_ref[2:3, :] * pl.dot(ot_ref[...], wo_ref[...], trans_a=True)
    h = _norm_mod(x, g_ref[...], mod_ref[0:1, :], mod_ref[1:2, :]).astype(BF16)
    a = jnp.dot(h, wa_ref[...], preferred_element_type=F32)
    u = jnp.dot(h, wu_ref[...], preferred_element_type=F32)
    act = (_silu(a) * u).astype(BF16)
    y = jnp.dot(act, wout_ref[...], preferred_element_type=F32)
    o_ref[...] = x + (FFN_RESIDUAL * mod_ref[2:3, :]) * y


def _ffn(x2, mod5, norm_g4, w_in, w_out, layer, half, *, tm, rows_per_batch, ctx_row, pre=None):
    r, d = x2.shape
    dff = w_out.shape[2]
    tpb = rows_per_batch // tm
    k = 2 * half
    resident = dict(pipeline_mode=pl.Buffered(1))

    def bidx(i):
        return ctx_row if ctx_row is not None else i // tpb

    def mod_spec(kk):
        return pl.BlockSpec((None, None, None, 3, d), lambda i: (layer, bidx(i), kk, 0, 0))

    in_specs = [pl.BlockSpec((tm, d), lambda i: (i, 0))]
    args = [x2]
    if pre is not None:
        ot, wo = pre
        in_specs += [
            pl.BlockSpec((None, d, tm), lambda i: (i // tpb, 0, i % tpb)),
            pl.BlockSpec((None, d, d), lambda i: (layer, 0, 0), **resident),
            mod_spec(1),
        ]
        args += [ot, wo, mod5]
    in_specs += [
        mod_spec(k),
        pl.BlockSpec((None, None, 1, d), lambda i: (layer, k, 0, 0)),
        pl.BlockSpec((None, None, d, dff), lambda i: (layer, half, 0, 0), **resident),
        pl.BlockSpec((None, None, d, dff), lambda i: (layer, half, 0, 1), **resident),
        pl.BlockSpec((None, None, dff, d), lambda i: (layer, half, 0, 0), **resident),
    ]
    args += [mod5, norm_g4, w_in, w_in, w_out]
    temps = tm * dff * (4 + 4 + 2) + tm * d * (4 + 4 + 2)
    return _pallas(
        functools.partial(_ffn_kernel, pre=pre is not None),
        name="ffn_pre" if pre is not None else "ffn", grid=(r // tm,), sem=("parallel",),
        in_specs=in_specs, args=args,
        out_specs=pl.BlockSpec((tm, d), lambda i: (i, 0)),
        out_shape=jax.ShapeDtypeStruct((r, d), F32), temps=temps)


def _headnorm_rope(z, gain, cos, sin, bd, rope):
    tm, w = z.shape
    lane = lax.broadcasted_iota(jnp.int32, (1, LANES), 1)
    first_half = (lane & (2 * N_FREQ - 1)) < N_FREQ
    outs = []
    for j in range(w // MXU_DIM):
        zj = z[:, j * MXU_DIM:(j + 1) * MXU_DIM]
        sq = zj * zj
        hi = sq.astype(BF16)
        lo = (sq - hi.astype(F32)).astype(BF16)
        ss = (jnp.dot(hi, bd, preferred_element_type=F32)
              + jnp.dot(lo, bd, preferred_element_type=F32))
        zn = zj * lax.rsqrt(ss * (1.0 / HEAD_DIM) + EPS)
        for half in range(MXU_DIM // LANES):
            t = zn[:, half * LANES:(half + 1) * LANES] * gain
            if rope:
                partner = jnp.where(first_half, pltpu.roll(t, LANES - N_FREQ, 1),
                                    pltpu.roll(t, N_FREQ, 1))
                t = t * cos + partner * sin
            outs.append(t)
    return jnp.concatenate(outs, axis=1)


def _proj_kernel(*refs, rope):
    if rope:
        (x_ref, mod_ref, g_ref, wq_ref, wk_ref, wvt_ref, gq_ref, gk_ref, bd_ref, cos_ref,
         sin_ref, q_ref, k_ref, vt_ref) = refs
        cos, sin = cos_ref[...], sin_ref[...]
    else:
        (x_ref, mod_ref, g_ref, wq_ref, wk_ref, wvt_ref, gq_ref, gk_ref, bd_ref,
         q_ref, k_ref, vt_ref) = refs
        cos = sin = None
    h = _norm_mod(x_ref[...], g_ref[...], mod_ref[0:1, :], mod_ref[1:2, :]).astype(BF16)
    bd = bd_ref[...]
    q = jnp.dot(h, wq_ref[...], preferred_element_type=F32)
    q_ref[...] = _headnorm_rope(q, gq_ref[...] * Q_SCALE, cos, sin, bd, rope).astype(BF16)
    k = jnp.dot(h, wk_ref[...], preferred_element_type=F32)
    k_ref[...] = _headnorm_rope(k, gk_ref[...], cos, sin, bd, rope).astype(BF16)
    vt = pl.dot(wvt_ref[...], h, trans_b=True).astype(BF16)
    tkv = vt_ref.shape[2]
    for c in range(vt_ref.shape[0]):
        vt_ref[c] = vt[:, c * tkv:(c + 1) * tkv]


def _proj(x2, mod5, norm_g4, layer, wq, wk, wvt, gq, gk, bd, rope_tabs, *, tm, rows_per_batch,
          ctx_row, tkv):
    r, d = x2.shape
    wq_n, wk_n, wv_n = wq.shape[1], wk.shape[1], wvt.shape[0]
    tpb = rows_per_batch // tm
    kpt = tm // tkv
    nb = r // rows_per_batch
    rope = rope_tabs is not None

    def bidx(i):
        return ctx_row if ctx_row is not None else i // tpb

    const = lambda i: (0, 0)
    in_specs = [
        pl.BlockSpec((tm, d), lambda i: (i, 0)),
        pl.BlockSpec((None, None, None, 3, d), lambda i: (layer, bidx(i), 1, 0, 0)),
        pl.BlockSpec((None, None, 1, d), lambda i: (layer, 1, 0, 0)),
        pl.BlockSpec((d, wq_n), const),
        pl.BlockSpec((d, wk_n), const),
        pl.BlockSpec((wv_n, d), const),
        pl.BlockSpec((1, LANES), const),
        pl.BlockSpec((1, LANES), const),
        pl.BlockSpec((MXU_DIM, MXU_DIM), const),
    ]
    args = [x2, mod5, norm_g4, wq, wk, wvt, gq, gk, bd]
    if rope:
        in_specs += [pl.BlockSpec((tm, LANES), lambda i: (i % tpb, 0))] * 2
        args += list(rope_tabs)
    temps = tm * d * (4 + 2) + 2 * tm * (wq_n + wk_n + wv_n) * 4
    return _pallas(
        functools.partial(_proj_kernel, rope=rope), name="qkv_proj", grid=(r // tm,),
        sem=("parallel",), in_specs=in_specs, args=args,
        out_specs=[
            pl.BlockSpec((tm, wq_n), lambda i: (i, 0)),
            pl.BlockSpec((tm, wk_n), lambda i: (i, 0)),
            pl.BlockSpec((None, kpt, wv_n, tkv), lambda i: (i // tpb, i % tpb, 0, 0)),
        ],
        out_shape=[
            jax.ShapeDtypeStruct((r, wq_n), BF16),
            jax.ShapeDtypeStruct((r, wk_n), BF16),
            jax.ShapeDtypeStruct((nb, tpb * kpt, wv_n, tkv), BF16),
        ],
        temps=temps)


def _slot_mask(slot):
    lane = lax.broadcasted_iota(jnp.int32, (1, LANES), 1)
    return (lane >= slot * HEAD_DIM) & (lane < (slot + 1) * HEAD_DIM)


def _gqa_q_tile(q_ref, kvh, g):
    j = g + GQA_GROUP * (kvh // 2)
    tile = q_ref[:, j * LANES:(j + 1) * LANES]
    return jnp.where(_slot_mask(kvh % 2), tile, jnp.zeros_like(tile))


_GQA_UNITS = tuple(
    (kvh // 2, kvh * HEAD_DIM,
     tuple((g + GQA_GROUP * (kvh // 2), kvh % 2) for g in range(GQA_GROUP)))
    for kvh in range(N_KV_HEADS))
_DIFF_UNITS = tuple((h, h * 2 * HEAD_DIM, ((h, 0), (h, 1))) for h in range(2))


def _flash_kernel(*refs, units, dv, tq, tk, n_kb, mode, has_sink, lam_init):
    refs = list(refs)
    q_ref, kc_ref, vtc_ref = refs[:3]
    pos = 3
    has_lat = n_kb > 0
    if has_lat:
        kl_ref, vtl_ref = refs[pos:pos + 2]
        pos += 2
    if mode == "diff":
        lam_ref, subln_ref = refs[pos:pos + 2]
        pos += 2
    elif has_sink:
        sink_ref = refs[pos]
        pos += 1
    o_ref, qs_scr, m_scr, acc_scr = refs[pos:pos + 4]
    if has_lat:
        s_scr, mx_scr = refs[pos + 4:]
    n_u = len(units)

    def pv(vt, p):
        ones = jnp.ones((SUM_ROWS, vt.shape[1]), BF16)
        return jnp.dot(jnp.concatenate([vt, ones], axis=0), p.astype(BF16),
                       preferred_element_type=F32)

    for u, (_, _, q_tiles) in enumerate(units):
        for t, (j, slot) in enumerate(q_tiles):
            tile = q_ref[:, j * LANES:(j + 1) * LANES]
            tile = jnp.where(_slot_mask(slot), tile, jnp.zeros_like(tile))
            qs_scr[u, :, t * tq:(t + 1) * tq] = tile.astype(F32).T.astype(BF16)

    for u, (half, v0, _) in enumerate(units):
        s = jnp.dot(kc_ref[:, half * LANES:(half + 1) * LANES], qs_scr[u],
                    preferred_element_type=F32)
        m = jnp.max(s, axis=0, keepdims=True)
        p = jnp.exp2(s - m)
        m_scr[u] = m
        acc_scr[u] = pv(vtc_ref[v0:v0 + dv, :], p)

    if has_lat:
        col_tiles = [slice(c0, c0 + MXU_DIM) for c0 in range(0, s_scr.shape[2], MXU_DIM)]

        key_halves = [slice(r0, r0 + MXU_DIM) for r0 in range(0, tk, MXU_DIM)]

        def scores(kb, u, slot, cs, rs):
            half = units[u][0]
            row0 = pl.multiple_of(kb * tk, tk) + rs.start
            k = kl_ref[pl.ds(row0, MXU_DIM), half * LANES:(half + 1) * LANES]
            s = jnp.dot(k, qs_scr[u, :, cs], preferred_element_type=F32)
            s_scr[slot, rs, cs] = s
            return jnp.max(s, axis=0, keepdims=True)

        def probs(u, slot, cs):
            m_prev = m_scr[u, :, cs]
            m_new = jnp.maximum(m_prev, mx_scr[slot, :, cs])
            m_scr[u, :, cs] = m_new
            return (jnp.exp2(s_scr[slot, :, cs] - m_new).astype(BF16),
                    jnp.exp2(m_prev - m_new))

        def pv_part(kb, u, p, rs):
            v0 = units[u][1]
            return pv(vtl_ref[kb, v0:v0 + dv, rs], p[rs, :])

        bpt = PIPE_SLOTS // n_u
        n_trips = n_kb // bpt

        def trip(it, last):
            pending = None
            for j in range(PIPE_SLOTS):
                ahead = j + LOOKAHEAD
                if ahead < PIPE_SLOTS:
                    sc = (it * bpt + ahead // n_u, ahead % n_u, ahead % SCORE_BUFS)
                elif not last:
                    a2 = ahead - PIPE_SLOTS
                    sc = ((it + 1) * bpt + a2 // n_u, a2 % n_u, a2 % SCORE_BUFS)
                else:
                    sc = None
                for cs in col_tiles:
                    fresh = (it * bpt + j // n_u, j % n_u, cs,
                             *probs(j % n_u, j % SCORE_BUFS, cs))
                    mx, parts = None, []
                    for rs in key_halves:
                        if sc is not None:
                            part = scores(*sc, cs, rs)
                            mx = part if mx is None else jnp.maximum(mx, part)
                        if pending is not None:
                            parts.append(pv_part(pending[0], pending[1], pending[3], rs))
                    if sc is not None:
                        mx_scr[sc[2], :, cs] = mx
                    if pending is not None:
                        _, u_p, cs_p, _, alpha_p = pending
                        acc_scr[u_p, :, cs_p] = alpha_p * acc_scr[u_p, :, cs_p] + sum(parts)
                    pending = fresh
            kb_p, u_p, cs_p, p, alpha_p = pending
            acc_scr[u_p, :, cs_p] = alpha_p * acc_scr[u_p, :, cs_p] + sum(
                pv_part(kb_p, u_p, p, rs) for rs in key_halves)

        for j in range(LOOKAHEAD):
            for cs in col_tiles:
                mx_scr[j, :, cs] = functools.reduce(
                    jnp.maximum, [scores(j // n_u, j % n_u, j, cs, rs) for rs in key_halves])
        lax.fori_loop(0, n_trips - 1, lambda it, c: (trip(it, False), c)[1], 0)
        trip(n_trips - 1, True)

    if mode == "diff":
        lp = lam_ref[...]
        lam = (jnp.exp(jnp.sum(lp[0:1] * lp[1:2], axis=1, keepdims=True))
               - jnp.exp(jnp.sum(lp[2:3] * lp[3:4], axis=1, keepdims=True)) + lam_init)
        gain = subln_ref[...]
        for u in range(n_u):
            o12 = acc_scr[u, :dv, :] / acc_scr[u, dv:dv + 1, :]
            o = o12[:, :tq] - lam * o12[:, tq:]
            o = o * lax.rsqrt(jnp.mean(o * o, axis=0, keepdims=True) + EPS)
            o_ref[u * dv:(u + 1) * dv, :] = ((o * gain) * (1.0 - lam_init)).astype(BF16)
    else:
        for u in range(n_u):
            m, l, acc = m_scr[u], acc_scr[u, dv:dv + 1, :], acc_scr[u, :dv, :]
            if has_sink:
                sk = sink_ref[u] * LOG2E
                m_f = jnp.maximum(m, sk)
                w = jnp.exp2(m - m_f)
                l = l * w + jnp.exp2(sk - m_f)
                acc = acc * w
            o = acc / l
            for g in range(GQA_GROUP):
                row = (u * GQA_GROUP + g) * HEAD_DIM
                o_ref[row:row + HEAD_DIM, :] = o[:, g * tq:(g + 1) * tq].astype(BF16)


def _flash_attn(q, kc, vtc, kl, vtl, *, mode, n_batch, tq, sink_rows=None, lam_params=None,
                subln=None, lam_init=0.0):
    r, dq = q.shape
    t_q = r // n_batch
    c = kc.shape[0] // n_batch
    wk = kc.shape[1]
    n_grp = wk // MXU_DIM
    q_cols = dq // n_grp
    units, dv = (_GQA_UNITS, HEAD_DIM) if mode == "gqa" else (_DIFF_UNITS, 2 * HEAD_DIM)
    cols = len(units[0][2]) * tq
    has_lat = kl is not None
    nq = t_q // tq
    in_specs = [
        pl.BlockSpec((tq, q_cols), lambda b, g, i: (b * nq + i, g)),
        pl.BlockSpec((c, MXU_DIM), lambda b, g, i: (b, g)),
        pl.BlockSpec((None, MXU_DIM, c), lambda b, g, i: (b, g, 0)),
    ]
    args = [q, kc, vtc]
    n_kb, tk = 0, 0
    scratch = [
        pltpu.VMEM((len(units), LANES, cols), BF16),
        pltpu.VMEM((len(units), 1, cols), F32),
        pltpu.VMEM((len(units), dv + SUM_ROWS, cols), F32),
    ]
    if has_lat:
        t = kl.shape[0] // n_batch
        n_kb, tk = vtl.shape[1], vtl.shape[3]
        in_specs += [
            pl.BlockSpec((t, MXU_DIM), lambda b, g, i: (b, g)),
            pl.BlockSpec((None, n_kb, MXU_DIM, tk), lambda b, g, i: (b, 0, g, 0)),
        ]
        args += [kl, vtl]
        assert PIPE_SLOTS % len(units) == 0 and n_kb % (PIPE_SLOTS // len(units)) == 0
        scratch += [pltpu.VMEM((SCORE_BUFS, tk, cols), F32),
                    pltpu.VMEM((SCORE_BUFS, 1, cols), F32)]
    if mode == "diff":
        in_specs += [
            pl.BlockSpec((4, HEAD_DIM), lambda b, g, i: (0, 0)),
            pl.BlockSpec((dv, 1), lambda b, g, i: (0, 0)),
        ]
        args += [lam_params, subln.reshape(dv, 1)]
    elif sink_rows is not None:
        in_specs.append(pl.BlockSpec((N_KV_HEADS, 1, cols), lambda b, g, i: (0, 0, 0)))
        args.append(sink_rows)
    temps = c * cols * (4 + 4 + 2) + 4 * max(tk, c) * MXU_DIM * (4 + 4 + 2)
    return _pallas(
        functools.partial(_flash_kernel, units=units, dv=dv, tq=tq, tk=tk, n_kb=n_kb, mode=mode,
                          has_sink=sink_rows is not None, lam_init=lam_init),
        name=mode + "_attn", grid=(n_batch, n_grp, nq),
        sem=("parallel", "parallel", "arbitrary"), in_specs=in_specs, args=args,
        out_specs=pl.BlockSpec((None, q_cols, tq), lambda b, g, i: (b, g, i)),
        out_shape=jax.ShapeDtypeStruct((n_batch, dq, t_q), BF16),
        scratch=scratch, temps=temps)


def _window_kernel(q_ref, kc_ref, vtc_ref, kp_ref, kcur_ref, kn_ref, vtp_ref, vtcur_ref,
                   vtn_ref, sink_ref, o_ref, *, tq):
    qi = pl.program_id(1)
    nq = pl.num_programs(1)
    n_lat = tq + 2 * WINDOW
    rr = lax.broadcasted_iota(jnp.int32, (n_lat, 1), 0)
    cc = lax.broadcasted_iota(jnp.int32, (1, tq), 1)
    valid = jnp.abs(rr - WINDOW - cc) <= WINDOW
    valid &= (rr >= WINDOW) | (qi > 0)
    valid &= (rr < tq + WINDOW) | (qi < nq - 1)
    bias = jnp.where(valid, 0.0, NEG_INF)

    k_lat = [jnp.concatenate([r[:, h * LANES:(h + 1) * LANES] for r in (kp_ref, kcur_ref, kn_ref)],
                             axis=0) for h in range(2)]

    def pv(vt, p):
        ones = jnp.ones((SUM_ROWS, vt.shape[1]), BF16)
        return jnp.dot(jnp.concatenate([vt, ones], axis=0), p, preferred_element_type=F32)

    def scores(kvh, g):
        half = kvh // 2
        qt = _gqa_q_tile(q_ref, kvh, g).astype(F32).T.astype(BF16)
        s_ctx = jnp.dot(kc_ref[:, half * LANES:(half + 1) * LANES], qt,
                        preferred_element_type=F32)
        s_lat = jnp.dot(k_lat[half], qt, preferred_element_type=F32) + bias
        sk = sink_ref[kvh, :, g * tq:(g + 1) * tq] * LOG2E
        m = jnp.maximum(jnp.maximum(jnp.max(s_ctx, axis=0, keepdims=True),
                                    jnp.max(s_lat, axis=0, keepdims=True)), sk)
        return s_ctx, s_lat, m, jnp.exp2(sk - m)

    def probs(s_ctx, s_lat, m, p_sink):
        return jnp.exp2(s_ctx - m).astype(BF16), jnp.exp2(s_lat - m).astype(BF16), p_sink

    def output(kvh, g, p_ctx, p_lat, p_sink):
        hs = slice(kvh * HEAD_DIM, (kvh + 1) * HEAD_DIM)
        vt_lat = jnp.concatenate([vtp_ref[hs, :], vtcur_ref[hs, :], vtn_ref[hs, :]], axis=1)
        acc = pv(vtc_ref[hs, :], p_ctx) + pv(vt_lat, p_lat)
        l = acc[HEAD_DIM:HEAD_DIM + 1, :] + p_sink
        row = (kvh * GQA_GROUP + g) * HEAD_DIM
        o_ref[row:row + HEAD_DIM, :] = (acc[:HEAD_DIM, :] / l).astype(BF16)

    tiles = [(kvh, g) for kvh in range(N_KV_HEADS) for g in range(GQA_GROUP)]
    scored, pending = scores(*tiles[0]), None
    for t, tile in enumerate(tiles):
        ahead = scores(*tiles[t + 1]) if t + 1 < len(tiles) else None
        fresh = (*tile, *probs(*scored))
        if pending is not None:
            output(*pending)
        scored, pending = ahead, fresh
    output(*pending)


def _window_attn(q, kc, vtc, kl, vtl, sink_rows, *, n_batch, tq):
    r, dq = q.shape
    t = r // n_batch
    c = kc.shape[0] // n_batch
    wk = kc.shape[1]
    tkv = vtl.shape[3]
    nq = t // tq
    rb = tq // WINDOW
    nwb = t // WINDOW
    wpb, qpb = tkv // WINDOW, tkv // tq
    prev = lambda b, i: jnp.maximum(i * rb - 1, 0)
    nxt = lambda b, i: jnp.minimum((i + 1) * rb, nwb - 1)
    in_specs = [
        pl.BlockSpec((tq, dq), lambda b, i: (b * nq + i, 0)),
        pl.BlockSpec((c, wk), lambda b, i: (b, 0)),
        pl.BlockSpec((None, wk, c), lambda b, i: (b, 0, 0)),
        pl.BlockSpec((WINDOW, wk), lambda b, i: (b * nwb + prev(b, i), 0)),
        pl.BlockSpec((tq, wk), lambda b, i: (b * nq + i, 0)),
        pl.BlockSpec((WINDOW, wk), lambda b, i: (b * nwb + nxt(b, i), 0)),
        pl.BlockSpec((None, None, wk, WINDOW),
                     lambda b, i: (b, prev(b, i) // wpb, 0, prev(b, i) % wpb)),
        pl.BlockSpec((None, None, wk, tq), lambda b, i: (b, i // qpb, 0, i % qpb)),
        pl.BlockSpec((None, None, wk, WINDOW),
                     lambda b, i: (b, nxt(b, i) // wpb, 0, nxt(b, i) % wpb)),
        pl.BlockSpec((N_KV_HEADS, 1, GQA_GROUP * tq), lambda b, i: (0, 0, 0)),
    ]
    n_keys = c + tq + 2 * WINDOW
    temps = n_keys * GQA_GROUP * tq * (4 + 4 + 2) + GQA_GROUP * tq * LANES * 2
    return _pallas(
        functools.partial(_window_kernel, tq=tq), name="window_attn", grid=(n_batch, nq),
        sem=("parallel", "parallel"), in_specs=in_specs,
        args=[q, kc, vtc, kl, kl, kl, vtl, vtl, vtl, sink_rows],
        out_specs=pl.BlockSpec((None, dq, tq), lambda b, i: (b, 0, i)),
        out_shape=jax.ShapeDtypeStruct((n_batch, dq, t), BF16), temps=temps)


def _rope_tables(n_tokens):
    rows = n_tokens // GRID_W
    row = jnp.repeat(jnp.arange(rows), GRID_W)
    col = jnp.tile(jnp.arange(GRID_W), rows)
    inv_freq = ROPE_BASE ** (-jnp.arange(N_FREQ, dtype=F32) / N_FREQ)
    ang = jnp.stack([row, col], axis=-1).astype(F32)[:, :, None] * inv_freq
    cos, sin = jnp.cos(ang), jnp.sin(ang)
    cos_h = jnp.concatenate([cos, cos], axis=-1).reshape(n_tokens, HEAD_DIM)
    sin_h = jnp.concatenate([-sin, sin], axis=-1).reshape(n_tokens, HEAD_DIM)
    return jnp.tile(cos_h, (1, 2)), jnp.tile(sin_h, (1, 2))


def _gqa_weights(w_qkv):
    d = w_qkv.shape[0]
    n_q = N_KV_HEADS * GQA_GROUP * HEAD_DIM
    n_kv = N_KV_HEADS * HEAD_DIM
    wq = w_qkv[:, :n_q].reshape(d, N_KV_HEADS // 2, 2, GQA_GROUP, HEAD_DIM)
    wq = wq.transpose(0, 1, 3, 2, 4).reshape(d, n_q)
    wk = w_qkv[:, n_q:n_q + n_kv]
    wvt = w_qkv[:, n_q + n_kv:].T
    return wq.astype(BF16), wk.astype(BF16), wvt.astype(BF16)


def _diff_weights(w_qkv):
    n = w_qkv.shape[1] // 3
    return (w_qkv[:, :n].astype(BF16), w_qkv[:, n:2 * n].astype(BF16),
            w_qkv[:, 2 * n:].T.astype(BF16))


def _lane_gain(g):
    return jnp.tile(g.reshape(1, HEAD_DIM), (1, LANES // HEAD_DIM))


def _sink_rows(sink, tq):
    return jnp.repeat(sink.reshape(N_KV_HEADS, 1, GQA_GROUP, 1), tq, axis=3).reshape(
        N_KV_HEADS, 1, GQA_GROUP * tq)


def kernel(x, c, ctx, c_ctx, norm_g, w_ada, b_ada, w_ffn_in, w_ffn_out, w_o, w_qkv_a, qk_norm_a,
           w_qkv_b, qk_norm_b, sink_b, w_qkv_c, qk_norm_c, diff_lambda, diff_subln):
    n_b, t, d = x.shape
    n_c = ctx.shape[1]
    depth = w_ada.shape[0]
    tm = 512
    tm_ffn = 1024
    tq, tq_diff = 1024, 2048
    tq_win = 256
    tq_c = n_c
    tkv = 512
    assert n_b + 1 <= SUBLANES and n_c % LANES == 0
    assert all(t % tile == 0 for tile in (tm, tm_ffn, tq, tq_diff, tq_win, tkv, GRID_W))

    c8 = jnp.zeros((SUBLANES, d), F32).at[:n_b].set(c).at[n_b].set(c_ctx)
    mod5 = _ada_all(c8, w_ada, b_ada).reshape(depth, SUBLANES, 3, 3, d)

    rope_tabs = _rope_tables(t)
    eye = jnp.arange(MXU_DIM) // HEAD_DIM
    bd = (eye[:, None] == eye[None, :]).astype(BF16)
    w_in = w_ffn_in.astype(BF16)
    w_out = w_ffn_out.astype(BF16)
    wo = w_o.astype(BF16)
    norm_g4 = norm_g.reshape(depth, 3, 1, d)

    xs = x.reshape(n_b * t, d)
    cs = ctx.reshape(n_b * n_c, d)
    lat = dict(tm=tm, rows_per_batch=t, ctx_row=None)
    lat_ffn = dict(tm=tm_ffn, rows_per_batch=t, ctx_row=None)
    cx = dict(tm=n_c, rows_per_batch=n_c, ctx_row=n_b)

    for i in range(depth):
        last = i == depth - 1
        kind, j = i % 3, i // 3
        xs = _ffn(xs, mod5, norm_g4, w_in, w_out, i, 0, **lat_ffn)
        cs = _ffn(cs, mod5, norm_g4, w_in, w_out, i, 0, **cx)
        if kind == 2:
            wq, wk, wvt = _diff_weights(w_qkv_c[j])
            qk_g = qk_norm_c[j]
        else:
            wq, wk, wvt = _gqa_weights((w_qkv_a, w_qkv_b)[kind][j])
            qk_g = (qk_norm_a, qk_norm_b)[kind][j]
        gq, gk = _lane_gain(qk_g[0]), _lane_gain(qk_g[1])
        q, kl, vtl = _proj(xs, mod5, norm_g4, i, wq, wk, wvt, gq, gk, bd, rope_tabs, tkv=tkv,
                           **lat)
        qc, kc, vtc = _proj(cs, mod5, norm_g4, i, wq, wk, wvt, gq, gk, bd, None, tkv=n_c, **cx)
        vtc = vtc.reshape(n_b, vtc.shape[2], n_c)
        if kind == 0:
            ot = _flash_attn(q, kc, vtc, kl, vtl, mode="gqa", n_batch=n_b, tq=tq)
            if not last:
                otc = _flash_attn(qc, kc, vtc, None, None, mode="gqa", n_batch=n_b, tq=tq_c)
        elif kind == 1:
            ot = _window_attn(q, kc, vtc, kl, vtl, _sink_rows(sink_b[j], tq_win), n_batch=n_b,
                              tq=tq_win)
            if not last:
                otc = _flash_attn(qc, kc, vtc, None, None, mode="gqa", n_batch=n_b, tq=tq_c,
                                  sink_rows=_sink_rows(sink_b[j], tq_c))
        else:
            extra = dict(lam_params=diff_lambda[j], subln=diff_subln[j],
                         lam_init=0.8 - 0.6 * math.exp(-0.3 * i))
            ot = _flash_attn(q, kc, vtc, kl, vtl, mode="diff", n_batch=n_b, tq=tq_diff, **extra)
            if not last:
                otc = _flash_attn(qc, kc, vtc, None, None, mode="diff", n_batch=n_b, tq=tq_c,
                                  **extra)
        xs = _ffn(xs, mod5, norm_g4, w_in, w_out, i, 1, pre=(ot, wo), **lat_ffn)
        if not last:
            cs = _ffn(cs, mod5, norm_g4, w_in, w_out, i, 1, pre=(otc, wo), **cx)
    return xs.reshape(n_b, t, d)
```

```python
import functools
import math

import jax
import jax.numpy as jnp
from jax import lax
from jax.experimental import pallas as pl
from jax.experimental.pallas import tpu as pltpu

F32 = jnp.float32
BF16 = jnp.bfloat16

HEAD_DIM = 64
N_KV_HEADS = 4
GQA_GROUP = 4
GRID_W = 64
WINDOW = 128
ROPE_BASE = 10000.0
N_FREQ = HEAD_DIM // 4
EPS = 1e-6
NEG_INF = -1e30
LOG2E = math.log2(math.e)
Q_SCALE = HEAD_DIM ** -0.5 * LOG2E
FFN_RESIDUAL = 0.5
LANES = 128
MXU_DIM = 256
SUM_ROWS = 16
PIPE_SLOTS = 4
LOOKAHEAD = 1
SCORE_BUFS = LOOKAHEAD + 1
V7X_VMEM_BYTES = 64 * 1024 * 1024
SUBLANES = 8


def _tile_bytes(shape, dtype):
    item = jnp.dtype(dtype).itemsize
    dims = [1 if s is None else s for s in shape]
    dims[-1] = pl.cdiv(dims[-1], LANES) * LANES
    if len(dims) > 1:
        rows = SUBLANES * 4 // item
        dims[-2] = pl.cdiv(dims[-2], rows) * rows
    return math.prod(dims) * item


def _pallas(body, *, name, grid, sem, in_specs, args, out_specs, out_shape, scratch=(), temps=0):
    outs = out_shape if isinstance(out_shape, (list, tuple)) else [out_shape]
    o_specs = out_specs if isinstance(out_specs, (list, tuple)) else [out_specs]
    need = temps + sum(_tile_bytes(s.shape, s.dtype) for s in scratch)
    for spec, a in list(zip(in_specs, args)) + list(zip(o_specs, outs)):
        n_buf = spec.pipeline_mode.buffer_count if spec.pipeline_mode is not None else 2
        need += n_buf * _tile_bytes(spec.block_shape, a.dtype)
    return pl.pallas_call(
        body, grid=grid, in_specs=in_specs, out_specs=out_specs, out_shape=out_shape,
        scratch_shapes=list(scratch), name=name,
        compiler_params=pltpu.CompilerParams(dimension_semantics=sem,
                                             vmem_limit_bytes=min(need, V7X_VMEM_BYTES)),
    )(*args)


def _silu(a):
    return a * jax.nn.sigmoid(a)


def _norm_mod(x, g, shift, scale):
    y = x * lax.rsqrt(jnp.mean(x * x, axis=-1, keepdims=True) + EPS)
    return (y * g) * (1.0 + scale) + shift


def _ada_kernel(c_ref, w_ref, b_ref, o_ref):
    s = _silu(c_ref[...])
    o_ref[...] = jnp.dot(s, w_ref[...], preferred_element_type=F32) + b_ref[...]


def _ada_all(c8, w_ada, b_ada):
    depth, d, nd = w_ada.shape
    rows = c8.shape[0]
    tn = nd // 8
    return _pallas(
        _ada_kernel, name="adaln", grid=(depth, nd // tn), sem=("parallel", "parallel"),
        in_specs=[
            pl.BlockSpec((rows, d), lambda i, j: (0, 0)),
            pl.BlockSpec((None, d, tn), lambda i, j: (i, 0, j)),
            pl.BlockSpec((None, 1, tn), lambda i, j: (i, 0, j)),
        ],
        args=[c8, w_ada, b_ada.reshape(depth, 1, nd)],
        out_specs=pl.BlockSpec((None, rows, tn), lambda i, j: (i, 0, j)),
        out_shape=jax.ShapeDtypeStruct((depth, rows, nd), F32),
        temps=_tile_bytes((rows, d), F32) + 2 * _tile_bytes((rows, tn), F32))


def _ffn_kernel(*refs, pre):
    if pre:
        x_ref, ot_ref, wo_ref, modp_ref, mod_ref, g_ref, wa_ref, wu_ref, wout_ref, o_ref = refs
    else:
        x_ref, mod_ref, g_ref, wa_ref, wu_ref, wout_ref, o_ref = refs
    x = x_ref[...]
    if pre:
        x = x + modp_ref[2:3, :] * pl.dot(ot_ref[...], wo_ref[...], trans_a=True)
    h = _norm_mod(x, g_ref[...], mod_ref[0:1, :], mod_ref[1:2, :]).astype(BF16)
    a = jnp.dot(h, wa_ref[...], preferred_element_type=F32)
    u = jnp.dot(h, wu_ref[...], preferred_element_type=F32)
    act = (_silu(a) * u).astype(BF16)
    y = jnp.dot(act, wout_ref[...], preferred_element_type=F32)
    o_ref[...] = x + (FFN_RESIDUAL * mod_ref[2:3, :]) * y


def _ffn(x2, mod5, norm_g4, w_in, w_out, layer, half, *, tm, rows_per_batch, ctx_row, pre=None):
    r, d = x2.shape
    dff = w_out.shape[2]
    tpb = rows_per_batch // tm
    k = 2 * half
    resident = dict(pipeline_mode=pl.Buffered(1))

    def bidx(i):
        return ctx_row if ctx_row is not None else i // tpb

    def mod_spec(kk):
        return pl.BlockSpec((None, None, None, 3, d), lambda i: (layer, bidx(i), kk, 0, 0))

    in_specs = [pl.BlockSpec((tm, d), lambda i: (i, 0))]
    args = [x2]
    if pre is not None:
        ot, wo = pre
        in_specs += [
            pl.BlockSpec((None, d, tm), lambda i: (i // tpb, 0, i % tpb)),
            pl.BlockSpec((None, d, d), lambda i: (layer, 0, 0), **resident),
            mod_spec(1),
        ]
        args += [ot, wo, mod5]
    in_specs += [
        mod_spec(k),
        pl.BlockSpec((None, None, 1, d), lambda i: (layer, k, 0, 0)),
        pl.BlockSpec((None, None, d, dff), lambda i: (layer, half, 0, 0), **resident),
        pl.BlockSpec((None, None, d, dff), lambda i: (layer, half, 0, 1), **resident),
        pl.BlockSpec((None, None, dff, d), lambda i: (layer, half, 0, 0), **resident),
    ]
    args += [mod5, norm_g4, w_in, w_in, w_out]
    temps = tm * dff * (4 + 4 + 2) + tm * d * (4 + 4 + 2)
    return _pallas(
        functools.partial(_ffn_kernel, pre=pre is not None),
        name="ffn_pre" if pre is not None else "ffn", grid=(r // tm,), sem=("parallel",),
        in_specs=in_specs, args=args,
        out_specs=pl.BlockSpec((tm, d), lambda i: (i, 0)),
        out_shape=jax.ShapeDtypeStruct((r, d), F32), temps=temps)


def _headnorm_rope(z, gain, cos, sin, bd, rope):
    tm, w = z.shape
    lane = lax.broadcasted_iota(jnp.int32, (1, LANES), 1)
    first_half = (lane & (2 * N_FREQ - 1)) < N_FREQ
    outs = []
    for j in range(w // MXU_DIM):
        zj = z[:, j * MXU_DIM:(j + 1) * MXU_DIM]
        sq = zj * zj
        hi = sq.astype(BF16)
        lo = (sq - hi.astype(F32)).astype(BF16)
        ss = (jnp.dot(hi, bd, preferred_element_type=F32)
              + jnp.dot(lo, bd, preferred_element_type=F32))
        zn = zj * lax.rsqrt(ss * (1.0 / HEAD_DIM) + EPS)
        for half in range(MXU_DIM // LANES):
            t = zn[:, half * LANES:(half + 1) * LANES] * gain
            if rope:
                partner = jnp.where(first_half, pltpu.roll(t, LANES - N_FREQ, 1),
                                    pltpu.roll(t, N_FREQ, 1))
                t = t * cos + partner * sin
            outs.append(t)
    return jnp.concatenate(outs, axis=1)


def _proj_kernel(*refs, rope):
    if rope:
        (x_ref, mod_ref, g_ref, wq_ref, wk_ref, wvt_ref, gq_ref, gk_ref, bd_ref, cos_ref,
         sin_ref, q_ref, k_ref, vt_ref) = refs
        cos, sin = cos_ref[...], sin_ref[...]
    else:
        (x_ref, mod_ref, g_ref, wq_ref, wk_ref, wvt_ref, gq_ref, gk_ref, bd_ref,
         q_ref, k_ref, vt_ref) = refs
        cos = sin = None
    h = _norm_mod(x_ref[...], g_ref[...], mod_ref[0:1, :], mod_ref[1:2, :]).astype(BF16)
    bd = bd_ref[...]
    q = jnp.dot(h, wq_ref[...], preferred_element_type=F32)
    q_ref[...] = _headnorm_rope(q, gq_ref[...] * Q_SCALE, cos, sin, bd, rope).astype(BF16)
    k = jnp.dot(h, wk_ref[...], preferred_element_type=F32)
    k_ref[...] = _headnorm_rope(k, gk_ref[...], cos, sin, bd, rope).astype(BF16)
    vt = pl.dot(wvt_ref[...], h, trans_b=True).astype(BF16)
    tkv = vt_ref.shape[2]
    for c in range(vt_ref.shape[0]):
        vt_ref[c] = vt[:, c * tkv:(c + 1) * tkv]


def _proj(x2, mod5, norm_g4, layer, wq, wk, wvt, gq, gk, bd, rope_tabs, *, tm, rows_per_batch,
          ctx_row, tkv):
    r, d = x2.shape
    wq_n, wk_n, wv_n = wq.shape[1], wk.shape[1], wvt.shape[0]
    tpb = rows_per_batch // tm
    kpt = tm // tkv
    nb = r // rows_per_batch
    rope = rope_tabs is not None

    def bidx(i):
        return ctx_row if ctx_row is not None else i // tpb

    const = lambda i: (0, 0)
    in_specs = [
        pl.BlockSpec((tm, d), lambda i: (i, 0)),
        pl.BlockSpec((None, None, None, 3, d), lambda i: (layer, bidx(i), 1, 0, 0)),
        pl.BlockSpec((None, None, 1, d), lambda i: (layer, 1, 0, 0)),
        pl.BlockSpec((d, wq_n), const),
        pl.BlockSpec((d, wk_n), const),
        pl.BlockSpec((wv_n, d), const),
        pl.BlockSpec((1, LANES), const),
        pl.BlockSpec((1, LANES), const),
        pl.BlockSpec((MXU_DIM, MXU_DIM), const),
    ]
    args = [x2, mod5, norm_g4, wq, wk, wvt, gq, gk, bd]
    if rope:
        in_specs += [pl.BlockSpec((tm, LANES), lambda i: (i % tpb, 0))] * 2
        args += list(rope_tabs)
    temps = tm * d * (4 + 2) + 2 * tm * (wq_n + wk_n + wv_n) * 4
    return _pallas(
        functools.partial(_proj_kernel, rope=rope), name="qkv_proj", grid=(r // tm,),
        sem=("parallel",), in_specs=in_specs, args=args,
        out_specs=[
            pl.BlockSpec((tm, wq_n), lambda i: (i, 0)),
            pl.BlockSpec((tm, wk_n), lambda i: (i, 0)),
            pl.BlockSpec((None, kpt, wv_n, tkv), lambda i: (i // tpb, i % tpb, 0, 0)),
        ],
        out_shape=[
            jax.ShapeDtypeStruct((r, wq_n), BF16),
            jax.ShapeDtypeStruct((r, wk_n), BF16),
            jax.ShapeDtypeStruct((nb, tpb * kpt, wv_n, tkv), BF16),
        ],
        temps=temps)


def _slot_mask(slot):
    lane = lax.broadcasted_iota(jnp.int32, (1, LANES), 1)
    return (lane >= slot * HEAD_DIM) & (lane < (slot + 1) * HEAD_DIM)


def _gqa_q_tile(q_ref, kvh, g):
    j = g + GQA_GROUP * (kvh // 2)
    tile = q_ref[:, j * LANES:(j + 1) * LANES]
    return jnp.where(_slot_mask(kvh % 2), tile, jnp.zeros_like(tile))


_GQA_UNITS = tuple(
    (kvh // 2, kvh * HEAD_DIM,
     tuple((g + GQA_GROUP * (kvh // 2), kvh % 2) for g in range(GQA_GROUP)))
    for kvh in range(N_KV_HEADS))
_DIFF_UNITS = tuple((h, h * 2 * HEAD_DIM, ((h, 0), (h, 1))) for h in range(2))


def _flash_kernel(*refs, units, dv, tq, tk, n_kb, mode, has_sink, lam_init):
    refs = list(refs)
    q_ref, kc_ref, vtc_ref = refs[:3]
    pos = 3
    has_lat = n_kb > 0
    if has_lat:
        kl_ref, vtl_ref = refs[pos:pos + 2]
        pos += 2
    if mode == "diff":
        lam_ref, subln_ref = refs[pos:pos + 2]
        pos += 2
    elif has_sink:
        sink_ref = refs[pos]
        pos += 1
    o_ref, qs_scr, m_scr, acc_scr = refs[pos:pos + 4]
    if has_lat:
        s_scr, mx_scr = refs[pos + 4:]
    n_u = len(units)

    def pv(vt, p):
        ones = jnp.ones((SUM_ROWS, vt.shape[1]), BF16)
        return jnp.dot(jnp.concatenate([vt, ones], axis=0), p.astype(BF16),
                       preferred_element_type=F32)

    for u, (_, _, q_tiles) in enumerate(units):
        for t, (j, slot) in enumerate(q_tiles):
            tile = q_ref[:, j * LANES:(j + 1) * LANES]
            tile = jnp.where(_slot_mask(slot), tile, jnp.zeros_like(tile))
            qs_scr[u, :, t * tq:(t + 1) * tq] = tile.T

    for u, (half, v0, _) in enumerate(units):
        s = jnp.dot(kc_ref[:, half * LANES:(half + 1) * LANES], qs_scr[u],
                    preferred_element_type=F32)
        m = jnp.max(s, axis=0, keepdims=True)
        p = jnp.exp2(s - m)
        m_scr[u] = m
        acc_scr[u] = pv(vtc_ref[v0:v0 + dv, :], p)

    if has_lat:
        col_tiles = [slice(c0, c0 + MXU_DIM) for c0 in range(0, s_scr.shape[2], MXU_DIM)]

        key_halves = [slice(r0, r0 + MXU_DIM) for r0 in range(0, tk, MXU_DIM)]

        def scores(kb, u, slot, cs, rs):
            half = units[u][0]
            row0 = pl.multiple_of(kb * tk, tk) + rs.start
            k = kl_ref[pl.ds(row0, MXU_DIM), half * LANES:(half + 1) * LANES]
            s = jnp.dot(k, qs_scr[u, :, cs], preferred_element_type=F32)
            s_scr[slot, rs, cs] = s
            return jnp.max(s, axis=0, keepdims=True)

        def probs(u, slot, cs):
            m_prev = m_scr[u, :, cs]
            m_new = jnp.maximum(m_prev, mx_scr[slot, :, cs])
            m_scr[u, :, cs] = m_new
            return (jnp.exp2(s_scr[slot, :, cs] - m_new).astype(BF16),
                    jnp.exp2(m_prev - m_new))

        def pv_part(kb, u, p, rs):
            v0 = units[u][1]
            return pv(vtl_ref[kb, v0:v0 + dv, rs], p[rs, :])

        bpt = PIPE_SLOTS // n_u
        n_trips = n_kb // bpt

        def trip(it, last):
            pending = None
            for j in range(PIPE_SLOTS):
                ahead = j + LOOKAHEAD
                if ahead < PIPE_SLOTS:
                    sc = (it * bpt + ahead // n_u, ahead % n_u, ahead % SCORE_BUFS)
                elif not last:
                    a2 = ahead - PIPE_SLOTS
                    sc = ((it + 1) * bpt + a2 // n_u, a2 % n_u, a2 % SCORE_BUFS)
                else:
                    sc = None
                for cs in col_tiles:
                    fresh = (it * bpt + j // n_u, j % n_u, cs,
                             *probs(j % n_u, j % SCORE_BUFS, cs))
                    mx, parts = None, []
                    for rs in key_halves:
                        if sc is not None:
                            part = scores(*sc, cs, rs)
                            mx = part if mx is None else jnp.maximum(mx, part)
                        if pending is not None:
                            parts.append(pv_part(pending[0], pending[1], pending[3], rs))
                    if sc is not None:
                        mx_scr[sc[2], :, cs] = mx
                    if pending is not None:
                        _, u_p, cs_p, _, alpha_p = pending
                        acc_scr[u_p, :, cs_p] = alpha_p * acc_scr[u_p, :, cs_p] + sum(parts)
                    pending = fresh
            kb_p, u_p, cs_p, p, alpha_p = pending
            acc_scr[u_p, :, cs_p] = alpha_p * acc_scr[u_p, :, cs_p] + sum(
                pv_part(kb_p, u_p, p, rs) for rs in key_halves)

        for j in range(LOOKAHEAD):
            for cs in col_tiles:
                mx_scr[j, :, cs] = functools.reduce(
                    jnp.maximum, [scores(j // n_u, j % n_u, j, cs, rs) for rs in key_halves])
        lax.fori_loop(0, n_trips - 1, lambda it, c: (trip(it, False), c)[1], 0)
        trip(n_trips - 1, True)

    if mode == "diff":
        lp = lam_ref[...]
        lam = (jnp.exp(jnp.sum(lp[0:1] * lp[1:2], axis=1, keepdims=True))
               - jnp.exp(jnp.sum(lp[2:3] * lp[3:4], axis=1, keepdims=True)) + lam_init)
        gain = subln_ref[...]
        for u in range(n_u):
            o12 = acc_scr[u, :dv, :] / acc_scr[u, dv:dv + 1, :]
            o = o12[:, :tq] - lam * o12[:, tq:]
            o = o * lax.rsqrt(jnp.mean(o * o, axis=0, keepdims=True) + EPS)
            o_ref[u * dv:(u + 1) * dv, :] = ((o * gain) * (1.0 - lam_init)).astype(BF16)
    else:
        for u in range(n_u):
            m, l, acc = m_scr[u], acc_scr[u, dv:dv + 1, :], acc_scr[u, :dv, :]
            if has_sink:
                sk = sink_ref[u] * LOG2E
                m_f = jnp.maximum(m, sk)
                w = jnp.exp2(m - m_f)
                l = l * w + jnp.exp2(sk - m_f)
                acc = acc * w
            o = acc / l
            for g in range(GQA_GROUP):
                row = (u * GQA_GROUP + g) * HEAD_DIM
                o_ref[row:row + HEAD_DIM, :] = o[:, g * tq:(g + 1) * tq].astype(BF16)


def _flash_attn(q, kc, vtc, kl, vtl, *, mode, n_batch, tq, sink_rows=None, lam_params=None,
                subln=None, lam_init=0.0):
    r, dq = q.shape
    t_q = r // n_batch
    c = kc.shape[0] // n_batch
    wk = kc.shape[1]
    n_grp = wk // MXU_DIM
    q_cols = dq // n_grp
    units, dv = (_GQA_UNITS, HEAD_DIM) if mode == "gqa" else (_DIFF_UNITS, 2 * HEAD_DIM)
    cols = len(units[0][2]) * tq
    has_lat = kl is not None
    nq = t_q // tq
    in_specs = [
        pl.BlockSpec((tq, q_cols), lambda b, g, i: (b * nq + i, g)),
        pl.BlockSpec((c, MXU_DIM), lambda b, g, i: (b, g)),
        pl.BlockSpec((None, MXU_DIM, c), lambda b, g, i: (b, g, 0)),
    ]
    args = [q, kc, vtc]
    n_kb, tk = 0, 0
    scratch = [
        pltpu.VMEM((len(units), LANES, cols), BF16),
        pltpu.VMEM((len(units), 1, cols), F32),
        pltpu.VMEM((len(units), dv + SUM_ROWS, cols), F32),
    ]
    if has_lat:
        t = kl.shape[0] // n_batch
        n_kb, tk = vtl.shape[1], vtl.shape[3]
        in_specs += [
            pl.BlockSpec((t, MXU_DIM), lambda b, g, i: (b, g)),
            pl.BlockSpec((None, n_kb, MXU_DIM, tk), lambda b, g, i: (b, 0, g, 0)),
        ]
        args += [kl, vtl]
        assert PIPE_SLOTS % len(units) == 0 and n_kb % (PIPE_SLOTS // len(units)) == 0
        scratch += [pltpu.VMEM((SCORE_BUFS, tk, cols), F32),
                    pltpu.VMEM((SCORE_BUFS, 1, cols), F32)]
    if mode == "diff":
        in_specs += [
            pl.BlockSpec((4, HEAD_DIM), lambda b, g, i: (0, 0)),
            pl.BlockSpec((dv, 1), lambda b, g, i: (0, 0)),
        ]
        args += [lam_params, subln.reshape(dv, 1)]
    elif sink_rows is not None:
        in_specs.append(pl.BlockSpec((N_KV_HEADS, 1, cols), lambda b, g, i: (0, 0, 0)))
        args.append(sink_rows)
    temps = c * cols * (4 + 4 + 2) + 4 * max(tk, c) * MXU_DIM * (4 + 4 + 2)
    return _pallas(
        functools.partial(_flash_kernel, units=units, dv=dv, tq=tq, tk=tk, n_kb=n_kb, mode=mode,
                          has_sink=sink_rows is not None, lam_init=lam_init),
        name=mode + "_attn", grid=(n_batch, n_grp, nq),
        sem=("parallel", "parallel", "arbitrary"), in_specs=in_specs, args=args,
        out_specs=pl.BlockSpec((None, q_cols, tq), lambda b, g, i: (b, g, i)),
        out_shape=jax.ShapeDtypeStruct((n_batch, dq, t_q), BF16),
        scratch=scratch, temps=temps)


def _window_kernel(q_ref, kc_ref, vtc_ref, kp_ref, kcur_ref, kn_ref, vtp_ref, vtcur_ref,
                   vtn_ref, sink_ref, o_ref, *, tq):
    qi = pl.program_id(1)
    nq = pl.num_programs(1)
    n_lat = tq + 2 * WINDOW
    rr = lax.broadcasted_iota(jnp.int32, (n_lat, 1), 0)
    cc = lax.broadcasted_iota(jnp.int32, (1, tq), 1)
    valid = jnp.abs(rr - WINDOW - cc) <= WINDOW
    valid &= (rr >= WINDOW) | (qi > 0)
    valid &= (rr < tq + WINDOW) | (qi < nq - 1)
    bias = jnp.where(valid, 0.0, NEG_INF)

    k_lat = [jnp.concatenate([r[:, h * LANES:(h + 1) * LANES] for r in (kp_ref, kcur_ref, kn_ref)],
                             axis=0) for h in range(2)]

    def pv(vt, p):
        ones = jnp.ones((SUM_ROWS, vt.shape[1]), BF16)
        return jnp.dot(jnp.concatenate([vt, ones], axis=0), p, preferred_element_type=F32)

    def scores(kvh, g):
        half = kvh // 2
        qt = _gqa_q_tile(q_ref, kvh, g).T
        s_ctx = jnp.dot(kc_ref[:, half * LANES:(half + 1) * LANES], qt,
                        preferred_element_type=F32)
        s_lat = jnp.dot(k_lat[half], qt, preferred_element_type=F32) + bias
        sk = sink_ref[kvh, :, g * tq:(g + 1) * tq] * LOG2E
        m = jnp.maximum(jnp.maximum(jnp.max(s_ctx, axis=0, keepdims=True),
                                    jnp.max(s_lat, axis=0, keepdims=True)), sk)
        return s_ctx, s_lat, m, jnp.exp2(sk - m)

    def probs(s_ctx, s_lat, m, p_sink):
        return jnp.exp2(s_ctx - m).astype(BF16), jnp.exp2(s_lat - m).astype(BF16), p_sink

    def output(kvh, g, p_ctx, p_lat, p_sink):
        hs = slice(kvh * HEAD_DIM, (kvh + 1) * HEAD_DIM)
        vt_lat = jnp.concatenate([vtp_ref[hs, :], vtcur_ref[hs, :], vtn_ref[hs, :]], axis=1)
        acc = pv(vtc_ref[hs, :], p_ctx) + pv(vt_lat, p_lat)
        l = acc[HEAD_DIM:HEAD_DIM + 1, :] + p_sink
        row = (kvh * GQA_GROUP + g) * HEAD_DIM
        o_ref[row:row + HEAD_DIM, :] = (acc[:HEAD_DIM, :] / l).astype(BF16)

    tiles = [(kvh, g) for kvh in range(N_KV_HEADS) for g in range(GQA_GROUP)]
    scored, pending = scores(*tiles[0]), None
    for t, tile in enumerate(tiles):
        ahead = scores(*tiles[t + 1]) if t + 1 < len(tiles) else None
        fresh = (*tile, *probs(*scored))
        if pending is not None:
            output(*pending)
        scored, pending = ahead, fresh
    output(*pending)


def _window_attn(q, kc, vtc, kl, vtl, sink_rows, *, n_batch, tq):
    r, dq = q.shape
    t = r // n_batch
    c = kc.shape[0] // n_batch
    wk = kc.shape[1]
    tkv = vtl.shape[3]
    nq = t // tq
    rb = tq // WINDOW
    nwb = t // WINDOW
    wpb, qpb = tkv // WINDOW, tkv // tq
    prev = lambda b, i: jnp.maximum(i * rb - 1, 0)
    nxt = lambda b, i: jnp.minimum((i + 1) * rb, nwb - 1)
    in_specs = [
        pl.BlockSpec((tq, dq), lambda b, i: (b * nq + i, 0)),
        pl.BlockSpec((c, wk), lambda b, i: (b, 0)),
        pl.BlockSpec((None, wk, c), lambda b, i: (b, 0, 0)),
        pl.BlockSpec((WINDOW, wk), lambda b, i: (b * nwb + prev(b, i), 0)),
        pl.BlockSpec((tq, wk), lambda b, i: (b * nq + i, 0)),
        pl.BlockSpec((WINDOW, wk), lambda b, i: (b * nwb + nxt(b, i), 0)),
        pl.BlockSpec((None, None, wk, WINDOW),
                     lambda b, i: (b, prev(b, i) // wpb, 0, prev(b, i) % wpb)),
        pl.BlockSpec((None, None, wk, tq), lambda b, i: (b, i // qpb, 0, i % qpb)),
        pl.BlockSpec((None, None, wk, WINDOW),
                     lambda b, i: (b, nxt(b, i) // wpb, 0, nxt(b, i) % wpb)),
        pl.BlockSpec((N_KV_HEADS, 1, GQA_GROUP * tq), lambda b, i: (0, 0, 0)),
    ]
    n_keys = c + tq + 2 * WINDOW
    temps = n_keys * GQA_GROUP * tq * (4 + 4 + 2) + GQA_GROUP * tq * LANES * 2
    return _pallas(
        functools.partial(_window_kernel, tq=tq), name="window_attn", grid=(n_batch, nq),
        sem=("parallel", "parallel"), in_specs=in_specs,
        args=[q, kc, vtc, kl, kl, kl, vtl, vtl, vtl, sink_rows],
        out_specs=pl.BlockSpec((None, dq, tq), lambda b, i: (b, 0, i)),
        out_shape=jax.ShapeDtypeStruct((n_batch, dq, t), BF16), temps=temps)


def _rope_tables(n_tokens):
    rows = n_tokens // GRID_W
    row = jnp.repeat(jnp.arange(rows), GRID_W)
    col = jnp.tile(jnp.arange(GRID_W), rows)
    inv_freq = ROPE_BASE ** (-jnp.arange(N_FREQ, dtype=F32) / N_FREQ)
    ang = jnp.stack([row, col], axis=-1).astype(F32)[:, :, None] * inv_freq
    cos, sin = jnp.cos(ang), jnp.sin(ang)
    cos_h = jnp.concatenate([cos, cos], axis=-1).reshape(n_tokens, HEAD_DIM)
    sin_h = jnp.concatenate([-sin, sin], axis=-1).reshape(n_tokens, HEAD_DIM)
    return jnp.tile(cos_h, (1, 2)), jnp.tile(sin_h, (1, 2))


def _gqa_weights(w_qkv):
    d = w_qkv.shape[0]
    n_q = N_KV_HEADS * GQA_GROUP * HEAD_DIM
    n_kv = N_KV_HEADS * HEAD_DIM
    wq = w_qkv[:, :n_q].reshape(d, N_KV_HEADS // 2, 2, GQA_GROUP, HEAD_DIM)
    wq = wq.transpose(0, 1, 3, 2, 4).reshape(d, n_q)
    wk = w_qkv[:, n_q:n_q + n_kv]
    wvt = w_qkv[:, n_q + n_kv:].T
    return wq.astype(BF16), wk.astype(BF16), wvt.astype(BF16)


def _diff_weights(w_qkv):
    n = w_qkv.shape[1] // 3
    return (w_qkv[:, :n].astype(BF16), w_qkv[:, n:2 * n].astype(BF16),
            w_qkv[:, 2 * n:].T.astype(BF16))


def _lane_gain(g):
    return jnp.tile(g.reshape(1, HEAD_DIM), (1, LANES // HEAD_DIM))


def _sink_rows(sink, tq):
    return jnp.repeat(sink.reshape(N_KV_HEADS, 1, GQA_GROUP, 1), tq, axis=3).reshape(
        N_KV_HEADS, 1, GQA_GROUP * tq)


def kernel(x, c, ctx, c_ctx, norm_g, w_ada, b_ada, w_ffn_in, w_ffn_out, w_o, w_qkv_a, qk_norm_a,
           w_qkv_b, qk_norm_b, sink_b, w_qkv_c, qk_norm_c, diff_lambda, diff_subln):
    n_b, t, d = x.shape
    n_c = ctx.shape[1]
    depth = w_ada.shape[0]
    tm = 512
    tm_ffn = 1024
    tq, tq_diff = 1024, 2048
    tq_win = 256
    tq_c = n_c
    tkv = 512
    assert n_b + 1 <= SUBLANES and n_c % LANES == 0
    assert all(t % tile == 0 for tile in (tm, tm_ffn, tq, tq_diff, tq_win, tkv, GRID_W))

    c8 = jnp.zeros((SUBLANES, d), F32).at[:n_b].set(c).at[n_b].set(c_ctx)
    mod5 = _ada_all(c8, w_ada, b_ada).reshape(depth, SUBLANES, 3, 3, d)

    rope_tabs = _rope_tables(t)
    eye = jnp.arange(MXU_DIM) // HEAD_DIM
    bd = (eye[:, None] == eye[None, :]).astype(BF16)
    w_in = w_ffn_in.astype(BF16)
    w_out = w_ffn_out.astype(BF16)
    wo = w_o.astype(BF16)
    norm_g4 = norm_g.reshape(depth, 3, 1, d)

    xs = x.reshape(n_b * t, d)
    cs = ctx.reshape(n_b * n_c, d)
    lat = dict(tm=tm, rows_per_batch=t, ctx_row=None)
    lat_ffn = dict(tm=tm_ffn, rows_per_batch=t, ctx_row=None)
    cx = dict(tm=n_c, rows_per_batch=n_c, ctx_row=n_b)

    for i in range(depth):
        last = i == depth - 1
        kind, j = i % 3, i // 3
        xs = _ffn(xs, mod5, norm_g4, w_in, w_out, i, 0, **lat_ffn)
        cs = _ffn(cs, mod5, norm_g4, w_in, w_out, i, 0, **cx)
        if kind == 2:
            wq, wk, wvt = _diff_weights(w_qkv_c[j])
            qk_g = qk_norm_c[j]
        else:
            wq, wk, wvt = _gqa_weights((w_qkv_a, w_qkv_b)[kind][j])
            qk_g = (qk_norm_a, qk_norm_b)[kind][j]
        gq, gk = _lane_gain(qk_g[0]), _lane_gain(qk_g[1])
        q, kl, vtl = _proj(xs, mod5, norm_g4, i, wq, wk, wvt, gq, gk, bd, rope_tabs, tkv=tkv,
                           **lat)
        qc, kc, vtc = _proj(cs, mod5, norm_g4, i, wq, wk, wvt, gq, gk, bd, None, tkv=n_c, **cx)
        vtc = vtc.reshape(n_b, vtc.shape[2], n_c)
        if kind == 0:
            ot = _flash_attn(q, kc, vtc, kl, vtl, mode="gqa", n_batch=n_b, tq=tq)
            if not last:
                otc = _flash_attn(qc, kc, vtc, None, None, mode="gqa", n_batch=n_b, tq=tq_c)
        elif kind == 1:
            ot = _window_attn(q, kc, vtc, kl, vtl, _sink_rows(sink_b[j], tq_win), n_batch=n_b,
                              tq=tq_win)
            if not last:
                otc = _flash_attn(qc, kc, vtc, None, None, mode="gqa", n_batch=n_b, tq=tq_c,
                                  sink_rows=_sink_rows(sink_b[j], tq_c))
        else:
            extra = dict(lam_params=diff_lambda[j], subln=diff_subln[j],
                         lam_init=0.8 - 0.6 * math.exp(-0.3 * i))
            ot = _flash_attn(q, kc, vtc, kl, vtl, mode="diff", n_batch=n_b, tq=tq_diff, **extra)
            if not last:
                otc = _flash_attn(qc, kc, vtc, None, None, mode="diff", n_batch=n_b, tq=tq_c,
                                  **extra)
        xs = _ffn(xs, mod5, norm_g4, w_in, w_out, i, 1, pre=(ot, wo), **lat_ffn)
        if not last:
            cs = _ffn(cs, mod5, norm_g4, w_in, w_out, i, 1, pre=(otc, wo), **cx)
    return xs.reshape(n_b, t, d)
```

```python
import functools
import math

import jax
import jax.numpy as jnp
from jax import lax
from jax.experimental import pallas as pl
from jax.experimental.pallas import tpu as pltpu

F32 = jnp.float32
BF16 = jnp.bfloat16

HEAD_DIM = 64
N_KV_HEADS = 4
GQA_GROUP = 4
GRID_W = 64
WINDOW = 128
ROPE_BASE = 10000.0
N_FREQ = HEAD_DIM // 4
EPS = 1e-6
NEG_INF = -1e30
LOG2E = math.log2(math.e)
Q_SCALE = HEAD_DIM ** -0.5 * LOG2E
FFN_RESIDUAL = 0.5
LANES = 128
MXU_DIM = 256
SUM_ROWS = 16
PIPE_SLOTS = 8
LOOKAHEAD = 1
SCORE_BUFS = LOOKAHEAD + 1
V7X_VMEM_BYTES = 64 * 1024 * 1024
SUBLANES = 8


def _tile_bytes(shape, dtype):
    item = jnp.dtype(dtype).itemsize
    dims = [1 if s is None else s for s in shape]
    dims[-1] = pl.cdiv(dims[-1], LANES) * LANES
    if len(dims) > 1:
        rows = SUBLANES * 4 // item
        dims[-2] = pl.cdiv(dims[-2], rows) * rows
    return math.prod(dims) * item


def _pallas(body, *, name, grid, sem, in_specs, args, out_specs, out_shape, scratch=(), temps=0):
    outs = out_shape if isinstance(out_shape, (list, tuple)) else [out_shape]
    o_specs = out_specs if isinstance(out_specs, (list, tuple)) else [out_specs]
    need = temps + sum(_tile_bytes(s.shape, s.dtype) for s in scratch)
    for spec, a in list(zip(in_specs, args)) + list(zip(o_specs, outs)):
        n_buf = spec.pipeline_mode.buffer_count if spec.pipeline_mode is not None else 2
        need += n_buf * _tile_bytes(spec.block_shape, a.dtype)
    return pl.pallas_call(
        body, grid=grid, in_specs=in_specs, out_specs=out_specs, out_shape=out_shape,
        scratch_shapes=list(scratch), name=name,
        compiler_params=pltpu.CompilerParams(dimension_semantics=sem,
                                             vmem_limit_bytes=min(need, V7X_VMEM_BYTES)),
    )(*args)


def _silu(a):
    return a * jax.nn.sigmoid(a)


def _norm_mod(x, g, shift, scale):
    y = x * lax.rsqrt(jnp.mean(x * x, axis=-1, keepdims=True) + EPS)
    return (y * g) * (1.0 + scale) + shift


def _ada_kernel(c_ref, w_ref, b_ref, o_ref):
    s = _silu(c_ref[...])
    o_ref[...] = jnp.dot(s, w_ref[...], preferred_element_type=F32) + b_ref[...]


def _ada_all(c8, w_ada, b_ada):
    depth, d, nd = w_ada.shape
    rows = c8.shape[0]
    tn = nd // 8
    return _pallas(
        _ada_kernel, name="adaln", grid=(depth, nd // tn), sem=("parallel", "parallel"),
        in_specs=[
            pl.BlockSpec((rows, d), lambda i, j: (0, 0)),
            pl.BlockSpec((None, d, tn), lambda i, j: (i, 0, j)),
            pl.BlockSpec((None, 1, tn), lambda i, j: (i, 0, j)),
        ],
        args=[c8, w_ada, b_ada.reshape(depth, 1, nd)],
        out_specs=pl.BlockSpec((None, rows, tn), lambda i, j: (i, 0, j)),
        out_shape=jax.ShapeDtypeStruct((depth, rows, nd), F32),
        temps=_tile_bytes((rows, d), F32) + 2 * _tile_bytes((rows, tn), F32))


def _ffn_kernel(*refs, pre):
    if pre:
        x_ref, ot_ref, wo_ref, modp_ref, mod_ref, g_ref, wa_ref, wu_ref, wout_ref, o_ref = refs
    else:
        x_ref, mod_ref, g_ref, wa_ref, wu_ref, wout_ref, o_ref = refs
    x = x_ref[...]
    if pre:
        x = x + modp_ref[2:3, :] * pl.dot(ot_ref[...], wo_ref[...], trans_a=True)
    h = _norm_mod(x, g_ref[...], mod_ref[0:1, :], mod_ref[1:2, :]).astype(BF16)
    a = jnp.dot(h, wa_ref[...], preferred_element_type=F32)
    u = jnp.dot(h, wu_ref[...], preferred_element_type=F32)
    act = (_silu(a) * u).astype(BF16)
    y = jnp.dot(act, wout_ref[...], preferred_element_type=F32)
    o_ref[...] = x + (FFN_RESIDUAL * mod_ref[2:3, :]) * y


def _ffn(x2, mod5, norm_g4, w_in, w_out, layer, half, *, tm, rows_per_batch, ctx_row, pre=None):
    r, d = x2.shape
    dff = w_out.shape[2]
    tpb = rows_per_batch // tm
    k = 2 * half
    resident = dict(pipeline_mode=pl.Buffered(1))

    def bidx(i):
        return ctx_row if ctx_row is not None else i // tpb

    def mod_spec(kk):
        return pl.BlockSpec((None, None, None, 3, d), lambda i: (layer, bidx(i), kk, 0, 0))

    in_specs = [pl.BlockSpec((tm, d), lambda i: (i, 0))]
    args = [x2]
    if pre is not None:
        ot, wo = pre
        in_specs += [
            pl.BlockSpec((None, d, tm), lambda i: (i // tpb, 0, i % tpb)),
            pl.BlockSpec((None, d, d), lambda i: (layer, 0, 0), **resident),
            mod_spec(1),
        ]
        args += [ot, wo, mod5]
    in_specs += [
        mod_spec(k),
        pl.BlockSpec((None, None, 1, d), lambda i: (layer, k, 0, 0)),
        pl.BlockSpec((None, None, d, dff), lambda i: (layer, half, 0, 0), **resident),
        pl.BlockSpec((None, None, d, dff), lambda i: (layer, half, 0, 1), **resident),
        pl.BlockSpec((None, None, dff, d), lambda i: (layer, half, 0, 0), **resident),
    ]
    args += [mod5, norm_g4, w_in, w_in, w_out]
    temps = tm * dff * (4 + 4 + 2) + tm * d * (4 + 4 + 2)
    return _pallas(
        functools.partial(_ffn_kernel, pre=pre is not None),
        name="ffn_pre" if pre is not None else "ffn", grid=(r // tm,), sem=("parallel",),
        in_specs=in_specs, args=args,
        out_specs=pl.BlockSpec((tm, d), lambda i: (i, 0)),
        out_shape=jax.ShapeDtypeStruct((r, d), F32), temps=temps)


def _headnorm_rope(z, gain, cos, sin, bd, rope):
    tm, w = z.shape
    lane = lax.broadcasted_iota(jnp.int32, (1, LANES), 1)
    first_half = (lane & (2 * N_FREQ - 1)) < N_FREQ
    outs = []
    for j in range(w // MXU_DIM):
        zj = z[:, j * MXU_DIM:(j + 1) * MXU_DIM]
        sq = zj * zj
        hi = sq.astype(BF16)
        lo = (sq - hi.astype(F32)).astype(BF16)
        ss = (jnp.dot(hi, bd, preferred_element_type=F32)
              + jnp.dot(lo, bd, preferred_element_type=F32))
        zn = zj * lax.rsqrt(ss * (1.0 / HEAD_DIM) + EPS)
        for half in range(MXU_DIM // LANES):
            t = zn[:, half * LANES:(half + 1) * LANES] * gain
            if rope:
                partner = jnp.where(first_half, pltpu.roll(t, LANES - N_FREQ, 1),
                                    pltpu.roll(t, N_FREQ, 1))
                t = t * cos + partner * sin
            outs.append(t)
    return jnp.concatenate(outs, axis=1)


def _proj_kernel(*refs, rope):
    if rope:
        (x_ref, mod_ref, g_ref, wq_ref, wk_ref, wvt_ref, gq_ref, gk_ref, bd_ref, cos_ref,
         sin_ref, q_ref, k_ref, vt_ref) = refs
        cos, sin = cos_ref[...], sin_ref[...]
    else:
        (x_ref, mod_ref, g_ref, wq_ref, wk_ref, wvt_ref, gq_ref, gk_ref, bd_ref,
         q_ref, k_ref, vt_ref) = refs
        cos = sin = None
    h = _norm_mod(x_ref[...], g_ref[...], mod_ref[0:1, :], mod_ref[1:2, :]).astype(BF16)
    bd = bd_ref[...]
    q = jnp.dot(h, wq_ref[...], preferred_element_type=F32)
    q_ref[...] = _headnorm_rope(q, gq_ref[...] * Q_SCALE, cos, sin, bd, rope).astype(BF16)
    k = jnp.dot(h, wk_ref[...], preferred_element_type=F32)
    k_ref[...] = _headnorm_rope(k, gk_ref[...], cos, sin, bd, rope).astype(BF16)
    vt = pl.dot(wvt_ref[...], h, trans_b=True).astype(BF16)
    tkv = vt_ref.shape[2]
    for c in range(vt_ref.shape[0]):
        vt_ref[c] = vt[:, c * tkv:(c + 1) * tkv]


def _proj(x2, mod5, norm_g4, layer, wq, wk, wvt, gq, gk, bd, rope_tabs, *, tm, rows_per_batch,
          ctx_row, tkv):
    r, d = x2.shape
    wq_n, wk_n, wv_n = wq.shape[1], wk.shape[1], wvt.shape[0]
    tpb = rows_per_batch // tm
    kpt = tm // tkv
    nb = r // rows_per_batch
    rope = rope_tabs is not None

    def bidx(i):
        return ctx_row if ctx_row is not None else i // tpb

    const = lambda i: (0, 0)
    in_specs = [
        pl.BlockSpec((tm, d), lambda i: (i, 0)),
        pl.BlockSpec((None, None, None, 3, d), lambda i: (layer, bidx(i), 1, 0, 0)),
        pl.BlockSpec((None, None, 1, d), lambda i: (layer, 1, 0, 0)),
        pl.BlockSpec((d, wq_n), const),
        pl.BlockSpec((d, wk_n), const),
        pl.BlockSpec((wv_n, d), const),
        pl.BlockSpec((1, LANES), const),
        pl.BlockSpec((1, LANES), const),
        pl.BlockSpec((MXU_DIM, MXU_DIM), const),
    ]
    args = [x2, mod5, norm_g4, wq, wk, wvt, gq, gk, bd]
    if rope:
        in_specs += [pl.BlockSpec((tm, LANES), lambda i: (i % tpb, 0))] * 2
        args += list(rope_tabs)
    temps = tm * d * (4 + 2) + 2 * tm * (wq_n + wk_n + wv_n) * 4
    return _pallas(
        functools.partial(_proj_kernel, rope=rope), name="qkv_proj", grid=(r // tm,),
        sem=("parallel",), in_specs=in_specs, args=args,
        out_specs=[
            pl.BlockSpec((tm, wq_n), lambda i: (i, 0)),
            pl.BlockSpec((tm, wk_n), lambda i: (i, 0)),
            pl.BlockSpec((None, kpt, wv_n, tkv), lambda i: (i // tpb, i % tpb, 0, 0)),
        ],
        out_shape=[
            jax.ShapeDtypeStruct((r, wq_n), BF16),
            jax.ShapeDtypeStruct((r, wk_n), BF16),
            jax.ShapeDtypeStruct((nb, tpb * kpt, wv_n, tkv), BF16),
        ],
        temps=temps)


def _slot_mask(slot):
    lane = lax.broadcasted_iota(jnp.int32, (1, LANES), 1)
    return (lane >= slot * HEAD_DIM) & (lane < (slot + 1) * HEAD_DIM)


def _gqa_q_tile(q_ref, kvh, g):
    j = g + GQA_GROUP * (kvh // 2)
    tile = q_ref[:, j * LANES:(j + 1) * LANES]
    return jnp.where(_slot_mask(kvh % 2), tile, jnp.zeros_like(tile))


_GQA_UNITS = tuple(
    (kvh // 2, kvh * HEAD_DIM,
     tuple((g + GQA_GROUP * (kvh // 2), kvh % 2) for g in range(GQA_GROUP)))
    for kvh in range(N_KV_HEADS))
_DIFF_UNITS = tuple((h, h * 2 * HEAD_DIM, ((h, 0), (h, 1))) for h in range(2))


def _flash_kernel(*refs, units, dv, tq, tk, n_kb, mode, has_sink, lam_init):
    refs = list(refs)
    q_ref, kc_ref, vtc_ref = refs[:3]
    pos = 3
    has_lat = n_kb > 0
    if has_lat:
        kl_ref, vtl_ref = refs[pos:pos + 2]
        pos += 2
    if mode == "diff":
        lam_ref, subln_ref = refs[pos:pos + 2]
        pos += 2
    elif has_sink:
        sink_ref = refs[pos]
        pos += 1
    o_ref, qs_scr, m_scr, acc_scr = refs[pos:pos + 4]
    if has_lat:
        s_scr, mx_scr = refs[pos + 4:]
    n_u = len(units)

    def pv(vt, p):
        ones = jnp.ones((SUM_ROWS, vt.shape[1]), BF16)
        return jnp.dot(jnp.concatenate([vt, ones], axis=0), p.astype(BF16),
                       preferred_element_type=F32)

    for u, (_, _, q_tiles) in enumerate(units):
        for t, (j, slot) in enumerate(q_tiles):
            tile = q_ref[:, j * LANES:(j + 1) * LANES]
            tile = jnp.where(_slot_mask(slot), tile, jnp.zeros_like(tile))
            qs_scr[u, :, t * tq:(t + 1) * tq] = tile.T

    for u, (half, v0, _) in enumerate(units):
        s = jnp.dot(kc_ref[:, half * LANES:(half + 1) * LANES], qs_scr[u],
                    preferred_element_type=F32)
        m = jnp.max(s, axis=0, keepdims=True)
        p = jnp.exp2(s - m)
        m_scr[u] = m
        acc_scr[u] = pv(vtc_ref[v0:v0 + dv, :], p)

    if has_lat:
        col_tiles = [slice(c0, c0 + MXU_DIM) for c0 in range(0, s_scr.shape[2], MXU_DIM)]

        key_halves = [slice(r0, r0 + MXU_DIM) for r0 in range(0, tk, MXU_DIM)]

        def scores(kb, u, slot, cs, rs):
            half = units[u][0]
            row0 = pl.multiple_of(kb * tk, tk) + rs.start
            k = kl_ref[pl.ds(row0, MXU_DIM), half * LANES:(half + 1) * LANES]
            s = jnp.dot(k, qs_scr[u, :, cs], preferred_element_type=F32)
            s_scr[slot, rs, cs] = s
            return jnp.max(s, axis=0, keepdims=True)

        def probs(u, slot, cs):
            m_prev = m_scr[u, :, cs]
            m_new = jnp.maximum(m_prev, mx_scr[slot, :, cs])
            m_scr[u, :, cs] = m_new
            return (jnp.exp2(s_scr[slot, :, cs] - m_new).astype(BF16),
                    jnp.exp2(m_prev - m_new))

        def pv_part(kb, u, p, rs):
            v0 = units[u][1]
            return pv(vtl_ref[kb, v0:v0 + dv, rs], p[rs, :])

        bpt = PIPE_SLOTS // n_u
        n_trips = n_kb // bpt

        def trip(it, last):
            pending = None
            for j in range(PIPE_SLOTS):
                ahead = j + LOOKAHEAD
                if ahead < PIPE_SLOTS:
                    sc = (it * bpt + ahead // n_u, ahead % n_u, ahead % SCORE_BUFS)
                elif not last:
                    a2 = ahead - PIPE_SLOTS
                    sc = ((it + 1) * bpt + a2 // n_u, a2 % n_u, a2 % SCORE_BUFS)
                else:
                    sc = None
                for cs in col_tiles:
                    fresh = (it * bpt + j // n_u, j % n_u, cs,
                             *probs(j % n_u, j % SCORE_BUFS, cs))
                    mx, parts = None, []
                    for rs in key_halves:
                        if sc is not None:
                            part = scores(*sc, cs, rs)
                            mx = part if mx is None else jnp.maximum(mx, part)
                        if pending is not None:
                            parts.append(pv_part(pending[0], pending[1], pending[3], rs))
                    if sc is not None:
                        mx_scr[sc[2], :, cs] = mx
                    if pending is not None:
                        _, u_p, cs_p, _, alpha_p = pending
                        acc_scr[u_p, :, cs_p] = alpha_p * acc_scr[u_p, :, cs_p] + sum(parts)
                    pending = fresh
            kb_p, u_p, cs_p, p, alpha_p = pending
            acc_scr[u_p, :, cs_p] = alpha_p * acc_scr[u_p, :, cs_p] + sum(
                pv_part(kb_p, u_p, p, rs) for rs in key_halves)

        for j in range(LOOKAHEAD):
            for cs in col_tiles:
                mx_scr[j, :, cs] = functools.reduce(
                    jnp.maximum, [scores(j // n_u, j % n_u, j, cs, rs) for rs in key_halves])
        lax.fori_loop(0, n_trips - 1, lambda it, c: (trip(it, False), c)[1], 0)
        trip(n_trips - 1, True)

    if mode == "diff":
        lp = lam_ref[...]
        lam = (jnp.exp(jnp.sum(lp[0:1] * lp[1:2], axis=1, keepdims=True))
               - jnp.exp(jnp.sum(lp[2:3] * lp[3:4], axis=1, keepdims=True)) + lam_init)
        gain = subln_ref[...]
        for u in range(n_u):
            o12 = acc_scr[u, :dv, :] / acc_scr[u, dv:dv + 1, :]
            o = o12[:, :tq] - lam * o12[:, tq:]
            o = o * lax.rsqrt(jnp.mean(o * o, axis=0, keepdims=True) + EPS)
            o_ref[u * dv:(u + 1) * dv, :] = ((o * gain) * (1.0 - lam_init)).astype(BF16)
    else:
        for u in range(n_u):
            m, l, acc = m_scr[u], acc_scr[u, dv:dv + 1, :], acc_scr[u, :dv, :]
            if has_sink:
                sk = sink_ref[u] * LOG2E
                m_f = jnp.maximum(m, sk)
                w = jnp.exp2(m - m_f)
                l = l * w + jnp.exp2(sk - m_f)
                acc = acc * w
            o = acc / l
            for g in range(GQA_GROUP):
                row = (u * GQA_GROUP + g) * HEAD_DIM
                o_ref[row:row + HEAD_DIM, :] = o[:, g * tq:(g + 1) * tq].astype(BF16)


def _flash_attn(q, kc, vtc, kl, vtl, *, mode, n_batch, tq, sink_rows=None, lam_params=None,
                subln=None, lam_init=0.0):
    r, dq = q.shape
    t_q = r // n_batch
    c = kc.shape[0] // n_batch
    wk = kc.shape[1]
    n_grp = wk // MXU_DIM
    q_cols = dq // n_grp
    units, dv = (_GQA_UNITS, HEAD_DIM) if mode == "gqa" else (_DIFF_UNITS, 2 * HEAD_DIM)
    cols = len(units[0][2]) * tq
    has_lat = kl is not None
    nq = t_q // tq
    in_specs = [
        pl.BlockSpec((tq, q_cols), lambda b, g, i: (b * nq + i, g)),
        pl.BlockSpec((c, MXU_DIM), lambda b, g, i: (b, g)),
        pl.BlockSpec((None, MXU_DIM, c), lambda b, g, i: (b, g, 0)),
    ]
    args = [q, kc, vtc]
    n_kb, tk = 0, 0
    scratch = [
        pltpu.VMEM((len(units), LANES, cols), BF16),
        pltpu.VMEM((len(units), 1, cols), F32),
        pltpu.VMEM((len(units), dv + SUM_ROWS, cols), F32),
    ]
    if has_lat:
        t = kl.shape[0] // n_batch
        n_kb, tk = vtl.shape[1], vtl.shape[3]
        in_specs += [
            pl.BlockSpec((t, MXU_DIM), lambda b, g, i: (b, g)),
            pl.BlockSpec((None, n_kb, MXU_DIM, tk), lambda b, g, i: (b, 0, g, 0)),
        ]
        args += [kl, vtl]
        assert PIPE_SLOTS % len(units) == 0 and n_kb % (PIPE_SLOTS // len(units)) == 0
        scratch += [pltpu.VMEM((SCORE_BUFS, tk, cols), F32),
                    pltpu.VMEM((SCORE_BUFS, 1, cols), F32)]
    if mode == "diff":
        in_specs += [
            pl.BlockSpec((4, HEAD_DIM), lambda b, g, i: (0, 0)),
            pl.BlockSpec((dv, 1), lambda b, g, i: (0, 0)),
        ]
        args += [lam_params, subln.reshape(dv, 1)]
    elif sink_rows is not None:
        in_specs.append(pl.BlockSpec((N_KV_HEADS, 1, cols), lambda b, g, i: (0, 0, 0)))
        args.append(sink_rows)
    temps = c * cols * (4 + 4 + 2) + 4 * max(tk, c) * MXU_DIM * (4 + 4 + 2)
    return _pallas(
        functools.partial(_flash_kernel, units=units, dv=dv, tq=tq, tk=tk, n_kb=n_kb, mode=mode,
                          has_sink=sink_rows is not None, lam_init=lam_init),
        name=mode + "_attn", grid=(n_batch, n_grp, nq),
        sem=("parallel", "parallel", "arbitrary"), in_specs=in_specs, args=args,
        out_specs=pl.BlockSpec((None, q_cols, tq), lambda b, g, i: (b, g, i)),
        out_shape=jax.ShapeDtypeStruct((n_batch, dq, t_q), BF16),
        scratch=scratch, temps=temps)


def _window_kernel(q_ref, kc_ref, vtc_ref, kp_ref, kcur_ref, kn_ref, vtp_ref, vtcur_ref,
                   vtn_ref, sink_ref, o_ref, *, tq):
    qi = pl.program_id(1)
    nq = pl.num_programs(1)
    n_lat = tq + 2 * WINDOW
    rr = lax.broadcasted_iota(jnp.int32, (n_lat, 1), 0)
    cc = lax.broadcasted_iota(jnp.int32, (1, tq), 1)
    valid = jnp.abs(rr - WINDOW - cc) <= WINDOW
    valid &= (rr >= WINDOW) | (qi > 0)
    valid &= (rr < tq + WINDOW) | (qi < nq - 1)
    bias = jnp.where(valid, 0.0, NEG_INF)

    k_lat = [jnp.concatenate([r[:, h * LANES:(h + 1) * LANES] for r in (kp_ref, kcur_ref, kn_ref)],
                             axis=0) for h in range(2)]

    def pv(vt, p):
        ones = jnp.ones((SUM_ROWS, vt.shape[1]), BF16)
        return jnp.dot(jnp.concatenate([vt, ones], axis=0), p, preferred_element_type=F32)

    def scores(kvh, g):
        half = kvh // 2
        qt = _gqa_q_tile(q_ref, kvh, g).T
        s_ctx = jnp.dot(kc_ref[:, half * LANES:(half + 1) * LANES], qt,
                        preferred_element_type=F32)
        s_lat = jnp.dot(k_lat[half], qt, preferred_element_type=F32) + bias
        sk = sink_ref[kvh, :, g * tq:(g + 1) * tq] * LOG2E
        m = jnp.maximum(jnp.maximum(jnp.max(s_ctx, axis=0, keepdims=True),
                                    jnp.max(s_lat, axis=0, keepdims=True)), sk)
        return s_ctx, s_lat, m, jnp.exp2(sk - m)

    def probs(s_ctx, s_lat, m, p_sink):
        return jnp.exp2(s_ctx - m).astype(BF16), jnp.exp2(s_lat - m).astype(BF16), p_sink

    def output(kvh, g, p_ctx, p_lat, p_sink):
        hs = slice(kvh * HEAD_DIM, (kvh + 1) * HEAD_DIM)
        vt_lat = jnp.concatenate([vtp_ref[hs, :], vtcur_ref[hs, :], vtn_ref[hs, :]], axis=1)
        acc = pv(vtc_ref[hs, :], p_ctx) + pv(vt_lat, p_lat)
        l = acc[HEAD_DIM:HEAD_DIM + 1, :] + p_sink
        row = (kvh * GQA_GROUP + g) * HEAD_DIM
        o_ref[row:row + HEAD_DIM, :] = (acc[:HEAD_DIM, :] / l).astype(BF16)

    tiles = [(kvh, g) for kvh in range(N_KV_HEADS) for g in range(GQA_GROUP)]
    scored, pending = scores(*tiles[0]), None
    for t, tile in enumerate(tiles):
        ahead = scores(*tiles[t + 1]) if t + 1 < len(tiles) else None
        fresh = (*tile, *probs(*scored))
        if pending is not None:
            output(*pending)
        scored, pending = ahead, fresh
    output(*pending)


def _window_attn(q, kc, vtc, kl, vtl, sink_rows, *, n_batch, tq):
    r, dq = q.shape
    t = r // n_batch
    c = kc.shape[0] // n_batch
    wk = kc.shape[1]
    tkv = vtl.shape[3]
    nq = t // tq
    rb = tq // WINDOW
    nwb = t // WINDOW
    wpb, qpb = tkv // WINDOW, tkv // tq
    prev = lambda b, i: jnp.maximum(i * rb - 1, 0)
    nxt = lambda b, i: jnp.minimum((i + 1) * rb, nwb - 1)
    in_specs = [
        pl.BlockSpec((tq, dq), lambda b, i: (b * nq + i, 0)),
        pl.BlockSpec((c, wk), lambda b, i: (b, 0)),
        pl.BlockSpec((None, wk, c), lambda b, i: (b, 0, 0)),
        pl.BlockSpec((WINDOW, wk), lambda b, i: (b * nwb + prev(b, i), 0)),
        pl.BlockSpec((tq, wk), lambda b, i: (b * nq + i, 0)),
        pl.BlockSpec((WINDOW, wk), lambda b, i: (b * nwb + nxt(b, i), 0)),
        pl.BlockSpec((None, None, wk, WINDOW),
                     lambda b, i: (b, prev(b, i) // wpb, 0, prev(b, i) % wpb)),
        pl.BlockSpec((None, None, wk, tq), lambda b, i: (b, i // qpb, 0, i % qpb)),
        pl.BlockSpec((None, None, wk, WINDOW),
                     lambda b, i: (b, nxt(b, i) // wpb, 0, nxt(b, i) % wpb)),
        pl.BlockSpec((N_KV_HEADS, 1, GQA_GROUP * tq), lambda b, i: (0, 0, 0)),
    ]
    n_keys = c + tq + 2 * WINDOW
    temps = n_keys * GQA_GROUP * tq * (4 + 4 + 2) + GQA_GROUP * tq * LANES * 2
    return _pallas(
        functools.partial(_window_kernel, tq=tq), name="window_attn", grid=(n_batch, nq),
        sem=("parallel", "parallel"), in_specs=in_specs,
        args=[q, kc, vtc, kl, kl, kl, vtl, vtl, vtl, sink_rows],
        out_specs=pl.BlockSpec((None, dq, tq), lambda b, i: (b, 0, i)),
        out_shape=jax.ShapeDtypeStruct((n_batch, dq, t), BF16), temps=temps)


def _rope_tables(n_tokens):
    rows = n_tokens // GRID_W
    row = jnp.repeat(jnp.arange(rows), GRID_W)
    col = jnp.tile(jnp.arange(GRID_W), rows)
    inv_freq = ROPE_BASE ** (-jnp.arange(N_FREQ, dtype=F32) / N_FREQ)
    ang = jnp.stack([row, col], axis=-1).astype(F32)[:, :, None] * inv_freq
    cos, sin = jnp.cos(ang), jnp.sin(ang)
    cos_h = jnp.concatenate([cos, cos], axis=-1).reshape(n_tokens, HEAD_DIM)
    sin_h = jnp.concatenate([-sin, sin], axis=-1).reshape(n_tokens, HEAD_DIM)
    return jnp.tile(cos_h, (1, 2)), jnp.tile(sin_h, (1, 2))


def _gqa_weights(w_qkv):
    d = w_qkv.shape[0]
    n_q = N_KV_HEADS * GQA_GROUP * HEAD_DIM
    n_kv = N_KV_HEADS * HEAD_DIM
    wq = w_qkv[:, :n_q].reshape(d, N_KV_HEADS // 2, 2, GQA_GROUP, HEAD_DIM)
    wq = wq.transpose(0, 1, 3, 2, 4).reshape(d, n_q)
    wk = w_qkv[:, n_q:n_q + n_kv]
    wvt = w_qkv[:, n_q + n_kv:].T
    return wq.astype(BF16), wk.astype(BF16), wvt.astype(BF16)


def _diff_weights(w_qkv):
    n = w_qkv.shape[1] // 3
    return (w_qkv[:, :n].astype(BF16), w_qkv[:, n:2 * n].astype(BF16),
            w_qkv[:, 2 * n:].T.astype(BF16))


def _lane_gain(g):
    return jnp.tile(g.reshape(1, HEAD_DIM), (1, LANES // HEAD_DIM))


def _sink_rows(sink, tq):
    return jnp.repeat(sink.reshape(N_KV_HEADS, 1, GQA_GROUP, 1), tq, axis=3).reshape(
        N_KV_HEADS, 1, GQA_GROUP * tq)


def kernel(x, c, ctx, c_ctx, norm_g, w_ada, b_ada, w_ffn_in, w_ffn_out, w_o, w_qkv_a, qk_norm_a,
           w_qkv_b, qk_norm_b, sink_b, w_qkv_c, qk_norm_c, diff_lambda, diff_subln):
    n_b, t, d = x.shape
    n_c = ctx.shape[1]
    depth = w_ada.shape[0]
    tm = 512
    tm_ffn = 1024
    tq, tq_diff = 1024, 2048
    tq_win = 256
    tq_c = n_c
    tkv = 512
    assert n_b + 1 <= SUBLANES and n_c % LANES == 0
    assert all(t % tile == 0 for tile in (tm, tm_ffn, tq, tq_diff, tq_win, tkv, GRID_W))

    c8 = jnp.zeros((SUBLANES, d), F32).at[:n_b].set(c).at[n_b].set(c_ctx)
    mod5 = _ada_all(c8, w_ada, b_ada).reshape(depth, SUBLANES, 3, 3, d)

    rope_tabs = _rope_tables(t)
    eye = jnp.arange(MXU_DIM) // HEAD_DIM
    bd = (eye[:, None] == eye[None, :]).astype(BF16)
    w_in = w_ffn_in.astype(BF16)
    w_out = w_ffn_out.astype(BF16)
    wo = w_o.astype(BF16)
    norm_g4 = norm_g.reshape(depth, 3, 1, d)

    xs = x.reshape(n_b * t, d)
    cs = ctx.reshape(n_b * n_c, d)
    lat = dict(tm=tm, rows_per_batch=t, ctx_row=None)
    lat_ffn = dict(tm=tm_ffn, rows_per_batch=t, ctx_row=None)
    cx = dict(tm=n_c, rows_per_batch=n_c, ctx_row=n_b)

    for i in range(depth):
        last = i == depth - 1
        kind, j = i % 3, i // 3
        xs = _ffn(xs, mod5, norm_g4, w_in, w_out, i, 0, **lat_ffn)
        cs = _ffn(cs, mod5, norm_g4, w_in, w_out, i, 0, **cx)
        if kind == 2:
            wq, wk, wvt = _diff_weights(w_qkv_c[j])
            qk_g = qk_norm_c[j]
        else:
            wq, wk, wvt = _gqa_weights((w_qkv_a, w_qkv_b)[kind][j])
            qk_g = (qk_norm_a, qk_norm_b)[kind][j]
        gq, gk = _lane_gain(qk_g[0]), _lane_gain(qk_g[1])
        q, kl, vtl = _proj(xs, mod5, norm_g4, i, wq, wk, wvt, gq, gk, bd, rope_tabs, tkv=tkv,
                           **lat)
        qc, kc, vtc = _proj(cs, mod5, norm_g4, i, wq, wk, wvt, gq, gk, bd, None, tkv=n_c, **cx)
        vtc = vtc.reshape(n_b, vtc.shape[2], n_c)
        if kind == 0:
            ot = _flash_attn(q, kc, vtc, kl, vtl, mode="gqa", n_batch=n_b, tq=tq)
            if not last:
                otc = _flash_attn(qc, kc, vtc, None, None, mode="gqa", n_batch=n_b, tq=tq_c)
        elif kind == 1:
            ot = _window_attn(q, kc, vtc, kl, vtl, _sink_rows(sink_b[j], tq_win), n_batch=n_b,
                              tq=tq_win)
            if not last:
                otc = _flash_attn(qc, kc, vtc, None, None, mode="gqa", n_batch=n_b, tq=tq_c,
                                  sink_rows=_sink_rows(sink_b[j], tq_c))
        else:
            extra = dict(lam_params=diff_lambda[j], subln=diff_subln[j],
                         lam_init=0.8 - 0.6 * math.exp(-0.3 * i))
            ot = _flash_attn(q, kc, vtc, kl, vtl, mode="diff", n_batch=n_b, tq=tq_diff, **extra)
            if not last:
                otc = _flash_attn(qc, kc, vtc, None, None, mode="diff", n_batch=n_b, tq=tq_c,
                                  **extra)
        xs = _ffn(xs, mod5, norm_g4, w_in, w_out, i, 1, pre=(ot, wo), **lat_ffn)
        if not last:
            cs = _ffn(cs, mod5, norm_g4, w_in, w_out, i, 1, pre=(otc, wo), **cx)
    return xs.reshape(n_b, t, d)
```

```python
import functools
import math

import jax
import jax.numpy as jnp
from jax import lax
from jax.experimental import pallas as pl
from jax.experimental.pallas import tpu as pltpu

F32 = jnp.float32
BF16 = jnp.bfloat16

HEAD_DIM = 64
N_KV_HEADS = 4
GQA_GROUP = 4
GRID_W = 64
WINDOW = 128
ROPE_BASE = 10000.0
N_FREQ = HEAD_DIM // 4
EPS = 1e-6
NEG_INF = -1e30
LOG2E = math.log2(math.e)
Q_SCALE = HEAD_DIM ** -0.5 * LOG2E
FFN_RESIDUAL = 0.5
LANES = 128
MXU_DIM = 256
SUM_ROWS = 16
PIPE_SLOTS = 16
LOOKAHEAD = 1
SCORE_BUFS = LOOKAHEAD + 1
V7X_VMEM_BYTES = 64 * 1024 * 1024
SUBLANES = 8


def _tile_bytes(shape, dtype):
    item = jnp.dtype(dtype).itemsize
    dims = [1 if s is None else s for s in shape]
    dims[-1] = pl.cdiv(dims[-1], LANES) * LANES
    if len(dims) > 1:
        rows = SUBLANES * 4 // item
        dims[-2] = pl.cdiv(dims[-2], rows) * rows
    return math.prod(dims) * item


def _pallas(body, *, name, grid, sem, in_specs, args, out_specs, out_shape, scratch=(), temps=0):
    outs = out_shape if isinstance(out_shape, (list, tuple)) else [out_shape]
    o_specs = out_specs if isinstance(out_specs, (list, tuple)) else [out_specs]
    need = temps + sum(_tile_bytes(s.shape, s.dtype) for s in scratch)
    for spec, a in list(zip(in_specs, args)) + list(zip(o_specs, outs)):
        n_buf = spec.pipeline_mode.buffer_count if spec.pipeline_mode is not None else 2
        need += n_buf * _tile_bytes(spec.block_shape, a.dtype)
    return pl.pallas_call(
        body, grid=grid, in_specs=in_specs, out_specs=out_specs, out_shape=out_shape,
        scratch_shapes=list(scratch), name=name,
        compiler_params=pltpu.CompilerParams(dimension_semantics=sem,
                                             vmem_limit_bytes=min(need, V7X_VMEM_BYTES)),
    )(*args)


def _silu(a):
    return a * jax.nn.sigmoid(a)


def _norm_mod(x, g, shift, scale):
    y = x * lax.rsqrt(jnp.mean(x * x, axis=-1, keepdims=True) + EPS)
    return (y * g) * (1.0 + scale) + shift


def _ada_kernel(c_ref, w_ref, b_ref, o_ref):
    s = _silu(c_ref[...])
    o_ref[...] = jnp.dot(s, w_ref[...], preferred_element_type=F32) + b_ref[...]


def _ada_all(c8, w_ada, b_ada):
    depth, d, nd = w_ada.shape
    rows = c8.shape[0]
    tn = nd // 8
    return _pallas(
        _ada_kernel, name="adaln", grid=(depth, nd // tn), sem=("parallel", "parallel"),
        in_specs=[
            pl.BlockSpec((rows, d), lambda i, j: (0, 0)),
            pl.BlockSpec((None, d, tn), lambda i, j: (i, 0, j)),
            pl.BlockSpec((None, 1, tn), lambda i, j: (i, 0, j)),
        ],
        args=[c8, w_ada, b_ada.reshape(depth, 1, nd)],
        out_specs=pl.BlockSpec((None, rows, tn), lambda i, j: (i, 0, j)),
        out_shape=jax.ShapeDtypeStruct((depth, rows, nd), F32),
        temps=_tile_bytes((rows, d), F32) + 2 * _tile_bytes((rows, tn), F32))


def _ffn_kernel(*refs, pre):
    if pre:
        x_ref, ot_ref, wo_ref, modp_ref, mod_ref, g_ref, wa_ref, wu_ref, wout_ref, o_ref = refs
    else:
        x_ref, mod_ref, g_ref, wa_ref, wu_ref, wout_ref, o_ref = refs
    x = x_ref[...]
    if pre:
        x = x + modp_ref[2:3, :] * pl.dot(ot_ref[...], wo_ref[...], trans_a=True)
    h = _norm_mod(x, g_ref[...], mod_ref[0:1, :], mod_ref[1:2, :]).astype(BF16)
    a = jnp.dot(h, wa_ref[...], preferred_element_type=F32)
    u = jnp.dot(h, wu_ref[...], preferred_element_type=F32)
    act = (_silu(a) * u).astype(BF16)
    y = jnp.dot(act, wout_ref[...], preferred_element_type=F32)
    o_ref[...] = x + (FFN_RESIDUAL * mod_ref[2:3, :]) * y


def _ffn(x2, mod5, norm_g4, w_in, w_out, layer, half, *, tm, rows_per_batch, ctx_row, pre=None):
    r, d = x2.shape
    dff = w_out.shape[2]
    tpb = rows_per_batch // tm
    k = 2 * half
    resident = dict(pipeline_mode=pl.Buffered(1))

    def bidx(i):
        return ctx_row if ctx_row is not None else i // tpb

    def mod_spec(kk):
        return pl.BlockSpec((None, None, None, 3, d), lambda i: (layer, bidx(i), kk, 0, 0))

    in_specs = [pl.BlockSpec((tm, d), lambda i: (i, 0))]
    args = [x2]
    if pre is not None:
        ot, wo = pre
        in_specs += [
            pl.BlockSpec((None, d, tm), lambda i: (i // tpb, 0, i % tpb)),
            pl.BlockSpec((None, d, d), lambda i: (layer, 0, 0), **resident),
            mod_spec(1),
        ]
        args += [ot, wo, mod5]
    in_specs += [
        mod_spec(k),
        pl.BlockSpec((None, None, 1, d), lambda i: (layer, k, 0, 0)),
        pl.BlockSpec((None, None, d, dff), lambda i: (layer, half, 0, 0), **resident),
        pl.BlockSpec((None, None, d, dff), lambda i: (layer, half, 0, 1), **resident),
        pl.BlockSpec((None, None, dff, d), lambda i: (layer, half, 0, 0), **resident),
    ]
    args += [mod5, norm_g4, w_in, w_in, w_out]
    temps = tm * dff * (4 + 4 + 2) + tm * d * (4 + 4 + 2)
    return _pallas(
        functools.partial(_ffn_kernel, pre=pre is not None),
        name="ffn_pre" if pre is not None else "ffn", grid=(r // tm,), sem=("parallel",),
        in_specs=in_specs, args=args,
        out_specs=pl.BlockSpec((tm, d), lambda i: (i, 0)),
        out_shape=jax.ShapeDtypeStruct((r, d), F32), temps=temps)


def _headnorm_rope(z, gain, cos, sin, bd, rope):
    tm, w = z.shape
    lane = lax.broadcasted_iota(jnp.int32, (1, LANES), 1)
    first_half = (lane & (2 * N_FREQ - 1)) < N_FREQ
    outs = []
    for j in range(w // MXU_DIM):
        zj = z[:, j * MXU_DIM:(j + 1) * MXU_DIM]
        sq = zj * zj
        hi = sq.astype(BF16)
        lo = (sq - hi.astype(F32)).astype(BF16)
        ss = (jnp.dot(hi, bd, preferred_element_type=F32)
              + jnp.dot(lo, bd, preferred_element_type=F32))
        zn = zj * lax.rsqrt(ss * (1.0 / HEAD_DIM) + EPS)
        for half in range(MXU_DIM // LANES):
            t = zn[:, half * LANES:(half + 1) * LANES] * gain
            if rope:
                partner = jnp.where(first_half, pltpu.roll(t, LANES - N_FREQ, 1),
                                    pltpu.roll(t, N_FREQ, 1))
                t = t * cos + partner * sin
            outs.append(t)
    return jnp.concatenate(outs, axis=1)


def _proj_kernel(*refs, rope):
    if rope:
        (x_ref, mod_ref, g_ref, wq_ref, wk_ref, wvt_ref, gq_ref, gk_ref, bd_ref, cos_ref,
         sin_ref, q_ref, k_ref, vt_ref) = refs
        cos, sin = cos_ref[...], sin_ref[...]
    else:
        (x_ref, mod_ref, g_ref, wq_ref, wk_ref, wvt_ref, gq_ref, gk_ref, bd_ref,
         q_ref, k_ref, vt_ref) = refs
        cos = sin = None
    h = _norm_mod(x_ref[...], g_ref[...], mod_ref[0:1, :], mod_ref[1:2, :]).astype(BF16)
    bd = bd_ref[...]
    q = jnp.dot(h, wq_ref[...], preferred_element_type=F32)
    q_ref[...] = _headnorm_rope(q, gq_ref[...] * Q_SCALE, cos, sin, bd, rope).astype(BF16)
    k = jnp.dot(h, wk_ref[...], preferred_element_type=F32)
    k_ref[...] = _headnorm_rope(k, gk_ref[...], cos, sin, bd, rope).astype(BF16)
    vt = pl.dot(wvt_ref[...], h, trans_b=True).astype(BF16)
    tkv = vt_ref.shape[2]
    for c in range(vt_ref.shape[0]):
        vt_ref[c] = vt[:, c * tkv:(c + 1) * tkv]


def _proj(x2, mod5, norm_g4, layer, wq, wk, wvt, gq, gk, bd, rope_tabs, *, tm, rows_per_batch,
          ctx_row, tkv):
    r, d = x2.shape
    wq_n, wk_n, wv_n = wq.shape[1], wk.shape[1], wvt.shape[0]
    tpb = rows_per_batch // tm
    kpt = tm // tkv
    nb = r // rows_per_batch
    rope = rope_tabs is not None

    def bidx(i):
        return ctx_row if ctx_row is not None else i // tpb

    const = lambda i: (0, 0)
    in_specs = [
        pl.BlockSpec((tm, d), lambda i: (i, 0)),
        pl.BlockSpec((None, None, None, 3, d), lambda i: (layer, bidx(i), 1, 0, 0)),
        pl.BlockSpec((None, None, 1, d), lambda i: (layer, 1, 0, 0)),
        pl.BlockSpec((d, wq_n), const),
        pl.BlockSpec((d, wk_n), const),
        pl.BlockSpec((wv_n, d), const),
        pl.BlockSpec((1, LANES), const),
        pl.BlockSpec((1, LANES), const),
        pl.BlockSpec((MXU_DIM, MXU_DIM), const),
    ]
    args = [x2, mod5, norm_g4, wq, wk, wvt, gq, gk, bd]
    if rope:
        in_specs += [pl.BlockSpec((tm, LANES), lambda i: (i % tpb, 0))] * 2
        args += list(rope_tabs)
    temps = tm * d * (4 + 2) + 2 * tm * (wq_n + wk_n + wv_n) * 4
    return _pallas(
        functools.partial(_proj_kernel, rope=rope), name="qkv_proj", grid=(r // tm,),
        sem=("parallel",), in_specs=in_specs, args=args,
        out_specs=[
            pl.BlockSpec((tm, wq_n), lambda i: (i, 0)),
            pl.BlockSpec((tm, wk_n), lambda i: (i, 0)),
            pl.BlockSpec((None, kpt, wv_n, tkv), lambda i: (i // tpb, i % tpb, 0, 0)),
        ],
        out_shape=[
            jax.ShapeDtypeStruct((r, wq_n), BF16),
            jax.ShapeDtypeStruct((r, wk_n), BF16),
            jax.ShapeDtypeStruct((nb, tpb * kpt, wv_n, tkv), BF16),
        ],
        temps=temps)


def _slot_mask(slot):
    lane = lax.broadcasted_iota(jnp.int32, (1, LANES), 1)
    return (lane >= slot * HEAD_DIM) & (lane < (slot + 1) * HEAD_DIM)


def _gqa_q_tile(q_ref, kvh, g):
    j = g + GQA_GROUP * (kvh // 2)
    tile = q_ref[:, j * LANES:(j + 1) * LANES]
    return jnp.where(_slot_mask(kvh % 2), tile, jnp.zeros_like(tile))


_GQA_UNITS = tuple(
    (kvh // 2, kvh * HEAD_DIM,
     tuple((g + GQA_GROUP * (kvh // 2), kvh % 2) for g in range(GQA_GROUP)))
    for kvh in range(N_KV_HEADS))
_DIFF_UNITS = tuple((h, h * 2 * HEAD_DIM, ((h, 0), (h, 1))) for h in range(2))


def _flash_kernel(*refs, units, dv, tq, tk, n_kb, mode, has_sink, lam_init):
    refs = list(refs)
    q_ref, kc_ref, vtc_ref = refs[:3]
    pos = 3
    has_lat = n_kb > 0
    if has_lat:
        kl_ref, vtl_ref = refs[pos:pos + 2]
        pos += 2
    if mode == "diff":
        lam_ref, subln_ref = refs[pos:pos + 2]
        pos += 2
    elif has_sink:
        sink_ref = refs[pos]
        pos += 1
    o_ref, qs_scr, m_scr, acc_scr = refs[pos:pos + 4]
    if has_lat:
        s_scr, mx_scr = refs[pos + 4:]
    n_u = len(units)

    def pv(vt, p):
        ones = jnp.ones((SUM_ROWS, vt.shape[1]), BF16)
        return jnp.dot(jnp.concatenate([vt, ones], axis=0), p.astype(BF16),
                       preferred_element_type=F32)

    for u, (_, _, q_tiles) in enumerate(units):
        for t, (j, slot) in enumerate(q_tiles):
            tile = q_ref[:, j * LANES:(j + 1) * LANES]
            tile = jnp.where(_slot_mask(slot), tile, jnp.zeros_like(tile))
            qs_scr[u, :, t * tq:(t + 1) * tq] = tile.T

    for u, (half, v0, _) in enumerate(units):
        s = jnp.dot(kc_ref[:, half * LANES:(half + 1) * LANES], qs_scr[u],
                    preferred_element_type=F32)
        m = jnp.max(s, axis=0, keepdims=True)
        p = jnp.exp2(s - m)
        m_scr[u] = m
        acc_scr[u] = pv(vtc_ref[v0:v0 + dv, :], p)

    if has_lat:
        col_tiles = [slice(c0, c0 + MXU_DIM) for c0 in range(0, s_scr.shape[2], MXU_DIM)]

        key_halves = [slice(r0, r0 + MXU_DIM) for r0 in range(0, tk, MXU_DIM)]

        def scores(kb, u, slot, cs, rs):
            half = units[u][0]
            row0 = pl.multiple_of(kb * tk, tk) + rs.start
            k = kl_ref[pl.ds(row0, MXU_DIM), half * LANES:(half + 1) * LANES]
            s = jnp.dot(k, qs_scr[u, :, cs], preferred_element_type=F32)
            s_scr[slot, rs, cs] = s
            return jnp.max(s, axis=0, keepdims=True)

        def probs(u, slot, cs):
            m_prev = m_scr[u, :, cs]
            m_new = jnp.maximum(m_prev, mx_scr[slot, :, cs])
            m_scr[u, :, cs] = m_new
            return (jnp.exp2(s_scr[slot, :, cs] - m_new).astype(BF16),
                    jnp.exp2(m_prev - m_new))

        def pv_part(kb, u, p, rs):
            v0 = units[u][1]
            return pv(vtl_ref[kb, v0:v0 + dv, rs], p[rs, :])

        bpt = PIPE_SLOTS // n_u
        n_trips = n_kb // bpt

        def trip(it, last):
            pending = None
            for j in range(PIPE_SLOTS):
                ahead = j + LOOKAHEAD
                if ahead < PIPE_SLOTS:
                    sc = (it * bpt + ahead // n_u, ahead % n_u, ahead % SCORE_BUFS)
                elif not last:
                    a2 = ahead - PIPE_SLOTS
                    sc = ((it + 1) * bpt + a2 // n_u, a2 % n_u, a2 % SCORE_BUFS)
                else:
                    sc = None
                for cs in col_tiles:
                    fresh = (it * bpt + j // n_u, j % n_u, cs,
                             *probs(j % n_u, j % SCORE_BUFS, cs))
                    mx, parts = None, []
                    for rs in key_halves:
                        if sc is not None:
                            part = scores(*sc, cs, rs)
                            mx = part if mx is None else jnp.maximum(mx, part)
                        if pending is not None:
                            parts.append(pv_part(pending[0], pending[1], pending[3], rs))
                    if sc is not None:
                        mx_scr[sc[2], :, cs] = mx
                    if pending is not None:
                        _, u_p, cs_p, _, alpha_p = pending
                        acc_scr[u_p, :, cs_p] = alpha_p * acc_scr[u_p, :, cs_p] + sum(parts)
                    pending = fresh
            kb_p, u_p, cs_p, p, alpha_p = pending
            acc_scr[u_p, :, cs_p] = alpha_p * acc_scr[u_p, :, cs_p] + sum(
                pv_part(kb_p, u_p, p, rs) for rs in key_halves)

        for j in range(LOOKAHEAD):
            for cs in col_tiles:
                mx_scr[j, :, cs] = functools.reduce(
                    jnp.maximum, [scores(j // n_u, j % n_u, j, cs, rs) for rs in key_halves])
        lax.fori_loop(0, n_trips - 1, lambda it, c: (trip(it, False), c)[1], 0)
        trip(n_trips - 1, True)

    if mode == "diff":
        lp = lam_ref[...]
        lam = (jnp.exp(jnp.sum(lp[0:1] * lp[1:2], axis=1, keepdims=True))
               - jnp.exp(jnp.sum(lp[2:3] * lp[3:4], axis=1, keepdims=True)) + lam_init)
        gain = subln_ref[...]
        for u in range(n_u):
            o12 = acc_scr[u, :dv, :] / acc_scr[u, dv:dv + 1, :]
            o = o12[:, :tq] - lam * o12[:, tq:]
            o = o * lax.rsqrt(jnp.mean(o * o, axis=0, keepdims=True) + EPS)
            o_ref[u * dv:(u + 1) * dv, :] = ((o * gain) * (1.0 - lam_init)).astype(BF16)
    else:
        for u in range(n_u):
            m, l, acc = m_scr[u], acc_scr[u, dv:dv + 1, :], acc_scr[u, :dv, :]
            if has_sink:
                sk = sink_ref[u] * LOG2E
                m_f = jnp.maximum(m, sk)
                w = jnp.exp2(m - m_f)
                l = l * w + jnp.exp2(sk - m_f)
                acc = acc * w
            o = acc / l
            for g in range(GQA_GROUP):
                row = (u * GQA_GROUP + g) * HEAD_DIM
                o_ref[row:row + HEAD_DIM, :] = o[:, g * tq:(g + 1) * tq].astype(BF16)


def _flash_attn(q, kc, vtc, kl, vtl, *, mode, n_batch, tq, sink_rows=None, lam_params=None,
                subln=None, lam_init=0.0):
    r, dq = q.shape
    t_q = r // n_batch
    c = kc.shape[0] // n_batch
    wk = kc.shape[1]
    n_grp = wk // MXU_DIM
    q_cols = dq // n_grp
    units, dv = (_GQA_UNITS, HEAD_DIM) if mode == "gqa" else (_DIFF_UNITS, 2 * HEAD_DIM)
    cols = len(units[0][2]) * tq
    has_lat = kl is not None
    nq = t_q // tq
    in_specs = [
        pl.BlockSpec((tq, q_cols), lambda b, g, i: (b * nq + i, g)),
        pl.BlockSpec((c, MXU_DIM), lambda b, g, i: (b, g)),
        pl.BlockSpec((None, MXU_DIM, c), lambda b, g, i: (b, g, 0)),
    ]
    args = [q, kc, vtc]
    n_kb, tk = 0, 0
    scratch = [
        pltpu.VMEM((len(units), LANES, cols), BF16),
        pltpu.VMEM((len(units), 1, cols), F32),
        pltpu.VMEM((len(units), dv + SUM_ROWS, cols), F32),
    ]
    if has_lat:
        t = kl.shape[0] // n_batch
        n_kb, tk = vtl.shape[1], vtl.shape[3]
        in_specs += [
            pl.BlockSpec((t, MXU_DIM), lambda b, g, i: (b, g)),
            pl.BlockSpec((None, n_kb, MXU_DIM, tk), lambda b, g, i: (b, 0, g, 0)),
        ]
        args += [kl, vtl]
        assert PIPE_SLOTS % len(units) == 0 and n_kb % (PIPE_SLOTS // len(units)) == 0
        scratch += [pltpu.VMEM((SCORE_BUFS, tk, cols), F32),
                    pltpu.VMEM((SCORE_BUFS, 1, cols), F32)]
    if mode == "diff":
        in_specs += [
            pl.BlockSpec((4, HEAD_DIM), lambda b, g, i: (0, 0)),
            pl.BlockSpec((dv, 1), lambda b, g, i: (0, 0)),
        ]
        args += [lam_params, subln.reshape(dv, 1)]
    elif sink_rows is not None:
        in_specs.append(pl.BlockSpec((N_KV_HEADS, 1, cols), lambda b, g, i: (0, 0, 0)))
        args.append(sink_rows)
    temps = c * cols * (4 + 4 + 2) + 4 * max(tk, c) * MXU_DIM * (4 + 4 + 2)
    return _pallas(
        functools.partial(_flash_kernel, units=units, dv=dv, tq=tq, tk=tk, n_kb=n_kb, mode=mode,
                          has_sink=sink_rows is not None, lam_init=lam_init),
        name=mode + "_attn", grid=(n_batch, n_grp, nq),
        sem=("parallel", "parallel", "arbitrary"), in_specs=in_specs, args=args,
        out_specs=pl.BlockSpec((None, q_cols, tq), lambda b, g, i: (b, g, i)),
        out_shape=jax.ShapeDtypeStruct((n_batch, dq, t_q), BF16),
        scratch=scratch, temps=temps)


def _window_kernel(q_ref, kc_ref, vtc_ref, kp_ref, kcur_ref, kn_ref, vtp_ref, vtcur_ref,
                   vtn_ref, sink_ref, o_ref, *, tq):
    qi = pl.program_id(1)
    nq = pl.num_programs(1)
    n_lat = tq + 2 * WINDOW
    rr = lax.broadcasted_iota(jnp.int32, (n_lat, 1), 0)
    cc = lax.broadcasted_iota(jnp.int32, (1, tq), 1)
    valid = jnp.abs(rr - WINDOW - cc) <= WINDOW
    valid &= (rr >= WINDOW) | (qi > 0)
    valid &= (rr < tq + WINDOW) | (qi < nq - 1)
    bias = jnp.where(valid, 0.0, NEG_INF)

    k_lat = [jnp.concatenate([r[:, h * LANES:(h + 1) * LANES] for r in (kp_ref, kcur_ref, kn_ref)],
                             axis=0) for h in range(2)]

    def pv(vt, p):
        ones = jnp.ones((SUM_ROWS, vt.shape[1]), BF16)
        return jnp.dot(jnp.concatenate([vt, ones], axis=0), p, preferred_element_type=F32)

    def scores(kvh, g):
        half = kvh // 2
        qt = _gqa_q_tile(q_ref, kvh, g).T
        s_ctx = jnp.dot(kc_ref[:, half * LANES:(half + 1) * LANES], qt,
                        preferred_element_type=F32)
        s_lat = jnp.dot(k_lat[half], qt, preferred_element_type=F32) + bias
        sk = sink_ref[kvh, :, g * tq:(g + 1) * tq] * LOG2E
        m = jnp.maximum(jnp.maximum(jnp.max(s_ctx, axis=0, keepdims=True),
                                    jnp.max(s_lat, axis=0, keepdims=True)), sk)
        return s_ctx, s_lat, m, jnp.exp2(sk - m)

    def probs(s_ctx, s_lat, m, p_sink):
        return jnp.exp2(s_ctx - m).astype(BF16), jnp.exp2(s_lat - m).astype(BF16), p_sink

    def output(kvh, g, p_ctx, p_lat, p_sink):
        hs = slice(kvh * HEAD_DIM, (kvh + 1) * HEAD_DIM)
        vt_lat = jnp.concatenate([vtp_ref[hs, :], vtcur_ref[hs, :], vtn_ref[hs, :]], axis=1)
        acc = pv(vtc_ref[hs, :], p_ctx) + pv(vt_lat, p_lat)
        l = acc[HEAD_DIM:HEAD_DIM + 1, :] + p_sink
        row = (kvh * GQA_GROUP + g) * HEAD_DIM
        o_ref[row:row + HEAD_DIM, :] = (acc[:HEAD_DIM, :] / l).astype(BF16)

    tiles = [(kvh, g) for kvh in range(N_KV_HEADS) for g in range(GQA_GROUP)]
    scored, pending = scores(*tiles[0]), None
    for t, tile in enumerate(tiles):
        ahead = scores(*tiles[t + 1]) if t + 1 < len(tiles) else None
        fresh = (*tile, *probs(*scored))
        if pending is not None:
            output(*pending)
        scored, pending = ahead, fresh
    output(*pending)


def _window_attn(q, kc, vtc, kl, vtl, sink_rows, *, n_batch, tq):
    r, dq = q.shape
    t = r // n_batch
    c = kc.shape[0] // n_batch
    wk = kc.shape[1]
    tkv = vtl.shape[3]
    nq = t // tq
    rb = tq // WINDOW
    nwb = t // WINDOW
    wpb, qpb = tkv // WINDOW, tkv // tq
    prev = lambda b, i: jnp.maximum(i * rb - 1, 0)
    nxt = lambda b, i: jnp.minimum((i + 1) * rb, nwb - 1)
    in_specs = [
        pl.BlockSpec((tq, dq), lambda b, i: (b * nq + i, 0)),
        pl.BlockSpec((c, wk), lambda b, i: (b, 0)),
        pl.BlockSpec((None, wk, c), lambda b, i: (b, 0, 0)),
        pl.BlockSpec((WINDOW, wk), lambda b, i: (b * nwb + prev(b, i), 0)),
        pl.BlockSpec((tq, wk), lambda b, i: (b * nq + i, 0)),
        pl.BlockSpec((WINDOW, wk), lambda b, i: (b * nwb + nxt(b, i), 0)),
        pl.BlockSpec((None, None, wk, WINDOW),
                     lambda b, i: (b, prev(b, i) // wpb, 0, prev(b, i) % wpb)),
        pl.BlockSpec((None, None, wk, tq), lambda b, i: (b, i // qpb, 0, i % qpb)),
        pl.BlockSpec((None, None, wk, WINDOW),
                     lambda b, i: (b, nxt(b, i) // wpb, 0, nxt(b, i) % wpb)),
        pl.BlockSpec((N_KV_HEADS, 1, GQA_GROUP * tq), lambda b, i: (0, 0, 0)),
    ]
    n_keys = c + tq + 2 * WINDOW
    temps = n_keys * GQA_GROUP * tq * (4 + 4 + 2) + GQA_GROUP * tq * LANES * 2
    return _pallas(
        functools.partial(_window_kernel, tq=tq), name="window_attn", grid=(n_batch, nq),
        sem=("parallel", "parallel"), in_specs=in_specs,
        args=[q, kc, vtc, kl, kl, kl, vtl, vtl, vtl, sink_rows],
        out_specs=pl.BlockSpec((None, dq, tq), lambda b, i: (b, 0, i)),
        out_shape=jax.ShapeDtypeStruct((n_batch, dq, t), BF16), temps=temps)


def _rope_tables(n_tokens):
    rows = n_tokens // GRID_W
    row = jnp.repeat(jnp.arange(rows), GRID_W)
    col = jnp.tile(jnp.arange(GRID_W), rows)
    inv_freq = ROPE_BASE ** (-jnp.arange(N_FREQ, dtype=F32) / N_FREQ)
    ang = jnp.stack([row, col], axis=-1).astype(F32)[:, :, None] * inv_freq
    cos, sin = jnp.cos(ang), jnp.sin(ang)
    cos_h = jnp.concatenate([cos, cos], axis=-1).reshape(n_tokens, HEAD_DIM)
    sin_h = jnp.concatenate([-sin, sin], axis=-1).reshape(n_tokens, HEAD_DIM)
    return jnp.tile(cos_h, (1, 2)), jnp.tile(sin_h, (1, 2))


def _gqa_weights(w_qkv):
    d = w_qkv.shape[0]
    n_q = N_KV_HEADS * GQA_GROUP * HEAD_DIM
    n_kv = N_KV_HEADS * HEAD_DIM
    wq = w_qkv[:, :n_q].reshape(d, N_KV_HEADS // 2, 2, GQA_GROUP, HEAD_DIM)
    wq = wq.transpose(0, 1, 3, 2, 4).reshape(d, n_q)
    wk = w_qkv[:, n_q:n_q + n_kv]
    wvt = w_qkv[:, n_q + n_kv:].T
    return wq.astype(BF16), wk.astype(BF16), wvt.astype(BF16)


def _diff_weights(w_qkv):
    n = w_qkv.shape[1] // 3
    return (w_qkv[:, :n].astype(BF16), w_qkv[:, n:2 * n].astype(BF16),
            w_qkv[:, 2 * n:].T.astype(BF16))


def _lane_gain(g):
    return jnp.tile(g.reshape(1, HEAD_DIM), (1, LANES // HEAD_DIM))


def _sink_rows(sink, tq):
    return jnp.repeat(sink.reshape(N_KV_HEADS, 1, GQA_GROUP, 1), tq, axis=3).reshape(
        N_KV_HEADS, 1, GQA_GROUP * tq)


def kernel(x, c, ctx, c_ctx, norm_g, w_ada, b_ada, w_ffn_in, w_ffn_out, w_o, w_qkv_a, qk_norm_a,
           w_qkv_b, qk_norm_b, sink_b, w_qkv_c, qk_norm_c, diff_lambda, diff_subln):
    n_b, t, d = x.shape
    n_c = ctx.shape[1]
    depth = w_ada.shape[0]
    tm = 512
    tm_ffn = 1024
    tq, tq_diff = 1024, 2048
    tq_win = 256
    tq_c = n_c
    tkv = 512
    assert n_b + 1 <= SUBLANES and n_c % LANES == 0
    assert all(t % tile == 0 for tile in (tm, tm_ffn, tq, tq_diff, tq_win, tkv, GRID_W))

    c8 = jnp.zeros((SUBLANES, d), F32).at[:n_b].set(c).at[n_b].set(c_ctx)
    mod5 = _ada_all(c8, w_ada, b_ada).reshape(depth, SUBLANES, 3, 3, d)

    rope_tabs = _rope_tables(t)
    eye = jnp.arange(MXU_DIM) // HEAD_DIM
    bd = (eye[:, None] == eye[None, :]).astype(BF16)
    w_in = w_ffn_in.astype(BF16)
    w_out = w_ffn_out.astype(BF16)
    wo = w_o.astype(BF16)
    norm_g4 = norm_g.reshape(depth, 3, 1, d)

    xs = x.reshape(n_b * t, d)
    cs = ctx.reshape(n_b * n_c, d)
    lat = dict(tm=tm, rows_per_batch=t, ctx_row=None)
    lat_ffn = dict(tm=tm_ffn, rows_per_batch=t, ctx_row=None)
    cx = dict(tm=n_c, rows_per_batch=n_c, ctx_row=n_b)

    for i in range(depth):
        last = i == depth - 1
        kind, j = i % 3, i // 3
        xs = _ffn(xs, mod5, norm_g4, w_in, w_out, i, 0, **lat_ffn)
        cs = _ffn(cs, mod5, norm_g4, w_in, w_out, i, 0, **cx)
        if kind == 2:
            wq, wk, wvt = _diff_weights(w_qkv_c[j])
            qk_g = qk_norm_c[j]
        else:
            wq, wk, wvt = _gqa_weights((w_qkv_a, w_qkv_b)[kind][j])
            qk_g = (qk_norm_a, qk_norm_b)[kind][j]
        gq, gk = _lane_gain(qk_g[0]), _lane_gain(qk_g[1])
        q, kl, vtl = _proj(xs, mod5, norm_g4, i, wq, wk, wvt, gq, gk, bd, rope_tabs, tkv=tkv,
                           **lat)
        qc, kc, vtc = _proj(cs, mod5, norm_g4, i, wq, wk, wvt, gq, gk, bd, None, tkv=n_c, **cx)
        vtc = vtc.reshape(n_b, vtc.shape[2], n_c)
        if kind == 0:
            ot = _flash_attn(q, kc, vtc, kl, vtl, mode="gqa", n_batch=n_b, tq=tq)
            if not last:
                otc = _flash_attn(qc, kc, vtc, None, None, mode="gqa", n_batch=n_b, tq=tq_c)
        elif kind == 1:
            ot = _window_attn(q, kc, vtc, kl, vtl, _sink_rows(sink_b[j], tq_win), n_batch=n_b,
                              tq=tq_win)
            if not last:
                otc = _flash_attn(qc, kc, vtc, None, None, mode="gqa", n_batch=n_b, tq=tq_c,
                                  sink_rows=_sink_rows(sink_b[j], tq_c))
        else:
            extra = dict(lam_params=diff_lambda[j], subln=diff_subln[j],
                         lam_init=0.8 - 0.6 * math.exp(-0.3 * i))
            ot = _flash_attn(q, kc, vtc, kl, vtl, mode="diff", n_batch=n_b, tq=tq_diff, **extra)
            if not last:
                otc = _flash_attn(qc, kc, vtc, None, None, mode="diff", n_batch=n_b, tq=tq_c,
                                  **extra)
        xs = _ffn(xs, mod5, norm_g4, w_in, w_out, i, 1, pre=(ot, wo), **lat_ffn)
        if not last:
            cs = _ffn(cs, mod5, norm_g4, w_in, w_out, i, 1, pre=(otc, wo), **cx)
    return xs.reshape(n_b, t, d)
```

```python
import functools
import math

import jax
import jax.numpy as jnp
from jax import lax
from jax.experimental import pallas as pl
from jax.experimental.pallas import tpu as pltpu

F32 = jnp.float32
BF16 = jnp.bfloat16

HEAD_DIM = 64
N_KV_HEADS = 4
GQA_GROUP = 4
GRID_W = 64
WINDOW = 128
ROPE_BASE = 10000.0
N_FREQ = HEAD_DIM // 4
EPS = 1e-6
NEG_INF = -1e30
LOG2E = math.log2(math.e)
Q_SCALE = HEAD_DIM ** -0.5 * LOG2E
FFN_RESIDUAL = 0.5
LANES = 128
MXU_DIM = 256
SUM_ROWS = 16
PIPE_SLOTS = 8
LOOKAHEAD = 1
SCORE_BUFS = LOOKAHEAD + 1
V7X_VMEM_BYTES = 64 * 1024 * 1024
SUBLANES = 8


def _tile_bytes(shape, dtype):
    item = jnp.dtype(dtype).itemsize
    dims = [1 if s is None else s for s in shape]
    dims[-1] = pl.cdiv(dims[-1], LANES) * LANES
    if len(dims) > 1:
        rows = SUBLANES * 4 // item
        dims[-2] = pl.cdiv(dims[-2], rows) * rows
    return math.prod(dims) * item


def _pallas(body, *, name, grid, sem, in_specs, args, out_specs, out_shape, scratch=(), temps=0):
    outs = out_shape if isinstance(out_shape, (list, tuple)) else [out_shape]
    o_specs = out_specs if isinstance(out_specs, (list, tuple)) else [out_specs]
    need = temps + sum(_tile_bytes(s.shape, s.dtype) for s in scratch)
    for spec, a in list(zip(in_specs, args)) + list(zip(o_specs, outs)):
        n_buf = spec.pipeline_mode.buffer_count if spec.pipeline_mode is not None else 2
        need += n_buf * _tile_bytes(spec.block_shape, a.dtype)
    return pl.pallas_call(
        body, grid=grid, in_specs=in_specs, out_specs=out_specs, out_shape=out_shape,
        scratch_shapes=list(scratch), name=name,
        compiler_params=pltpu.CompilerParams(dimension_semantics=sem,
                                             vmem_limit_bytes=min(need, V7X_VMEM_BYTES)),
    )(*args)


def _silu(a):
    return a * jax.nn.sigmoid(a)


def _norm_mod(x, g, shift, scale):
    y = x * lax.rsqrt(jnp.mean(x * x, axis=-1, keepdims=True) + EPS)
    return (y * g) * (1.0 + scale) + shift


def _ada_kernel(c_ref, w_ref, b_ref, o_ref):
    s = _silu(c_ref[...])
    o_ref[...] = jnp.dot(s, w_ref[...], preferred_element_type=F32) + b_ref[...]


def _ada_all(c8, w_ada, b_ada):
    depth, d, nd = w_ada.shape
    rows = c8.shape[0]
    tn = nd // 8
    return _pallas(
        _ada_kernel, name="adaln", grid=(depth, nd // tn), sem=("parallel", "parallel"),
        in_specs=[
            pl.BlockSpec((rows, d), lambda i, j: (0, 0)),
            pl.BlockSpec((None, d, tn), lambda i, j: (i, 0, j)),
            pl.BlockSpec((None, 1, tn), lambda i, j: (i, 0, j)),
        ],
        args=[c8, w_ada, b_ada.reshape(depth, 1, nd)],
        out_specs=pl.BlockSpec((None, rows, tn), lambda i, j: (i, 0, j)),
        out_shape=jax.ShapeDtypeStruct((depth, rows, nd), F32),
        temps=_tile_bytes((rows, d), F32) + 2 * _tile_bytes((rows, tn), F32))


def _ffn_kernel(*refs, pre):
    if pre:
        x_ref, ot_ref, wo_ref, modp_ref, mod_ref, g_ref, wa_ref, wu_ref, wout_ref, o_ref = refs
    else:
        x_ref, mod_ref, g_ref, wa_ref, wu_ref, wout_ref, o_ref = refs
    x = x_ref[...]
    if pre:
        x = x + modp_ref[2:3, :] * pl.dot(ot_ref[...], wo_ref[...], trans_a=True)
    h = _norm_mod(x, g_ref[...], mod_ref[0:1, :], mod_ref[1:2, :]).astype(BF16)
    a = jnp.dot(h, wa_ref[...], preferred_element_type=F32)
    u = jnp.dot(h, wu_ref[...], preferred_element_type=F32)
    act = (_silu(a) * u).astype(BF16)
    y = jnp.dot(act, wout_ref[...], preferred_element_type=F32)
    o_ref[...] = x + (FFN_RESIDUAL * mod_ref[2:3, :]) * y


def _ffn(x2, mod5, norm_g4, w_in, w_out, layer, half, *, tm, rows_per_batch, ctx_row, pre=None):
    r, d = x2.shape
    dff = w_out.shape[2]
    tpb = rows_per_batch // tm
    k = 2 * half
    resident = dict(pipeline_mode=pl.Buffered(1))

    def bidx(i):
        return ctx_row if ctx_row is not None else i // tpb

    def mod_spec(kk):
        return pl.BlockSpec((None, None, None, 3, d), lambda i: (layer, bidx(i), kk, 0, 0))

    in_specs = [pl.BlockSpec((tm, d), lambda i: (i, 0))]
    args = [x2]
    if pre is not None:
        ot, wo = pre
        in_specs += [
            pl.BlockSpec((None, d, tm), lambda i: (i // tpb, 0, i % tpb)),
            pl.BlockSpec((None, d, d), lambda i: (layer, 0, 0), **resident),
            mod_spec(1),
        ]
        args += [ot, wo, mod5]
    in_specs += [
        mod_spec(k),
        pl.BlockSpec((None, None, 1, d), lambda i: (layer, k, 0, 0)),
        pl.BlockSpec((None, None, d, dff), lambda i: (layer, half, 0, 0), **resident),
        pl.BlockSpec((None, None, d, dff), lambda i: (layer, half, 0, 1), **resident),
        pl.BlockSpec((None, None, dff, d), lambda i: (layer, half, 0, 0), **resident),
    ]
    args += [mod5, norm_g4, w_in, w_in, w_out]
    temps = tm * dff * (4 + 4 + 2) + tm * d * (4 + 4 + 2)
    return _pallas(
        functools.partial(_ffn_kernel, pre=pre is not None),
        name="ffn_pre" if pre is not None else "ffn", grid=(r // tm,), sem=("parallel",),
        in_specs=in_specs, args=args,
        out_specs=pl.BlockSpec((tm, d), lambda i: (i, 0)),
        out_shape=jax.ShapeDtypeStruct((r, d), F32), temps=temps)


def _headnorm_rope(z, gain, cos, sin, bd, rope):
    tm, w = z.shape
    lane = lax.broadcasted_iota(jnp.int32, (1, LANES), 1)
    first_half = (lane & (2 * N_FREQ - 1)) < N_FREQ
    outs = []
    for j in range(w // MXU_DIM):
        zj = z[:, j * MXU_DIM:(j + 1) * MXU_DIM]
        sq = zj * zj
        hi = sq.astype(BF16)
        lo = (sq - hi.astype(F32)).astype(BF16)
        ss = (jnp.dot(hi, bd, preferred_element_type=F32)
              + jnp.dot(lo, bd, preferred_element_type=F32))
        zn = zj * lax.rsqrt(ss * (1.0 / HEAD_DIM) + EPS)
        for half in range(MXU_DIM // LANES):
            t = zn[:, half * LANES:(half + 1) * LANES] * gain
            if rope:
                partner = jnp.where(first_half, pltpu.roll(t, LANES - N_FREQ, 1),
                                    pltpu.roll(t, N_FREQ, 1))
                t = t * cos + partner * sin
            outs.append(t)
    return jnp.concatenate(outs, axis=1)


def _proj_kernel(*refs, rope):
    if rope:
        (x_ref, mod_ref, g_ref, wq_ref, wk_ref, wvt_ref, gq_ref, gk_ref, bd_ref, cos_ref,
         sin_ref, q_ref, k_ref, vt_ref) = refs
        cos, sin = cos_ref[...], sin_ref[...]
    else:
        (x_ref, mod_ref, g_ref, wq_ref, wk_ref, wvt_ref, gq_ref, gk_ref, bd_ref,
         q_ref, k_ref, vt_ref) = refs
        cos = sin = None
    h = _norm_mod(x_ref[...], g_ref[...], mod_ref[0:1, :], mod_ref[1:2, :]).astype(BF16)
    bd = bd_ref[...]
    q = jnp.dot(h, wq_ref[...], preferred_element_type=F32)
    q_ref[...] = _headnorm_rope(q, gq_ref[...] * Q_SCALE, cos, sin, bd, rope).astype(BF16)
    k = jnp.dot(h, wk_ref[...], preferred_element_type=F32)
    k_ref[...] = _headnorm_rope(k, gk_ref[...], cos, sin, bd, rope).astype(BF16)
    vt = pl.dot(wvt_ref[...], h, trans_b=True).astype(BF16)
    tkv = vt_ref.shape[2]
    for c in range(vt_ref.shape[0]):
        vt_ref[c] = vt[:, c * tkv:(c + 1) * tkv]


def _proj(x2, mod5, norm_g4, layer, wq, wk, wvt, gq, gk, bd, rope_tabs, *, tm, rows_per_batch,
          ctx_row, tkv):
    r, d = x2.shape
    wq_n, wk_n, wv_n = wq.shape[1], wk.shape[1], wvt.shape[0]
    tpb = rows_per_batch // tm
    kpt = tm // tkv
    nb = r // rows_per_batch
    rope = rope_tabs is not None

    def bidx(i):
        return ctx_row if ctx_row is not None else i // tpb

    const = lambda i: (0, 0)
    in_specs = [
        pl.BlockSpec((tm, d), lambda i: (i, 0)),
        pl.BlockSpec((None, None, None, 3, d), lambda i: (layer, bidx(i), 1, 0, 0)),
        pl.BlockSpec((None, None, 1, d), lambda i: (layer, 1, 0, 0)),
        pl.BlockSpec((d, wq_n), const),
        pl.BlockSpec((d, wk_n), const),
        pl.BlockSpec((wv_n, d), const),
        pl.BlockSpec((1, LANES), const),
        pl.BlockSpec((1, LANES), const),
        pl.BlockSpec((MXU_DIM, MXU_DIM), const),
    ]
    args = [x2, mod5, norm_g4, wq, wk, wvt, gq, gk, bd]
    if rope:
        in_specs += [pl.BlockSpec((tm, LANES), lambda i: (i % tpb, 0))] * 2
        args += list(rope_tabs)
    temps = tm * d * (4 + 2) + 2 * tm * (wq_n + wk_n + wv_n) * 4
    return _pallas(
        functools.partial(_proj_kernel, rope=rope), name="qkv_proj", grid=(r // tm,),
        sem=("parallel",), in_specs=in_specs, args=args,
        out_specs=[
            pl.BlockSpec((tm, wq_n), lambda i: (i, 0)),
            pl.BlockSpec((tm, wk_n), lambda i: (i, 0)),
            pl.BlockSpec((None, kpt, wv_n, tkv), lambda i: (i // tpb, i % tpb, 0, 0)),
        ],
        out_shape=[
            jax.ShapeDtypeStruct((r, wq_n), BF16),
            jax.ShapeDtypeStruct((r, wk_n), BF16),
            jax.ShapeDtypeStruct((nb, tpb * kpt, wv_n, tkv), BF16),
        ],
        temps=temps)


def _slot_mask(slot):
    lane = lax.broadcasted_iota(jnp.int32, (1, LANES), 1)
    return (lane >= slot * HEAD_DIM) & (lane < (slot + 1) * HEAD_DIM)


def _gqa_q_tile(q_ref, kvh, g):
    j = g + GQA_GROUP * (kvh // 2)
    tile = q_ref[:, j * LANES:(j + 1) * LANES]
    return jnp.where(_slot_mask(kvh % 2), tile, jnp.zeros_like(tile))


_GQA_UNITS = tuple(
    (kvh // 2, kvh * HEAD_DIM,
     tuple((g + GQA_GROUP * (kvh // 2), kvh % 2) for g in range(GQA_GROUP)))
    for kvh in range(N_KV_HEADS))
_DIFF_UNITS = tuple((h, h * 2 * HEAD_DIM, ((h, 0), (h, 1))) for h in range(2))


def _flash_kernel(*refs, units, dv, tq, tk, n_kb, mode, has_sink, lam_init):
    refs = list(refs)
    q_ref, kc_ref, vtc_ref = refs[:3]
    pos = 3
    has_lat = n_kb > 0
    if has_lat:
        kl_ref, vtl_ref = refs[pos:pos + 2]
        pos += 2
    if mode == "diff":
        lam_ref, subln_ref = refs[pos:pos + 2]
        pos += 2
    elif has_sink:
        sink_ref = refs[pos]
        pos += 1
    o_ref, qs_scr, m_scr, acc_scr = refs[pos:pos + 4]
    if has_lat:
        s_scr, mx_scr = refs[pos + 4:]
    n_u = len(units)

    def pv(vt, p):
        ones = jnp.ones((SUM_ROWS, vt.shape[1]), BF16)
        return jnp.dot(jnp.concatenate([vt, ones], axis=0), p.astype(BF16),
                       preferred_element_type=F32)

    for u, (_, _, q_tiles) in enumerate(units):
        for t, (j, slot) in enumerate(q_tiles):
            tile = q_ref[:, j * LANES:(j + 1) * LANES]
            tile = jnp.where(_slot_mask(slot), tile, jnp.zeros_like(tile))
            qs_scr[u, :, t * tq:(t + 1) * tq] = tile.T

    for u, (half, v0, _) in enumerate(units):
        s = jnp.dot(kc_ref[:, half * LANES:(half + 1) * LANES], qs_scr[u],
                    preferred_element_type=F32)
        m = jnp.max(s, axis=0, keepdims=True)
        p = jnp.exp2(s - m)
        m_scr[u] = m
        acc_scr[u] = pv(vtc_ref[v0:v0 + dv, :], p)

    if has_lat:
        col_tiles = [slice(c0, c0 + MXU_DIM) for c0 in range(0, s_scr.shape[2], MXU_DIM)]

        key_halves = [slice(r0, r0 + MXU_DIM) for r0 in range(0, tk, MXU_DIM)]

        def scores(kb, u, slot, cs, rs):
            half = units[u][0]
            row0 = pl.multiple_of(kb * tk, tk) + rs.start
            k = kl_ref[pl.ds(row0, MXU_DIM), half * LANES:(half + 1) * LANES]
            s = jnp.dot(k, qs_scr[u, :, cs], preferred_element_type=F32)
            s_scr[slot, rs, cs] = s
            return jnp.max(s, axis=0, keepdims=True)

        def probs(u, slot, cs):
            m_prev = m_scr[u, :, cs]
            m_new = jnp.maximum(m_prev, mx_scr[slot, :, cs])
            m_scr[u, :, cs] = m_new
            return (jnp.exp2(s_scr[slot, :, cs] - m_new).astype(BF16),
                    jnp.exp2(m_prev - m_new))

        def pv_part(kb, u, p, rs):
            v0 = units[u][1]
            return pv(vtl_ref[kb, v0:v0 + dv, rs], p[rs, :])

        bpt = PIPE_SLOTS // n_u
        n_trips = n_kb // bpt

        def trip(it, last):
            pending = None
            for j in range(PIPE_SLOTS):
                ahead = j + LOOKAHEAD
                if ahead < PIPE_SLOTS:
                    sc = (it * bpt + ahead // n_u, ahead % n_u, ahead % SCORE_BUFS)
                elif not last:
                    a2 = ahead - PIPE_SLOTS
                    sc = ((it + 1) * bpt + a2 // n_u, a2 % n_u, a2 % SCORE_BUFS)
                else:
                    sc = None
                for cs in col_tiles:
                    fresh = (it * bpt + j // n_u, j % n_u, cs,
                             *probs(j % n_u, j % SCORE_BUFS, cs))
                    mx, parts = None, []
                    for rs in key_halves:
                        if sc is not None:
                            part = scores(*sc, cs, rs)
                            mx = part if mx is None else jnp.maximum(mx, part)
                        if pending is not None:
                            parts.append(pv_part(pending[0], pending[1], pending[3], rs))
                    if sc is not None:
                        mx_scr[sc[2], :, cs] = mx
                    if pending is not None:
                        _, u_p, cs_p, _, alpha_p = pending
                        acc_scr[u_p, :, cs_p] = alpha_p * acc_scr[u_p, :, cs_p] + sum(parts)
                    pending = fresh
            kb_p, u_p, cs_p, p, alpha_p = pending
            acc_scr[u_p, :, cs_p] = alpha_p * acc_scr[u_p, :, cs_p] + sum(
                pv_part(kb_p, u_p, p, rs) for rs in key_halves)

        for j in range(LOOKAHEAD):
            for cs in col_tiles:
                mx_scr[j, :, cs] = functools.reduce(
                    jnp.maximum, [scores(j // n_u, j % n_u, j, cs, rs) for rs in key_halves])
        lax.fori_loop(0, n_trips - 1, lambda it, c: (trip(it, False), c)[1], 0)
        trip(n_trips - 1, True)

    if mode == "diff":
        lp = lam_ref[...]
        lam = (jnp.exp(jnp.sum(lp[0:1] * lp[1:2], axis=1, keepdims=True))
               - jnp.exp(jnp.sum(lp[2:3] * lp[3:4], axis=1, keepdims=True)) + lam_init)
        gain = subln_ref[...]
        for u in range(n_u):
            o12 = acc_scr[u, :dv, :] / acc_scr[u, dv:dv + 1, :]
            o = o12[:, :tq] - lam * o12[:, tq:]
            o = o * lax.rsqrt(jnp.mean(o * o, axis=0, keepdims=True) + EPS)
            o_ref[u * dv:(u + 1) * dv, :] = ((o * gain) * (1.0 - lam_init)).astype(BF16)
    else:
        for u in range(n_u):
            m, l, acc = m_scr[u], acc_scr[u, dv:dv + 1, :], acc_scr[u, :dv, :]
            if has_sink:
                sk = sink_ref[u] * LOG2E
                m_f = jnp.maximum(m, sk)
                w = jnp.exp2(m - m_f)
                l = l * w + jnp.exp2(sk - m_f)
                acc = acc * w
            o = acc / l
            for g in range(GQA_GROUP):
                row = (u * GQA_GROUP + g) * HEAD_DIM
                o_ref[row:row + HEAD_DIM, :] = o[:, g * tq:(g + 1) * tq].astype(BF16)


def _flash_attn(q, kc, vtc, kl, vtl, *, mode, n_batch, tq, sink_rows=None, lam_params=None,
                subln=None, lam_init=0.0):
    r, dq = q.shape
    t_q = r // n_batch
    c = kc.shape[0] // n_batch
    wk = kc.shape[1]
    n_grp = wk // MXU_DIM
    q_cols = dq // n_grp
    units, dv = (_GQA_UNITS, HEAD_DIM) if mode == "gqa" else (_DIFF_UNITS, 2 * HEAD_DIM)
    cols = len(units[0][2]) * tq
    has_lat = kl is not None
    nq = t_q // tq
    in_specs = [
        pl.BlockSpec((tq, q_cols), lambda b, g, i: (b * nq + i, g)),
        pl.BlockSpec((c, MXU_DIM), lambda b, g, i: (b, g)),
        pl.BlockSpec((None, MXU_DIM, c), lambda b, g, i: (b, g, 0)),
    ]
    args = [q, kc, vtc]
    n_kb, tk = 0, 0
    scratch = [
        pltpu.VMEM((len(units), LANES, cols), BF16),
        pltpu.VMEM((len(units), 1, cols), F32),
        pltpu.VMEM((len(units), dv + SUM_ROWS, cols), F32),
    ]
    if has_lat:
        t = kl.shape[0] // n_batch
        n_kb, tk = vtl.shape[1], vtl.shape[3]
        in_specs += [
            pl.BlockSpec((t, MXU_DIM), lambda b, g, i: (b, g), pipeline_mode=pl.Buffered(1)),
            pl.BlockSpec((None, n_kb, MXU_DIM, tk), lambda b, g, i: (b, 0, g, 0),
                         pipeline_mode=pl.Buffered(1)),
        ]
        args += [kl, vtl]
        assert PIPE_SLOTS % len(units) == 0 and n_kb % (PIPE_SLOTS // len(units)) == 0
        scratch += [pltpu.VMEM((SCORE_BUFS, tk, cols), F32),
                    pltpu.VMEM((SCORE_BUFS, 1, cols), F32)]
    if mode == "diff":
        in_specs += [
            pl.BlockSpec((4, HEAD_DIM), lambda b, g, i: (0, 0)),
            pl.BlockSpec((dv, 1), lambda b, g, i: (0, 0)),
        ]
        args += [lam_params, subln.reshape(dv, 1)]
    elif sink_rows is not None:
        in_specs.append(pl.BlockSpec((N_KV_HEADS, 1, cols), lambda b, g, i: (0, 0, 0)))
        args.append(sink_rows)
    temps = c * cols * (4 + 4 + 2) + 4 * max(tk, c) * MXU_DIM * (4 + 4 + 2)
    return _pallas(
        functools.partial(_flash_kernel, units=units, dv=dv, tq=tq, tk=tk, n_kb=n_kb, mode=mode,
                          has_sink=sink_rows is not None, lam_init=lam_init),
        name=mode + "_attn", grid=(n_batch, n_grp, nq),
        sem=("parallel", "parallel", "arbitrary"), in_specs=in_specs, args=args,
        out_specs=pl.BlockSpec((None, q_cols, tq), lambda b, g, i: (b, g, i)),
        out_shape=jax.ShapeDtypeStruct((n_batch, dq, t_q), BF16),
        scratch=scratch, temps=temps)


def _window_kernel(q_ref, kc_ref, vtc_ref, kp_ref, kcur_ref, kn_ref, vtp_ref, vtcur_ref,
                   vtn_ref, sink_ref, o_ref, *, tq):
    qi = pl.program_id(1)
    nq = pl.num_programs(1)
    n_lat = tq + 2 * WINDOW
    rr = lax.broadcasted_iota(jnp.int32, (n_lat, 1), 0)
    cc = lax.broadcasted_iota(jnp.int32, (1, tq), 1)
    valid = jnp.abs(rr - WINDOW - cc) <= WINDOW
    valid &= (rr >= WINDOW) | (qi > 0)
    valid &= (rr < tq + WINDOW) | (qi < nq - 1)
    bias = jnp.where(valid, 0.0, NEG_INF)

    k_lat = [jnp.concatenate([r[:, h * LANES:(h + 1) * LANES] for r in (kp_ref, kcur_ref, kn_ref)],
                             axis=0) for h in range(2)]

    def pv(vt, p):
        ones = jnp.ones((SUM_ROWS, vt.shape[1]), BF16)
        return jnp.dot(jnp.concatenate([vt, ones], axis=0), p, preferred_element_type=F32)

    def scores(kvh, g):
        half = kvh // 2
        qt = _gqa_q_tile(q_ref, kvh, g).T
        s_ctx = jnp.dot(kc_ref[:, half * LANES:(half + 1) * LANES], qt,
                        preferred_element_type=F32)
        s_lat = jnp.dot(k_lat[half], qt, preferred_element_type=F32) + bias
        sk = sink_ref[kvh, :, g * tq:(g + 1) * tq] * LOG2E
        m = jnp.maximum(jnp.maximum(jnp.max(s_ctx, axis=0, keepdims=True),
                                    jnp.max(s_lat, axis=0, keepdims=True)), sk)
        return s_ctx, s_lat, m, jnp.exp2(sk - m)

    def probs(s_ctx, s_lat, m, p_sink):
        return jnp.exp2(s_ctx - m).astype(BF16), jnp.exp2(s_lat - m).astype(BF16), p_sink

    def output(kvh, g, p_ctx, p_lat, p_sink):
        hs = slice(kvh * HEAD_DIM, (kvh + 1) * HEAD_DIM)
        vt_lat = jnp.concatenate([vtp_ref[hs, :], vtcur_ref[hs, :], vtn_ref[hs, :]], axis=1)
        acc = pv(vtc_ref[hs, :], p_ctx) + pv(vt_lat, p_lat)
        l = acc[HEAD_DIM:HEAD_DIM + 1, :] + p_sink
        row = (kvh * GQA_GROUP + g) * HEAD_DIM
        o_ref[row:row + HEAD_DIM, :] = (acc[:HEAD_DIM, :] / l).astype(BF16)

    tiles = [(kvh, g) for kvh in range(N_KV_HEADS) for g in range(GQA_GROUP)]
    scored, pending = scores(*tiles[0]), None
    for t, tile in enumerate(tiles):
        ahead = scores(*tiles[t + 1]) if t + 1 < len(tiles) else None
        fresh = (*tile, *probs(*scored))
        if pending is not None:
            output(*pending)
        scored, pending = ahead, fresh
    output(*pending)


def _window_attn(q, kc, vtc, kl, vtl, sink_rows, *, n_batch, tq):
    r, dq = q.shape
    t = r // n_batch
    c = kc.shape[0] // n_batch
    wk = kc.shape[1]
    tkv = vtl.shape[3]
    nq = t // tq
    rb = tq // WINDOW
    nwb = t // WINDOW
    wpb, qpb = tkv // WINDOW, tkv // tq
    prev = lambda b, i: jnp.maximum(i * rb - 1, 0)
    nxt = lambda b, i: jnp.minimum((i + 1) * rb, nwb - 1)
    in_specs = [
        pl.BlockSpec((tq, dq), lambda b, i: (b * nq + i, 0)),
        pl.BlockSpec((c, wk), lambda b, i: (b, 0)),
        pl.BlockSpec((None, wk, c), lambda b, i: (b, 0, 0)),
        pl.BlockSpec((WINDOW, wk), lambda b, i: (b * nwb + prev(b, i), 0)),
        pl.BlockSpec((tq, wk), lambda b, i: (b * nq + i, 0)),
        pl.BlockSpec((WINDOW, wk), lambda b, i: (b * nwb + nxt(b, i), 0)),
        pl.BlockSpec((None, None, wk, WINDOW),
                     lambda b, i: (b, prev(b, i) // wpb, 0, prev(b, i) % wpb)),
        pl.BlockSpec((None, None, wk, tq), lambda b, i: (b, i // qpb, 0, i % qpb)),
        pl.BlockSpec((None, None, wk, WINDOW),
                     lambda b, i: (b, nxt(b, i) // wpb, 0, nxt(b, i) % wpb)),
        pl.BlockSpec((N_KV_HEADS, 1, GQA_GROUP * tq), lambda b, i: (0, 0, 0)),
    ]
    n_keys = c + tq + 2 * WINDOW
    temps = n_keys * GQA_GROUP * tq * (4 + 4 + 2) + GQA_GROUP * tq * LANES * 2
    return _pallas(
        functools.partial(_window_kernel, tq=tq), name="window_attn", grid=(n_batch, nq),
        sem=("parallel", "parallel"), in_specs=in_specs,
        args=[q, kc, vtc, kl, kl, kl, vtl, vtl, vtl, sink_rows],
        out_specs=pl.BlockSpec((None, dq, tq), lambda b, i: (b, 0, i)),
        out_shape=jax.ShapeDtypeStruct((n_batch, dq, t), BF16), temps=temps)


def _rope_tables(n_tokens):
    rows = n_tokens // GRID_W
    row = jnp.repeat(jnp.arange(rows), GRID_W)
    col = jnp.tile(jnp.arange(GRID_W), rows)
    inv_freq = ROPE_BASE ** (-jnp.arange(N_FREQ, dtype=F32) / N_FREQ)
    ang = jnp.stack([row, col], axis=-1).astype(F32)[:, :, None] * inv_freq
    cos, sin = jnp.cos(ang), jnp.sin(ang)
    cos_h = jnp.concatenate([cos, cos], axis=-1).reshape(n_tokens, HEAD_DIM)
    sin_h = jnp.concatenate([-sin, sin], axis=-1).reshape(n_tokens, HEAD_DIM)
    return jnp.tile(cos_h, (1, 2)), jnp.tile(sin_h, (1, 2))


def _gqa_weights(w_qkv):
    d = w_qkv.shape[0]
    n_q = N_KV_HEADS * GQA_GROUP * HEAD_DIM
    n_kv = N_KV_HEADS * HEAD_DIM
    wq = w_qkv[:, :n_q].reshape(d, N_KV_HEADS // 2, 2, GQA_GROUP, HEAD_DIM)
    wq = wq.transpose(0, 1, 3, 2, 4).reshape(d, n_q)
    wk = w_qkv[:, n_q:n_q + n_kv]
    wvt = w_qkv[:, n_q + n_kv:].T
    return wq.astype(BF16), wk.astype(BF16), wvt.astype(BF16)


def _diff_weights(w_qkv):
    n = w_qkv.shape[1] // 3
    return (w_qkv[:, :n].astype(BF16), w_qkv[:, n:2 * n].astype(BF16),
            w_qkv[:, 2 * n:].T.astype(BF16))


def _lane_gain(g):
    return jnp.tile(g.reshape(1, HEAD_DIM), (1, LANES // HEAD_DIM))


def _sink_rows(sink, tq):
    return jnp.repeat(sink.reshape(N_KV_HEADS, 1, GQA_GROUP, 1), tq, axis=3).reshape(
        N_KV_HEADS, 1, GQA_GROUP * tq)


def kernel(x, c, ctx, c_ctx, norm_g, w_ada, b_ada, w_ffn_in, w_ffn_out, w_o, w_qkv_a, qk_norm_a,
           w_qkv_b, qk_norm_b, sink_b, w_qkv_c, qk_norm_c, diff_lambda, diff_subln):
    n_b, t, d = x.shape
    n_c = ctx.shape[1]
    depth = w_ada.shape[0]
    tm = 512
    tm_ffn = 1024
    tq, tq_diff = 1024, 2048
    tq_win = 256
    tq_c = n_c
    tkv = 512
    assert n_b + 1 <= SUBLANES and n_c % LANES == 0
    assert all(t % tile == 0 for tile in (tm, tm_ffn, tq, tq_diff, tq_win, tkv, GRID_W))

    c8 = jnp.zeros((SUBLANES, d), F32).at[:n_b].set(c).at[n_b].set(c_ctx)
    mod5 = _ada_all(c8, w_ada, b_ada).reshape(depth, SUBLANES, 3, 3, d)

    rope_tabs = _rope_tables(t)
    eye = jnp.arange(MXU_DIM) // HEAD_DIM
    bd = (eye[:, None] == eye[None, :]).astype(BF16)
    w_in = w_ffn_in.astype(BF16)
    w_out = w_ffn_out.astype(BF16)
    wo = w_o.astype(BF16)
    norm_g4 = norm_g.reshape(depth, 3, 1, d)

    xs = x.reshape(n_b * t, d)
    cs = ctx.reshape(n_b * n_c, d)
    lat = dict(tm=tm, rows_per_batch=t, ctx_row=None)
    lat_ffn = dict(tm=tm_ffn, rows_per_batch=t, ctx_row=None)
    cx = dict(tm=n_c, rows_per_batch=n_c, ctx_row=n_b)

    for i in range(depth):
        last = i == depth - 1
        kind, j = i % 3, i // 3
        xs = _ffn(xs, mod5, norm_g4, w_in, w_out, i, 0, **lat_ffn)
        cs = _ffn(cs, mod5, norm_g4, w_in, w_out, i, 0, **cx)
        if kind == 2:
            wq, wk, wvt = _diff_weights(w_qkv_c[j])
            qk_g = qk_norm_c[j]
        else:
            wq, wk, wvt = _gqa_weights((w_qkv_a, w_qkv_b)[kind][j])
            qk_g = (qk_norm_a, qk_norm_b)[kind][j]
        gq, gk = _lane_gain(qk_g[0]), _lane_gain(qk_g[1])
        q, kl, vtl = _proj(xs, mod5, norm_g4, i, wq, wk, wvt, gq, gk, bd, rope_tabs, tkv=tkv,
                           **lat)
        qc, kc, vtc = _proj(cs, mod5, norm_g4, i, wq, wk, wvt, gq, gk, bd, None, tkv=n_c, **cx)
        vtc = vtc.reshape(n_b, vtc.shape[2], n_c)
        if kind == 0:
            ot = _flash_attn(q, kc, vtc, kl, vtl, mode="gqa", n_batch=n_b, tq=tq)
            if not last:
                otc = _flash_attn(qc, kc, vtc, None, None, mode="gqa", n_batch=n_b, tq=tq_c)
        elif kind == 1:
            ot = _window_attn(q, kc, vtc, kl, vtl, _sink_rows(sink_b[j], tq_win), n_batch=n_b,
                              tq=tq_win)
            if not last:
                otc = _flash_attn(qc, kc, vtc, None, None, mode="gqa", n_batch=n_b, tq=tq_c,
                                  sink_rows=_sink_rows(sink_b[j], tq_c))
        else:
            extra = dict(lam_params=diff_lambda[j], subln=diff_subln[j],
                         lam_init=0.8 - 0.6 * math.exp(-0.3 * i))
            ot = _flash_attn(q, kc, vtc, kl, vtl, mode="diff", n_batch=n_b, tq=tq_diff, **extra)
            if not last:
                otc = _flash_attn(qc, kc, vtc, None, None, mode="diff", n_batch=n_b, tq=tq_c,
                                  **extra)
        xs = _ffn(xs, mod5, norm_g4, w_in, w_out, i, 1, pre=(ot, wo), **lat_ffn)
        if not last:
            cs = _ffn(cs, mod5, norm_g4, w_in, w_out, i, 1, pre=(otc, wo), **cx)
    return xs.reshape(n_b, t, d)
```
